```python
import math
import jax, jax.numpy as jnp
from jax import lax
import numpy as np

D_MODEL = 1024
BATCH = 4
SEQ = 4096
DEPTH = 2
DEC_BATCH = 8
DEC_SEQ = 4096
PAST_LEN = 128

ROPE_THETA = 10000.0
EPS = 1e-6
NEG_INF = -1e30
QBLK = 128
N_BRANCH = 4

A_HEADS = 4
A_HD = 64
A_QK_W = A_HEADS * 2 * A_HD
A_V_W = A_HEADS * 2 * A_HD
A_OUT = A_V_W
B_PATTERNS = ((128, 1), (512, 4), (2048, 16))
B_HEADS = 4
B_HD = 64
B_BLK = 64
B_W = B_HEADS * B_HD
B_OUT = B_W
C_HEADS = 8
C_Q_RANK = 256
C_KV_RANK = 128
C_NOPE = 64
C_ROPE = 32
C_VD = 64
C_OUT = C_HEADS * C_VD
D_QHEADS = 8
D_KVHEADS = 2
D_HD = 64
D_WIN = 128
D_BLK = 128
D_Q_W = D_QHEADS * D_HD
D_KV_W = D_KVHEADS * D_HD
D_OUT = D_Q_W
D_FF = -(-8 * D_MODEL // (3 * 256)) * 256

IN_SIZES = (A_QK_W, A_QK_W, A_V_W) + (B_W,) * (3 * len(B_PATTERNS)) + (C_Q_RANK, C_KV_RANK, C_ROPE, D_Q_W, D_KV_W, D_KV_W)
N_IN = sum(IN_SIZES)
SPLIT_IDX = tuple(int(i) for i in np.cumsum(IN_SIZES)[:-1])

kernel_name = "hybrid_gated_encoder_4mixer"


def lambda_init(layer):
    return 0.8 - 0.6 * math.exp(-0.3 * layer)


def rms_norm(x, g):
    xf = x.astype(jnp.float32)
    y = xf * lax.rsqrt(jnp.mean(xf * xf, axis=-1, keepdims=True) + EPS)
    return (y * g.astype(jnp.float32)).astype(x.dtype)


def rope(x, pos):
    half = x.shape[-1] // 2
    inv = jnp.power(ROPE_THETA, -jnp.arange(half, dtype=jnp.float32) / half)
    ang = pos.astype(jnp.float32)[:, None] * inv[None, :]
    cos = jnp.cos(ang)[None, :, None, :]
    sin = jnp.sin(ang)[None, :, None, :]
    xf = x.astype(jnp.float32)
    x1, x2 = xf[..., :half], xf[..., half:]
    return jnp.concatenate([x1 * cos - x2 * sin, x2 * cos + x1 * sin], axis=-1).astype(x.dtype)


def _qblocks(t):
    b, s = t.shape[:2]
    return jnp.moveaxis(t.reshape(b, s // QBLK, QBLK, *t.shape[2:]), 1, 0)


def _unblock(t):
    t = jnp.moveaxis(t, 0, 1)
    return t.reshape(t.shape[0], -1, *t.shape[3:])


def _to_residues(t, dil):
    b, s = t.shape[:2]
    return t.reshape(b, s // dil, dil, *t.shape[2:]).swapaxes(1, 2).reshape(b * dil, s // dil, *t.shape[2:])


def _from_residues(t, dil, b):
    n, m = t.shape[:2]
    return t.reshape(b, dil, m, *t.shape[2:]).swapaxes(1, 2).reshape(b, dil * m, *t.shape[2:])


def banded_attention(q, k, v, half_w, blk, sink=None):
    n, L, hq, d = q.shape
    hk = k.shape[2]
    rep = hq // hk
    nblk = -(-L // blk)
    lp = nblk * blk
    nb = -(-half_w // blk)
    kw_len = (2 * nb + 1) * blk
    q = jnp.pad(q, ((0, 0), (0, lp - L), (0, 0), (0, 0)))
    kv_pad = ((0, 0), (nb * blk, lp - L + nb * blk), (0, 0), (0, 0))
    kb = jnp.pad(k, kv_pad).reshape(n, nblk + 2 * nb, blk, hk, d)
    vb = jnp.pad(v, kv_pad).reshape(n, nblk + 2 * nb, blk, hk, d)
    kw = jnp.concatenate([kb[:, j:j + nblk] for j in range(2 * nb + 1)], axis=2)
    vw = jnp.concatenate([vb[:, j:j + nblk] for j in range(2 * nb + 1)], axis=2)
    qb = q.reshape(n, nblk, blk, hk, rep, d)
    s = jnp.einsum("nbqgrd,nbkgd->nbgrqk", qb, kw).astype(jnp.float32) * (d ** -0.5)
    qpos = jnp.arange(lp).reshape(nblk, blk)
    kpos = (jnp.arange(nblk)[:, None] - nb) * blk + jnp.arange(kw_len)[None, :]
    valid = (jnp.abs(qpos[:, :, None] - kpos[:, None, :]) <= half_w) & ((kpos >= 0) & (kpos < L))[:, None, :]
    s = jnp.where(valid[None, :, None, None], s, NEG_INF)
    m = jnp.max(s, axis=-1, keepdims=True)
    if sink is not None:
        sk = sink.astype(jnp.float32).reshape(1, 1, hk, rep, 1, 1)
        m = jnp.maximum(m, sk)
        e = jnp.exp(s - m)
        l = jnp.sum(e, axis=-1, keepdims=True) + jnp.exp(sk - m)
    else:
        e = jnp.exp(s - m)
        l = jnp.sum(e, axis=-1, keepdims=True)
    p = e / l
    o = jnp.einsum("nbgrqk,nbkgd->nbqgrd", p.astype(v.dtype), vw).reshape(n, lp, hq, d)[:, :L]
    lse = (m + jnp.log(l))[..., 0]
    lse = jnp.moveaxis(lse, 4, 2).reshape(n, lp, hq)[:, :L]
    return o, lse


def diff_attention(a_q, a_k, a_v, qn_g, kn_g, lam_p, subln_g, lam_init, pos):
    b, s, _ = a_q.shape
    q = rope(rms_norm(a_q.reshape(b, s, 2 * A_HEADS, A_HD), qn_g), pos).reshape(b, s, A_HEADS, 2, A_HD)
    k = rope(rms_norm(a_k.reshape(b, s, 2 * A_HEADS, A_HD), kn_g), pos).reshape(b, s, A_HEADS, 2, A_HD)
    v = a_v.reshape(b, s, A_HEADS, 2 * A_HD)
    lp = lam_p.astype(jnp.float32)
    lam = jnp.exp(jnp.sum(lp[0] * lp[1])) - jnp.exp(jnp.sum(lp[2] * lp[3])) + lam_init

    def block(qb):
        sc = jnp.einsum("bqhcd,bkhcd->bhcqk", qb, k).astype(jnp.float32) * (A_HD ** -0.5)
        p = jax.nn.softmax(sc, axis=-1)
        w = p[:, :, 0] - lam * p[:, :, 1]
        return jnp.einsum("bhqk,bkhe->bqhe", w.astype(v.dtype), v)

    o = _unblock(lax.map(block, _qblocks(q)))
    o = rms_norm(o, subln_g) * (1.0 - lam_init)
    return o.reshape(b, s, A_OUT)


def dilated_attention(cols, qn_g, kn_g, pos):
    b, s, _ = cols[0].shape
    outs, lses = [], []
    for g, (window, dil) in enumerate(B_PATTERNS):
        q, k, v = (c.reshape(b, s, B_HEADS, B_HD) for c in cols[3 * g:3 * g + 3])
        q = rope(rms_norm(q, qn_g[g]), pos)
        k = rope(rms_norm(k, kn_g[g]), pos)
        o, lse = banded_attention(_to_residues(q, dil), _to_residues(k, dil), _to_residues(v, dil),
                                  window // (2 * dil), B_BLK)
        outs.append(_from_residues(o, dil, b))
        lses.append(_from_residues(lse, dil, b))
    wts = jax.nn.softmax(jnp.stack(lses, axis=0), axis=0)
    out = jnp.einsum("gbsh,gbshd->bshd", wts, jnp.stack(outs, axis=0).astype(jnp.float32))
    return out.astype(cols[0].dtype).reshape(b, s, B_OUT)


def mla_attention(c_q, c_kv, k_rope, qa_g, kva_g, w_uq, w_ukv, qn_g, kn_g, pos):
    b, s, _ = c_q.shape
    q = (rms_norm(c_q, qa_g) @ w_uq).reshape(b, s, C_HEADS, C_NOPE + C_ROPE)
    kv = (rms_norm(c_kv, kva_g) @ w_ukv).reshape(b, s, C_HEADS, C_NOPE + C_VD)
    v = kv[..., C_NOPE:]
    k = jnp.concatenate([kv[..., :C_NOPE], jnp.broadcast_to(k_rope[:, :, None, :], (b, s, C_HEADS, C_ROPE))], axis=-1)
    q = rms_norm(q, qn_g)
    k = rms_norm(k, kn_g)
    q = jnp.concatenate([q[..., :C_NOPE], rope(q[..., C_NOPE:], pos)], axis=-1)
    k = jnp.concatenate([k[..., :C_NOPE], rope(k[..., C_NOPE:], pos)], axis=-1)
    scale = (C_NOPE + C_ROPE) ** -0.5

    def block(qb):
        sc = jnp.einsum("bqhd,bkhd->bhqk", qb, k).astype(jnp.float32) * scale
        p = jax.nn.softmax(sc, axis=-1)
        return jnp.einsum("bhqk,bkhe->bqhe", p.astype(v.dtype), v)

    o = _unblock(lax.map(block, _qblocks(q)))
    return o.reshape(b, s, C_OUT)


def window_gqa_sink(d_q, d_k, d_v, qn_g, kn_g, sink, pos):
    b, s, _ = d_q.shape
    q = rope(rms_norm(d_q.reshape(b, s, D_QHEADS, D_HD), qn_g), pos)
    k = rope(rms_norm(d_k.reshape(b, s, D_KVHEADS, D_HD), kn_g), pos)
    v = d_v.reshape(b, s, D_KVHEADS, D_HD)
    o, _ = banded_attention(q, k, v, D_WIN, D_BLK, sink)
    return o.reshape(b, s, D_OUT)


def encoder_layer(x, lam_init, norm1_g, w_in, w_gate, a_qnorm_g, a_knorm_g, a_lambda, a_subln_g,
                  b_qnorm_g, b_knorm_g, c_qa_norm_g, c_kva_norm_g, c_w_uq, c_w_ukv, c_qnorm_g, c_knorm_g,
                  d_qnorm_g, d_knorm_g, d_sink, w_br_a, w_br_b, w_br_c, w_br_d, w_o,
                  norm2_g, w_ffn_gate, w_ffn_up, w_ffn_down):
    b, s, _ = x.shape
    pos = jnp.arange(s)
    h = rms_norm(x, norm1_g)
    cols = jnp.split(h @ w_in, SPLIT_IDX, axis=-1)
    y_a = diff_attention(cols[0], cols[1], cols[2], a_qnorm_g, a_knorm_g, a_lambda, a_subln_g, lam_init, pos) @ w_br_a
    y_b = dilated_attention(cols[3:12], b_qnorm_g, b_knorm_g, pos) @ w_br_b
    y_c = mla_attention(cols[12], cols[13], cols[14], c_qa_norm_g, c_kva_norm_g, c_w_uq, c_w_ukv,
                        c_qnorm_g, c_knorm_g, pos) @ w_br_c
    y_d = window_gqa_sink(cols[15], cols[16], cols[17], d_qnorm_g, d_knorm_g, d_sink, pos) @ w_br_d
    gates = jax.nn.sigmoid((h @ w_gate).astype(jnp.float32)).reshape(b, s, N_BRANCH, D_MODEL)
    branches = jnp.stack([y_a, y_b, y_c, y_d], axis=2).astype(jnp.float32)
    merged = jnp.sum(gates * branches, axis=2).astype(x.dtype)
    x = x + merged @ w_o
    hf = rms_norm(x, norm2_g)
    x = x + (jax.nn.silu(hf @ w_ffn_gate) * (hf @ w_ffn_up)) @ w_ffn_down
    return x


def _trunk(x, params):
    for l in range(DEPTH):
        x = encoder_layer(x, lambda_init(l), *[p[l] for p in params])
    return x


def setup_inputs(seed: int = 0) -> dict:
    key = jax.random.key(seed)
    ks = iter(jax.random.split(key, 40))

    def nrm(shape, scale):
        return jax.random.normal(next(ks), shape, jnp.float32) * scale

    def gain(shape):
        return 1.0 + nrm(shape, 0.02)

    L = DEPTH
    return {
        "x_prompt": nrm((BATCH, SEQ, D_MODEL), 1.0),
        "x_sample": nrm((DEC_BATCH, DEC_SEQ, D_MODEL), 1.0),
        "norm1_g": gain((L, D_MODEL)),
        "w_in": nrm((L, D_MODEL, N_IN), D_MODEL ** -0.5),
        "w_gate": nrm((L, D_MODEL, N_BRANCH * D_MODEL), D_MODEL ** -0.5),
        "a_qnorm_g": gain((L, A_HD)),
        "a_knorm_g": gain((L, A_HD)),
        "a_lambda": nrm((L, 4, A_HD), 0.1),
        "a_subln_g": gain((L, 2 * A_HD)),
        "b_qnorm_g": gain((L, len(B_PATTERNS), B_HD)),
        "b_knorm_g": gain((L, len(B_PATTERNS), B_HD)),
        "c_qa_norm_g": gain((L, C_Q_RANK)),
        "c_kva_norm_g": gain((L, C_KV_RANK)),
        "c_w_uq": nrm((L, C_Q_RANK, C_HEADS * (C_NOPE + C_ROPE)), C_Q_RANK ** -0.5),
        "c_w_ukv": nrm((L, C_KV_RANK, C_HEADS * (C_NOPE + C_VD)), C_KV_RANK ** -0.5),
        "c_qnorm_g": gain((L, C_NOPE + C_ROPE)),
        "c_knorm_g": gain((L, C_NOPE + C_ROPE)),
        "d_qnorm_g": gain((L, D_HD)),
        "d_knorm_g": gain((L, D_HD)),
        "d_sink": nrm((L, D_QHEADS), 0.5),
        "w_br_a": nrm((L, A_OUT, D_MODEL), A_OUT ** -0.5),
        "w_br_b": nrm((L, B_OUT, D_MODEL), B_OUT ** -0.5),
        "w_br_c": nrm((L, C_OUT, D_MODEL), C_OUT ** -0.5),
        "w_br_d": nrm((L, D_OUT, D_MODEL), D_OUT ** -0.5),
        "w_o": nrm((L, D_MODEL, D_MODEL), D_MODEL ** -0.5),
        "norm2_g": gain((L, D_MODEL)),
        "w_ffn_gate": nrm((L, D_MODEL, D_FF), D_MODEL ** -0.5),
        "w_ffn_up": nrm((L, D_MODEL, D_FF), D_MODEL ** -0.5),
        "w_ffn_down": nrm((L, D_FF, D_MODEL), D_FF ** -0.5),
    }


def reference(x_prompt, x_sample, norm1_g, w_in, w_gate, a_qnorm_g, a_knorm_g, a_lambda, a_subln_g,
              b_qnorm_g, b_knorm_g, c_qa_norm_g, c_kva_norm_g, c_w_uq, c_w_ukv, c_qnorm_g, c_knorm_g,
              d_qnorm_g, d_knorm_g, d_sink, w_br_a, w_br_b, w_br_c, w_br_d, w_o,
              norm2_g, w_ffn_gate, w_ffn_up, w_ffn_down):
    params = (norm1_g, w_in, w_gate, a_qnorm_g, a_knorm_g, a_lambda, a_subln_g,
              b_qnorm_g, b_knorm_g, c_qa_norm_g, c_kva_norm_g, c_w_uq, c_w_ukv, c_qnorm_g, c_knorm_g,
              d_qnorm_g, d_knorm_g, d_sink, w_br_a, w_br_b, w_br_c, w_br_d, w_o,
              norm2_g, w_ffn_gate, w_ffn_up, w_ffn_down)
    y_prompt = _trunk(x_prompt, params)
    y_sample = _trunk(x_sample, params)
    return (y_prompt, y_sample)
```

```python
import functools
import math

import jax
import jax.numpy as jnp
import numpy as np
from jax import lax
from jax.experimental import pallas as pl
from jax.experimental.pallas import tpu as pltpu

D_MODEL = 1024
SEQ = 4096
DEPTH = 2
ROPE_THETA = 10000.0
EPS = 1e-6
NEG_INF = -1e30
N_BRANCH = 4
LOG2E = 1.4426950408889634
LN2 = 0.6931471805599453

A_HEADS = 4
A_HD = 64
B_PATTERNS = ((128, 1), (512, 4), (2048, 16))
B_HEADS = 4
B_HD = 64
B_W = B_HEADS * B_HD
C_HEADS = 8
C_Q_RANK = 256
C_KV_RANK = 128
C_NOPE = 64
C_ROPE = 32
C_VD = 64
D_QHEADS = 8
D_KVHEADS = 2
D_HD = 64
D_WIN = 128
D_FF = -(-8 * D_MODEL // (3 * 256)) * 256

IN_SIZES = (512, 512, 512) + (B_W,) * 9 + (C_Q_RANK, C_KV_RANK, C_ROPE, 512, 128, 128)
SPLIT_IDX = tuple(int(i) for i in np.cumsum(IN_SIZES)[:-1])

LANES = 128
MXU_N = 256

OFF_AQ, OFF_AK, OFF_AV = 0, 512, 1024
OFF_B = 1536
OFF_CQ = 3840
OFF_CKV = 4096
OFF_KR = 4224
OFF_DQ = 4352
OFF_DK = 4864
OFF_DV = 5120
N_PROJ = 5376

VMEM_LIMIT = 56 * 1024 * 1024

TM_PROJ = 256
TQ_A = 256
TQ_C = 256
TM_MERGE = 256
TM_FFN = 256
FFN_CHUNKS = ((0, 1024), (1024, 1024), (2048, 768))


def lambda_init(layer):
    return 0.8 - 0.6 * math.exp(-0.3 * layer)


def _const_spec(shape):
    nd = len(shape)
    return pl.BlockSpec(shape, lambda *_: (0,) * nd, pipeline_mode=pl.Buffered(1))


def _params(n_grid):
    return pltpu.CompilerParams(dimension_semantics=("arbitrary",) * n_grid,
                                vmem_limit_bytes=VMEM_LIMIT)


def _dot(a, b):
    return jnp.dot(a, b, preferred_element_type=jnp.float32)


def _dot_nt(a, b):
    return lax.dot_general(a, b, (((1,), (1,)), ((), ())), preferred_element_type=jnp.float32)


def _rms(x, g):
    ms = jnp.mean(x * x, axis=-1, keepdims=True)
    return x * lax.rsqrt(ms + EPS) * g


def _group_sumsq(p, bd):
    outs = []
    for c in range(p.shape[1] // MXU_N):
        pc = p[:, c * MXU_N:(c + 1) * MXU_N]
        p2 = pc * pc
        hi = p2.astype(jnp.bfloat16)
        lo = (p2 - hi.astype(jnp.float32)).astype(jnp.bfloat16)
        outs.append(_dot(hi, bd) + _dot(lo, bd))
    return outs[0] if len(outs) == 1 else jnp.concatenate(outs, axis=-1)


def _rope_chunks(y, cos, sin, first_half, shift):
    outs = []
    for c in range(y.shape[1] // LANES):
        yc = y[:, c * LANES:(c + 1) * LANES]
        sw = jnp.where(first_half, pltpu.roll(yc, LANES - shift, 1), pltpu.roll(yc, shift, 1))
        outs.append(yc * cos + sw * sin)
    return outs[0] if len(outs) == 1 else jnp.concatenate(outs, axis=-1)


def _proj_kernel(x_ref, g1_ref, w_ref, grow_ref, bd_ref, cs_ref, qag_ref, kvag_ref,
                 wuq_ref, wuk_ref, wuv_ref, gqc_ref, gkc_ref,
                 qa, ka, va, qb0, kb0, vb0, qb1, kb1, vb1, qb2, kb2, vb2,
                 qc, kc, vc, qd, kd, vd):
    tm = x_ref.shape[0]
    h = _rms(x_ref[...], g1_ref[...]).astype(jnp.bfloat16)
    bd64 = bd_ref[0]
    bd128 = bd_ref[1]
    cos64, sin64, cosc, sinc = cs_ref[0], cs_ref[1], cs_ref[2], cs_ref[3]
    lane = lax.broadcasted_iota(jnp.int32, (tm, LANES), 1)
    first64 = (lane & 63) < 32
    firstc = lane < (C_NOPE + C_ROPE // 2)

    def proj(off, width):
        return _dot(h, w_ref[:, off:off + width])

    def norm_rope64(off, width, out_ref):
        p = proj(off, width)
        ss = _group_sumsq(p, bd64)
        y = p * lax.rsqrt(ss * (1.0 / 64) + EPS) * grow_ref[:, off:off + width]
        out_ref[...] = _rope_chunks(y, cos64, sin64, first64, 32).astype(out_ref.dtype)

    def plain(off, width, out_ref):
        out_ref[...] = proj(off, width).astype(out_ref.dtype)

    norm_rope64(OFF_AQ, 512, qa)
    norm_rope64(OFF_AK, 512, ka)
    plain(OFF_AV, 512, va)
    for g, (qo, ko, vo) in enumerate(((qb0, kb0, vb0), (qb1, kb1, vb1), (qb2, kb2, vb2))):
        base = OFF_B + g * 768
        norm_rope64(base, 256, qo)
        norm_rope64(base + 256, 256, ko)
        plain(base + 512, 256, vo)
    norm_rope64(OFF_DQ, 512, qd)
    norm_rope64(OFF_DK, 256, kd)
    plain(OFF_DV, 256, vd)

    cqn = _rms(proj(OFF_CQ, C_Q_RANK), qag_ref[...]).astype(jnp.bfloat16)
    qfull = _dot(cqn, wuq_ref[...])
    qn = qfull * lax.rsqrt(_group_sumsq(qfull, bd128) * (1.0 / 96) + EPS) * gqc_ref[...]
    qc[...] = _rope_chunks(qn, cosc, sinc, firstc, C_ROPE // 2).astype(qc.dtype)
    ckvn = _rms(proj(OFF_CKV, C_KV_RANK), kvag_ref[...]).astype(jnp.bfloat16)
    kr = proj(OFF_KR, LANES)
    kfull = _dot(ckvn, wuk_ref[...]) + jnp.concatenate([kr] * C_HEADS, axis=-1)
    kn = kfull * lax.rsqrt(_group_sumsq(kfull, bd128) * (1.0 / 96) + EPS) * gkc_ref[...]
    kc[...] = _rope_chunks(kn, cosc, sinc, firstc, C_ROPE // 2).astype(kc.dtype)
    vc[...] = _dot(ckvn, wuv_ref[...]).astype(vc.dtype)


def _proj_call(x, lw, tables):
    t = x.shape[0]
    tm = TM_PROJ
    n_pos = SEQ // tm
    widths = (512, 512, 512) + (256,) * 9 + (1024, 1024, 512, 512, 256, 256)
    out_shape = [jax.ShapeDtypeStruct((t, w), jnp.bfloat16) for w in widths]
    out_specs = [pl.BlockSpec((tm, w), lambda i: (i, 0)) for w in widths]
    in_specs = [
        pl.BlockSpec((tm, D_MODEL), lambda i: (i, 0)),
        _const_spec((1, D_MODEL)),
        _const_spec((D_MODEL, N_PROJ)),
        _const_spec((1, N_PROJ)),
        _const_spec((2, MXU_N, MXU_N)),
        pl.BlockSpec((4, tm, LANES), lambda i: (0, i % n_pos, 0)),
        _const_spec((1, C_Q_RANK)),
        _const_spec((1, C_KV_RANK)),
        _const_spec((C_Q_RANK, 1024)),
        _const_spec((C_KV_RANK, 1024)),
        _const_spec((C_KV_RANK, 512)),
        _const_spec((1, 1024)),
        _const_spec((1, 1024)),
    ]
    return pl.pallas_call(
        _proj_kernel, grid=(t // tm,), in_specs=in_specs, out_specs=out_specs, out_shape=out_shape,
        compiler_params=_params(1), name="proj",
    )(x, lw["g1"], lw["w_in"], lw["grow"], tables["bd"], tables["cs"], lw["qag"], lw["kvag"],
      lw["wuq"], lw["wuk"], lw["wuv"], lw["gqc"], lw["gkc"])


def _attn_a_kernel(lam_ref, q_ref, k_ref, v_ref, g_ref, o_ref, *, lam_init):
    lp = lam_ref[...]
    lam = (jnp.exp(jnp.sum(lp[0:1] * lp[1:2], axis=-1, keepdims=True))
           - jnp.exp(jnp.sum(lp[2:3] * lp[3:4], axis=-1, keepdims=True)) + lam_init)
    q = q_ref[...]
    k = k_ref[...]
    lane = lax.broadcasted_iota(jnp.int32, q.shape, 1)
    zero = jnp.zeros_like(q)
    es, inv = [], []
    for c in range(2):
        qc = jnp.where((lane >= c * A_HD) & (lane < (c + 1) * A_HD), q, zero)
        s = _dot_nt(qc, k)
        m = jnp.max(s, axis=-1, keepdims=True)
        e = jnp.exp2(s - m)
        es.append(e)
        inv.append(1.0 / jnp.sum(e, axis=-1, keepdims=True))
    w = es[0] * inv[0] - es[1] * (lam * inv[1])
    o = _dot(w.astype(jnp.bfloat16), v_ref[...])
    o = _rms(o, g_ref[...]) * (1.0 - lam_init)
    o_ref[...] = o.astype(o_ref.dtype)


def _attn_a_call(q, k, v, a_lambda, subln_row, lam_init):
    t = q.shape[0]
    nseq = t // SEQ
    tq = TQ_A
    nq = SEQ // tq
    return pl.pallas_call(
        functools.partial(_attn_a_kernel, lam_init=lam_init),
        grid=(nseq, A_HEADS, nq),
        in_specs=[
            _const_spec((4, A_HD)),
            pl.BlockSpec((tq, LANES), lambda s, h, i: (s * nq + i, h)),
            pl.BlockSpec((SEQ, LANES), lambda s, h, i: (s, h)),
            pl.BlockSpec((SEQ, LANES), lambda s, h, i: (s, h)),
            _const_spec((1, LANES)),
        ],
        out_specs=pl.BlockSpec((tq, LANES), lambda s, h, i: (s * nq + i, h)),
        out_shape=jax.ShapeDtypeStruct((t, A_HEADS * LANES), jnp.bfloat16),
        compiler_params=_params(3), name="attn_a",
    )(a_lambda, q, k, v, subln_row)


def _attn_c_kernel(q_ref, k_ref, v_ref, o_ref):
    v = v_ref[...]
    lane = lax.broadcasted_iota(jnp.int32, o_ref.shape, 1)
    outs = []
    for hh in range(2):
        q = q_ref[:, hh * LANES:(hh + 1) * LANES]
        k = k_ref[:, hh * LANES:(hh + 1) * LANES]
        s = _dot_nt(q, k)
        m = jnp.max(s, axis=-1, keepdims=True)
        e = jnp.exp2(s - m)
        inv = 1.0 / jnp.sum(e, axis=-1, keepdims=True)
        outs.append(_dot(e.astype(jnp.bfloat16), v) * inv)
    o_ref[...] = jnp.where(lane < C_VD, outs[0], outs[1]).astype(o_ref.dtype)


def _attn_c_call(q, k, v):
    t = q.shape[0]
    nseq = t // SEQ
    tq = TQ_C
    nq = SEQ // tq
    return pl.pallas_call(
        _attn_c_kernel,
        grid=(nseq, C_HEADS // 2, nq),
        in_specs=[
            pl.BlockSpec((tq, 2 * LANES), lambda s, j, i: (s * nq + i, j)),
            pl.BlockSpec((SEQ, 2 * LANES), lambda s, j, i: (s, j)),
            pl.BlockSpec((SEQ, LANES), lambda s, j, i: (s, j)),
        ],
        out_specs=pl.BlockSpec((tq, LANES), lambda s, j, i: (s * nq + i, j)),
        out_shape=jax.ShapeDtypeStruct((t, C_HEADS * C_VD), jnp.bfloat16),
        compiler_params=_params(3), name="attn_c",
    )(q, k, v)


def _band_kernel(*refs, heads, bw, bq, hw, seg_len, with_sink, with_lse):
    if with_sink:
        sink_ref, q_ref, k_ref, v_ref = refs[:4]
        outs = refs[4:]
    else:
        q_ref, k_ref, v_ref = refs[:3]
        outs = refs[3:]
    o_ref = outs[0]
    lse_ref = outs[1] if with_lse else None
    ch = q_ref.shape[0]
    win = bq + 2 * hw
    shift = int(math.log2(seg_len))
    lane = lax.broadcasted_iota(jnp.int32, (bq, bw), 1)
    row = lax.broadcasted_iota(jnp.int32, (bq, win), 0)
    col = lax.broadcasted_iota(jnp.int32, (bq, win), 1)
    q_offs = sorted({hd[0] for hd in heads})

    def body(i, carry):
        q0 = pl.multiple_of(i * bq, bq)
        ws = pl.multiple_of(jnp.clip(q0 - hw, 0, ch - win), hw)
        qi = q0 + row
        kj = ws + col
        d = qi - kj
        valid = (d <= hw) & (d >= -hw) & ((qi >> shift) == (kj >> shift))
        bias = jnp.where(valid, 0.0, NEG_INF)
        qb = q_ref[pl.ds(q0, bq), :]
        kw = k_ref[pl.ds(ws, win), :]
        vw = v_ref[pl.ds(ws, win), :]
        acc = {qo: jnp.zeros((bq, bw), jnp.float32) for qo in q_offs}
        lacc = {qo: jnp.zeros((bq, bw), jnp.float32) for qo in q_offs}
        for qo, mo, ko, hid in heads:
            hm = (lane >= mo) & (lane < mo + 64)
            qblk = qb[:, qo:qo + bw]
            qh = jnp.where(hm, qblk, jnp.zeros_like(qblk))
            s = _dot_nt(qh, kw[:, ko:ko + bw]) + bias
            m = jnp.max(s, axis=-1, keepdims=True)
            if with_sink:
                sk = sink_ref[hid] * LOG2E
                m = jnp.maximum(m, sk)
            e = jnp.exp2(s - m)
            l = jnp.sum(e, axis=-1, keepdims=True)
            if with_sink:
                l = l + jnp.exp2(sk - m)
            o = _dot(e.astype(jnp.bfloat16), vw[:, ko:ko + bw]) * (1.0 / l)
            acc[qo] = jnp.where(hm, o, acc[qo])
            if with_lse:
                lacc[qo] = jnp.where(hm, (m + jnp.log2(l)) * LN2, lacc[qo])
        for qo in q_offs:
            o_ref[pl.ds(q0, bq), qo:qo + bw] = acc[qo].astype(o_ref.dtype)
            if with_lse:
                lse_ref[pl.ds(q0, bq), qo:qo + bw] = lacc[qo]
        return carry

    lax.fori_loop(0, ch // bq, body, 0)


def _band_call(q, k, v, *, heads, bw, bq, hw, seg_len, sink=None, with_lse, out_dtype, name):
    t, wq = q.shape
    wk = k.shape[1]
    ch = SEQ
    kern = functools.partial(_band_kernel, heads=heads, bw=bw, bq=bq, hw=hw, seg_len=seg_len,
                             with_sink=sink is not None, with_lse=with_lse)
    in_specs = [
        pl.BlockSpec((ch, wq), lambda i: (i, 0)),
        pl.BlockSpec((ch, wk), lambda i: (i, 0)),
        pl.BlockSpec((ch, wk), lambda i: (i, 0)),
    ]
    args = [q, k, v]
    if sink is not None:
        in_specs = [pl.BlockSpec(memory_space=pltpu.SMEM)] + in_specs
        args = [sink] + args
    out_shape = [jax.ShapeDtypeStruct((t, wq), out_dtype)]
    out_specs = [pl.BlockSpec((ch, wq), lambda i: (i, 0))]
    if with_lse:
        out_shape.append(jax.ShapeDtypeStruct((t, wq), jnp.float32))
        out_specs.append(pl.BlockSpec((ch, wq), lambda i: (i, 0)))
    return pl.pallas_call(
        kern, grid=(t // ch,), in_specs=in_specs, out_specs=out_specs, out_shape=out_shape,
        compiler_params=_params(1), name=name,
    )(*args)


B_HEAD_SPECS = tuple((0, h * B_HD, 0, h) for h in range(B_HEADS))
D_HEAD_SPECS = tuple(((h // 2) * LANES, (h % 2) * D_HD, (h // 4) * LANES, h) for h in range(D_QHEADS))


def _sigmoid(z):
    return 1.0 / (1.0 + jnp.exp(-z))


def _merge_kernel(x_ref, g1_ref, wg_ref, oa_ref, ob0, ob1, ob2, ls0, ls1, ls2, oc_ref, od_ref,
                  wa_ref, wb_ref, wc_ref, wd_ref, wo_ref, out_ref):
    x = x_ref[...]
    h = _rms(x, g1_ref[...]).astype(jnp.bfloat16)
    l0, l1, l2 = ls0[...], ls1[...], ls2[...]
    lm = jnp.maximum(jnp.maximum(l0, l1), l2)
    e0, e1, e2 = jnp.exp(l0 - lm), jnp.exp(l1 - lm), jnp.exp(l2 - lm)
    den = e0 + e1 + e2
    ob = ((e0 / den) * ob0[...] + (e1 / den) * ob1[...] + (e2 / den) * ob2[...]).astype(jnp.bfloat16)
    branches = ((oa_ref[...], wa_ref), (ob, wb_ref), (oc_ref[...], wc_ref), (od_ref[...], wd_ref))
    merged = None
    for i, (o, w_ref) in enumerate(branches):
        gate = _sigmoid(_dot(h, wg_ref[:, i * D_MODEL:(i + 1) * D_MODEL]))
        term = gate * _dot(o, w_ref[...])
        merged = term if merged is None else merged + term
    out_ref[...] = x + _dot(merged.astype(jnp.bfloat16), wo_ref[...])


def _merge_call(x, lw, oa, obs, lses, oc, od):
    t = x.shape[0]
    tm = TM_MERGE

    def tile(w):
        return pl.BlockSpec((tm, w), lambda i: (i, 0))

    in_specs = [tile(D_MODEL), _const_spec((1, D_MODEL)), _const_spec((D_MODEL, N_BRANCH * D_MODEL)),
                tile(512)] + [tile(256)] * 6 + [tile(512), tile(512),
                _const_spec((512, D_MODEL)), _const_spec((256, D_MODEL)), _const_spec((512, D_MODEL)),
                _const_spec((512, D_MODEL)), _const_spec((D_MODEL, D_MODEL))]
    return pl.pallas_call(
        _merge_kernel, grid=(t // tm,), in_specs=in_specs, out_specs=tile(D_MODEL),
        out_shape=jax.ShapeDtypeStruct((t, D_MODEL), jnp.float32),
        compiler_params=_params(1), name="merge",
    )(x, lw["g1"], lw["w_gate"], oa, *obs, *lses, oc, od,
      lw["w_br_a"], lw["w_br_b"], lw["w_br_c"], lw["w_br_d"], lw["w_o"])


def _ffn_kernel(x_ref, g2_ref, wg_ref, wu_ref, wd_ref, out_ref):
    x = x_ref[...]
    hf = _rms(x, g2_ref[...]).astype(jnp.bfloat16)
    acc = x
    for off, width in FFN_CHUNKS:
        a = _dot(hf, wg_ref[:, off:off + width])
        u = _dot(hf, wu_ref[:, off:off + width])
        act = (a * _sigmoid(a) * u).astype(jnp.bfloat16)
        acc = acc + _dot(act, wd_ref[off:off + width, :])
    out_ref[...] = acc


def _ffn_call(x, lw):
    t = x.shape[0]
    tm = TM_FFN
    tile = pl.BlockSpec((tm, D_MODEL), lambda i: (i, 0))
    return pl.pallas_call(
        _ffn_kernel, grid=(t // tm,),
        in_specs=[tile, _const_spec((1, D_MODEL)), _const_spec((D_MODEL, D_FF)),
                  _const_spec((D_MODEL, D_FF)), _const_spec((D_FF, D_MODEL))],
        out_specs=tile, out_shape=jax.ShapeDtypeStruct((t, D_MODEL), jnp.float32),
        compiler_params=_params(1), name="ffn",
    )(x, lw["g2"], lw["w_ffn_gate"], lw["w_ffn_up"], lw["w_ffn_down"])


def _tables():
    pos = jnp.arange(SEQ, dtype=jnp.float32)[:, None]
    lane = jnp.arange(LANES)
    inv64 = jnp.power(ROPE_THETA, -jnp.arange(32, dtype=jnp.float32) / 32)
    ang64 = pos * inv64[lane % 32][None, :]
    sign64 = jnp.where((lane % 64) < 32, -1.0, 1.0)[None, :]
    cos64 = jnp.cos(ang64)
    sin64 = jnp.sin(ang64) * sign64
    invc = jnp.power(ROPE_THETA, -jnp.arange(16, dtype=jnp.float32) / 16)
    angc = pos * invc[lane % 16][None, :]
    is_rope = ((lane >= C_NOPE) & (lane < C_NOPE + C_ROPE))[None, :]
    signc = jnp.where(lane < C_NOPE + C_ROPE // 2, -1.0, 1.0)[None, :]
    cosc = jnp.where(is_rope, jnp.cos(angc), 1.0)
    sinc = jnp.where(is_rope, jnp.sin(angc) * signc, 0.0)
    cs = jnp.stack([cos64, sin64, cosc, sinc]).astype(jnp.float32)
    idx = np.arange(MXU_N)
    bd = np.stack([(idx[:, None] // 64) == (idx[None, :] // 64),
                   (idx[:, None] // 128) == (idx[None, :] // 128)]).astype(np.float32)
    return {"cs": cs, "bd": jnp.asarray(bd, dtype=jnp.bfloat16)}


def _layer_weights(l, p):
    bf = jnp.bfloat16
    f32 = jnp.float32
    cols = jnp.split(p["w_in"][l], SPLIT_IDX, axis=-1)
    zeros = lambda n: jnp.zeros((D_MODEL, n), f32)
    dk, dv = cols[16], cols[17]
    w_in = jnp.concatenate(
        list(cols[0:12]) + [cols[12], cols[13], zeros(64), cols[14], zeros(32), cols[15],
                            dk[:, :64], dk[:, :64], dk[:, 64:], dk[:, 64:],
                            dv[:, :64], dv[:, :64], dv[:, 64:], dv[:, 64:]], axis=-1).astype(bf)
    ones = lambda n: jnp.ones((n,), f32)
    qs = A_HD ** -0.5 * LOG2E
    grow = jnp.concatenate(
        [jnp.tile(p["a_qnorm_g"][l], 8) * qs, jnp.tile(p["a_knorm_g"][l], 8), ones(512)]
        + sum([[jnp.tile(p["b_qnorm_g"][l, g], 4) * qs, jnp.tile(p["b_knorm_g"][l, g], 4), ones(256)]
               for g in range(3)], [])
        + [ones(512), jnp.tile(p["d_qnorm_g"][l], 8) * qs, jnp.tile(p["d_knorm_g"][l], 4), ones(256)]
    )[None, :].astype(f32)
    wuq = p["c_w_uq"][l].reshape(C_Q_RANK, C_HEADS, C_NOPE + C_ROPE)
    wuq = jnp.pad(wuq, ((0, 0), (0, 0), (0, 32))).reshape(C_Q_RANK, C_HEADS * LANES).astype(bf)
    wukv = p["c_w_ukv"][l].reshape(C_KV_RANK, C_HEADS, C_NOPE + C_VD)
    wuk = jnp.pad(wukv[:, :, :C_NOPE], ((0, 0), (0, 0), (0, 64))).reshape(C_KV_RANK, C_HEADS * LANES).astype(bf)
    wuv = wukv[:, :, C_NOPE:].reshape(C_KV_RANK, C_HEADS * C_VD).astype(bf)
    cscale = (C_NOPE + C_ROPE) ** -0.5 * LOG2E
    pad32 = lambda g: jnp.tile(jnp.pad(g, (0, 32)), C_HEADS)[None, :].astype(f32)
    return {
        "g1": p["norm1_g"][l][None, :], "w_in": w_in, "grow": grow,
        "qag": p["c_qa_norm_g"][l][None, :], "kvag": p["c_kva_norm_g"][l][None, :],
        "wuq": wuq, "wuk": wuk, "wuv": wuv,
        "gqc": pad32(p["c_qnorm_g"][l]) * cscale, "gkc": pad32(p["c_knorm_g"][l]),
        "a_lambda": p["a_lambda"][l], "subln": p["a_subln_g"][l][None, :],
        "d_sink": p["d_sink"][l],
        "w_gate": p["w_gate"][l].astype(bf),
        "w_br_a": p["w_br_a"][l].astype(bf), "w_br_b": p["w_br_b"][l].astype(bf),
        "w_br_c": p["w_br_c"][l].astype(bf), "w_br_d": p["w_br_d"][l].astype(bf),
        "w_o": p["w_o"][l].astype(bf), "g2": p["norm2_g"][l][None, :],
        "w_ffn_gate": p["w_ffn_gate"][l].astype(bf), "w_ffn_up": p["w_ffn_up"][l].astype(bf),
        "w_ffn_down": p["w_ffn_down"][l].astype(bf),
    }


def _to_residues(t, dil):
    n, w = t.shape
    return t.reshape(n // SEQ, SEQ // dil, dil, w).swapaxes(1, 2).reshape(n, w)


def _from_residues(t, dil):
    n, w = t.shape
    return t.reshape(n // SEQ, dil, SEQ // dil, w).swapaxes(1, 2).reshape(n, w)


def _layer(x, l, lw, tables):
    (qa, ka, va, qb0, kb0, vb0, qb1, kb1, vb1, qb2, kb2, vb2, qc, kc, vc, qd, kd, vd) = _proj_call(x, lw, tables)
    oa = _attn_a_call(qa, ka, va, lw["a_lambda"], lw["subln"], lambda_init(l))
    obs, lses = [], []
    for g, (qg, kg, vg) in enumerate(((qb0, kb0, vb0), (qb1, kb1, vb1), (qb2, kb2, vb2))):
        window, dil = B_PATTERNS[g]
        if dil > 1:
            qg, kg, vg = (_to_residues(a, dil) for a in (qg, kg, vg))
        o, lse = _band_call(qg, kg, vg, heads=B_HEAD_SPECS, bw=B_W, bq=128, hw=window // (2 * dil),
                            seg_len=SEQ // dil, with_lse=True, out_dtype=jnp.float32, name=f"band_b{g}")
        if dil > 1:
            o, lse = _from_residues(o, dil), _from_residues(lse, dil)
        obs.append(o)
        lses.append(lse)
    oc = _attn_c_call(qc, kc, vc)
    od = _band_call(qd, kd, vd, heads=D_HEAD_SPECS, bw=LANES, bq=256, hw=D_WIN, seg_len=SEQ,
                    sink=lw["d_sink"], with_lse=False, out_dtype=jnp.bfloat16, name="band_d")[0]
    x = _merge_call(x, lw, oa, obs, lses, oc, od)
    return _ffn_call(x, lw)


def kernel(x_prompt, x_sample, norm1_g, w_in, w_gate, a_qnorm_g, a_knorm_g, a_lambda, a_subln_g, b_qnorm_g, b_knorm_g, c_qa_norm_g, c_kva_norm_g, c_w_uq, c_w_ukv, c_qnorm_g, c_knorm_g, d_qnorm_g, d_knorm_g, d_sink, w_br_a, w_br_b, w_br_c, w_br_d, w_o, norm2_g, w_ffn_gate, w_ffn_up, w_ffn_down):
    p = dict(norm1_g=norm1_g, w_in=w_in, w_gate=w_gate, a_qnorm_g=a_qnorm_g, a_knorm_g=a_knorm_g,
             a_lambda=a_lambda, a_subln_g=a_subln_g, b_qnorm_g=b_qnorm_g, b_knorm_g=b_knorm_g,
             c_qa_norm_g=c_qa_norm_g, c_kva_norm_g=c_kva_norm_g, c_w_uq=c_w_uq, c_w_ukv=c_w_ukv,
             c_qnorm_g=c_qnorm_g, c_knorm_g=c_knorm_g, d_qnorm_g=d_qnorm_g, d_knorm_g=d_knorm_g,
             d_sink=d_sink, w_br_a=w_br_a, w_br_b=w_br_b, w_br_c=w_br_c, w_br_d=w_br_d, w_o=w_o,
             norm2_g=norm2_g, w_ffn_gate=w_ffn_gate, w_ffn_up=w_ffn_up, w_ffn_down=w_ffn_down)
    n_prompt = x_prompt.shape[0] * x_prompt.shape[1]
    x = jnp.concatenate([x_prompt.reshape(-1, D_MODEL), x_sample.reshape(-1, D_MODEL)], axis=0)
    tables = _tables()
    for l in range(DEPTH):
        x = _layer(x, l, _layer_weights(l, p), tables)
    return (x[:n_prompt].reshape(x_prompt.shape), x[n_prompt:].reshape(x_sample.shape))
```

```python
import functools
import math

import jax
import jax.numpy as jnp
import numpy as np
from jax import lax
from jax.experimental import pallas as pl
from jax.experimental.pallas import tpu as pltpu

D_MODEL = 1024
SEQ = 4096
DEPTH = 2
ROPE_THETA = 10000.0
EPS = 1e-6
NEG_INF = -1e30
N_BRANCH = 4
LOG2E = 1.4426950408889634
LN2 = 0.6931471805599453

A_HEADS = 4
A_HD = 64
B_PATTERNS = ((128, 1), (512, 4), (2048, 16))
B_HEADS = 4
B_HD = 64
B_W = B_HEADS * B_HD
C_HEADS = 8
C_Q_RANK = 256
C_KV_RANK = 128
C_NOPE = 64
C_ROPE = 32
C_VD = 64
D_QHEADS = 8
D_KVHEADS = 2
D_HD = 64
D_WIN = 128
D_FF = -(-8 * D_MODEL // (3 * 256)) * 256

IN_SIZES = (512, 512, 512) + (B_W,) * 9 + (C_Q_RANK, C_KV_RANK, C_ROPE, 512, 128, 128)
SPLIT_IDX = tuple(int(i) for i in np.cumsum(IN_SIZES)[:-1])

LANES = 128
MXU_N = 256

OFF_AQ, OFF_AK, OFF_AV = 0, 512, 1024
OFF_B = 1536
OFF_CQ = 3840
OFF_CKV = 4096
OFF_KR = 4224
OFF_DQ = 4352
OFF_DK = 4864
OFF_DV = 5120
N_PROJ = 5376

VMEM_LIMIT = 56 * 1024 * 1024

TM_PROJ = 256
TQ_A = 256
TQ_C = 256
HEADS_PER_STEP_A = 4
HEADS_PER_STEP_C = 8
TK_ATTN = 4096
TM_MERGE = 256
TM_FFN = 256
FFN_CHUNKS = ((0, 1024), (1024, 1024), (2048, 768))


def lambda_init(layer):
    return 0.8 - 0.6 * math.exp(-0.3 * layer)


def _const_spec(shape):
    nd = len(shape)
    return pl.BlockSpec(shape, lambda *_: (0,) * nd, pipeline_mode=pl.Buffered(1))


def _params(n_grid):
    return pltpu.CompilerParams(dimension_semantics=("arbitrary",) * n_grid,
                                vmem_limit_bytes=VMEM_LIMIT)


def _dot(a, b):
    return jnp.dot(a, b, preferred_element_type=jnp.float32)


def _dot_nt(a, b):
    return lax.dot_general(a, b, (((1,), (1,)), ((), ())), preferred_element_type=jnp.float32)


def _rms(x, g):
    ms = jnp.mean(x * x, axis=-1, keepdims=True)
    return x * lax.rsqrt(ms + EPS) * g


def _group_sumsq(p, bd):
    outs = []
    for c in range(p.shape[1] // MXU_N):
        pc = p[:, c * MXU_N:(c + 1) * MXU_N]
        p2 = pc * pc
        hi = p2.astype(jnp.bfloat16)
        lo = (p2 - hi.astype(jnp.float32)).astype(jnp.bfloat16)
        outs.append(_dot(hi, bd) + _dot(lo, bd))
    return outs[0] if len(outs) == 1 else jnp.concatenate(outs, axis=-1)


def _rope_chunks(y, cos, sin, first_half, shift):
    outs = []
    for c in range(y.shape[1] // LANES):
        yc = y[:, c * LANES:(c + 1) * LANES]
        sw = jnp.where(first_half, pltpu.roll(yc, LANES - shift, 1), pltpu.roll(yc, shift, 1))
        outs.append(yc * cos + sw * sin)
    return outs[0] if len(outs) == 1 else jnp.concatenate(outs, axis=-1)


def _proj_kernel(x_ref, g1_ref, w_ref, grow_ref, bd_ref, cs_ref, qag_ref, kvag_ref,
                 wuq_ref, wuk_ref, wuv_ref, gqc_ref, gkc_ref,
                 qa, ka, va, qb0, kb0, vb0, qb1, kb1, vb1, qb2, kb2, vb2,
                 qc, kc, vc, qd, kd, vd):
    tm = x_ref.shape[0]
    h = _rms(x_ref[...], g1_ref[...]).astype(jnp.bfloat16)
    bd64 = bd_ref[0]
    bd128 = bd_ref[1]
    cos64, sin64, cosc, sinc = cs_ref[0], cs_ref[1], cs_ref[2], cs_ref[3]
    lane = lax.broadcasted_iota(jnp.int32, (tm, LANES), 1)
    first64 = (lane & 63) < 32
    firstc = lane < (C_NOPE + C_ROPE // 2)

    def proj(off, width):
        return _dot(h, w_ref[:, off:off + width])

    def norm_rope64(off, width, out_ref):
        p = proj(off, width)
        ss = _group_sumsq(p, bd64)
        y = p * lax.rsqrt(ss * (1.0 / 64) + EPS) * grow_ref[:, off:off + width]
        out_ref[...] = _rope_chunks(y, cos64, sin64, first64, 32).astype(out_ref.dtype)

    def plain(off, width, out_ref):
        out_ref[...] = proj(off, width).astype(out_ref.dtype)

    norm_rope64(OFF_AQ, 512, qa)
    norm_rope64(OFF_AK, 512, ka)
    plain(OFF_AV, 512, va)
    for g, (qo, ko, vo) in enumerate(((qb0, kb0, vb0), (qb1, kb1, vb1), (qb2, kb2, vb2))):
        base = OFF_B + g * 768
        norm_rope64(base, 256, qo)
        norm_rope64(base + 256, 256, ko)
        plain(base + 512, 256, vo)
    norm_rope64(OFF_DQ, 512, qd)
    norm_rope64(OFF_DK, 256, kd)
    plain(OFF_DV, 256, vd)

    cqn = _rms(proj(OFF_CQ, C_Q_RANK), qag_ref[...]).astype(jnp.bfloat16)
    qfull = _dot(cqn, wuq_ref[...])
    qn = qfull * lax.rsqrt(_group_sumsq(qfull, bd128) * (1.0 / 96) + EPS) * gqc_ref[...]
    qc[...] = _rope_chunks(qn, cosc, sinc, firstc, C_ROPE // 2).astype(qc.dtype)
    ckvn = _rms(proj(OFF_CKV, C_KV_RANK), kvag_ref[...]).astype(jnp.bfloat16)
    kr = proj(OFF_KR, LANES)
    kfull = _dot(ckvn, wuk_ref[...]) + jnp.concatenate([kr] * C_HEADS, axis=-1)
    kn = kfull * lax.rsqrt(_group_sumsq(kfull, bd128) * (1.0 / 96) + EPS) * gkc_ref[...]
    kc[...] = _rope_chunks(kn, cosc, sinc, firstc, C_ROPE // 2).astype(kc.dtype)
    vc[...] = _dot(ckvn, wuv_ref[...]).astype(vc.dtype)


def _proj_call(x, lw, tables):
    t = x.shape[0]
    tm = TM_PROJ
    n_pos = SEQ // tm
    widths = (512, 512, 512) + (256,) * 9 + (1024, 1024, 512, 512, 256, 256)
    out_shape = [jax.ShapeDtypeStruct((t, w), jnp.bfloat16) for w in widths]
    out_specs = [pl.BlockSpec((tm, w), lambda i: (i, 0)) for w in widths]
    in_specs = [
        pl.BlockSpec((tm, D_MODEL), lambda i: (i, 0)),
        _const_spec((1, D_MODEL)),
        _const_spec((D_MODEL, N_PROJ)),
        _const_spec((1, N_PROJ)),
        _const_spec((2, MXU_N, MXU_N)),
        pl.BlockSpec((4, tm, LANES), lambda i: (0, i % n_pos, 0)),
        _const_spec((1, C_Q_RANK)),
        _const_spec((1, C_KV_RANK)),
        _const_spec((C_Q_RANK, 1024)),
        _const_spec((C_KV_RANK, 1024)),
        _const_spec((C_KV_RANK, 512)),
        _const_spec((1, 1024)),
        _const_spec((1, 1024)),
    ]
    return pl.pallas_call(
        _proj_kernel, grid=(t // tm,), in_specs=in_specs, out_specs=out_specs, out_shape=out_shape,
        compiler_params=_params(1), name="proj",
    )(x, lw["g1"], lw["w_in"], lw["grow"], tables["bd"], tables["cs"], lw["qag"], lw["kvag"],
      lw["wuq"], lw["wuk"], lw["wuv"], lw["gqc"], lw["gkc"])


def _lane_fold(x, op):
    acc = x[:, :LANES]
    for t in range(1, x.shape[1] // LANES):
        acc = op(acc, x[:, t * LANES:(t + 1) * LANES])
    return acc


def _online_step(state, s, v):
    r = jnp.max(_lane_fold(s, jnp.maximum), axis=-1, keepdims=True)
    if state is None:
        e = jnp.exp2(s - r)
        return r, _lane_fold(e, jnp.add), _dot(e.astype(jnp.bfloat16), v)
    m, l, acc = state
    m_new = jnp.maximum(m, r)
    alpha = jnp.exp2(m - m_new)
    e = jnp.exp2(s - m_new)
    return m_new, alpha * l + _lane_fold(e, jnp.add), alpha * acc + _dot(e.astype(jnp.bfloat16), v)


def _online_finish(state):
    _, l, acc = state
    return acc * (1.0 / jnp.sum(l, axis=-1, keepdims=True))


def _attn_a_kernel(lam_ref, q_ref, k_ref, v_ref, g_ref, o_ref, *, lam_init):
    lp = lam_ref[...]
    lam = (jnp.exp(jnp.sum(lp[0:1] * lp[1:2], axis=-1, keepdims=True))
           - jnp.exp(jnp.sum(lp[2:3] * lp[3:4], axis=-1, keepdims=True)) + lam_init)
    lane = lax.broadcasted_iota(jnp.int32, (q_ref.shape[0], LANES), 1)
    for h in range(q_ref.shape[1] // LANES):
        hs = slice(h * LANES, (h + 1) * LANES)
        q = q_ref[:, hs]
        k = k_ref[:, hs]
        es, inv = [], []
        for c in range(2):
            qc = jnp.where((lane >= c * A_HD) & (lane < (c + 1) * A_HD), q, jnp.zeros_like(q))
            s = _dot_nt(qc, k)
            m = jnp.max(s, axis=-1, keepdims=True)
            e = jnp.exp2(s - m)
            es.append(e)
            inv.append(1.0 / jnp.sum(e, axis=-1, keepdims=True))
        w = es[0] * inv[0] - es[1] * (lam * inv[1])
        o = _dot(w.astype(jnp.bfloat16), v_ref[:, hs])
        o = _rms(o, g_ref[...]) * (1.0 - lam_init)
        o_ref[:, hs] = o.astype(o_ref.dtype)


def _attn_a_call(q, k, v, a_lambda, subln_row, lam_init):
    t = q.shape[0]
    nseq = t // SEQ
    tq = TQ_A
    nq = SEQ // tq
    w = HEADS_PER_STEP_A * LANES
    return pl.pallas_call(
        functools.partial(_attn_a_kernel, lam_init=lam_init),
        grid=(nseq, A_HEADS // HEADS_PER_STEP_A, nq),
        in_specs=[
            _const_spec((4, A_HD)),
            pl.BlockSpec((tq, w), lambda s, h, i: (s * nq + i, h)),
            pl.BlockSpec((SEQ, w), lambda s, h, i: (s, h)),
            pl.BlockSpec((SEQ, w), lambda s, h, i: (s, h)),
            _const_spec((1, LANES)),
        ],
        out_specs=pl.BlockSpec((tq, w), lambda s, h, i: (s * nq + i, h)),
        out_shape=jax.ShapeDtypeStruct((t, A_HEADS * LANES), jnp.bfloat16),
        compiler_params=_params(3), name="attn_a",
    )(a_lambda, q, k, v, subln_row)


def _attn_c_kernel(q_ref, k_ref, v_ref, o_ref):
    lane = lax.broadcasted_iota(jnp.int32, (o_ref.shape[0], LANES), 1)
    n_heads = q_ref.shape[1] // LANES
    states = [None] * n_heads
    for c in range(SEQ // TK_ATTN):
        ks = slice(c * TK_ATTN, (c + 1) * TK_ATTN)
        for h in range(n_heads):
            hs = slice(h * LANES, (h + 1) * LANES)
            s = _dot_nt(q_ref[:, hs], k_ref[ks, hs])
            states[h] = _online_step(states[h], s, v_ref[ks, (h // 2) * LANES:(h // 2 + 1) * LANES])
    for j in range(n_heads // 2):
        o_ref[:, j * LANES:(j + 1) * LANES] = jnp.where(
            lane < C_VD, _online_finish(states[2 * j]), _online_finish(states[2 * j + 1])).astype(o_ref.dtype)


def _attn_c_call(q, k, v):
    t = q.shape[0]
    nseq = t // SEQ
    tq = TQ_C
    nq = SEQ // tq
    hps = HEADS_PER_STEP_C
    return pl.pallas_call(
        _attn_c_kernel,
        grid=(nseq, C_HEADS // hps, nq),
        in_specs=[
            pl.BlockSpec((tq, hps * LANES), lambda s, j, i: (s * nq + i, j)),
            pl.BlockSpec((SEQ, hps * LANES), lambda s, j, i: (s, j)),
            pl.BlockSpec((SEQ, hps * C_VD), lambda s, j, i: (s, j)),
        ],
        out_specs=pl.BlockSpec((tq, hps * C_VD), lambda s, j, i: (s * nq + i, j)),
        out_shape=jax.ShapeDtypeStruct((t, C_HEADS * C_VD), jnp.bfloat16),
        compiler_params=_params(3), name="attn_c",
    )(q, k, v)


def _band_kernel(*refs, heads, bw, bq, hw, seg_len, with_sink, with_lse):
    if with_sink:
        sink_ref, q_ref, k_ref, v_ref = refs[:4]
        outs = refs[4:]
    else:
        q_ref, k_ref, v_ref = refs[:3]
        outs = refs[3:]
    o_ref = outs[0]
    lse_ref = outs[1] if with_lse else None
    ch = q_ref.shape[0]
    win = bq + 2 * hw
    shift = int(math.log2(seg_len))
    lane = lax.broadcasted_iota(jnp.int32, (bq, bw), 1)
    row = lax.broadcasted_iota(jnp.int32, (bq, win), 0)
    col = lax.broadcasted_iota(jnp.int32, (bq, win), 1)
    q_offs = sorted({hd[0] for hd in heads})
    k_offs = sorted({hd[2] for hd in heads})

    def body(i, carry):
        q0 = pl.multiple_of(i * bq, bq)
        ws = pl.multiple_of(jnp.clip(q0 - hw, 0, ch - win), hw)
        qi = q0 + row
        kj = ws + col
        d = qi - kj
        valid = (d <= hw) & (d >= -hw) & ((qi >> shift) == (kj >> shift))
        bias = jnp.where(valid, 0.0, NEG_INF)
        qb = q_ref[pl.ds(q0, bq), :]
        kw = k_ref[pl.ds(ws, win), :]
        vw = v_ref[pl.ds(ws, win), :]
        acc = {qo: jnp.zeros((bq, bw), jnp.float32) for qo in q_offs}
        lacc = {qo: jnp.zeros((bq, bw), jnp.float32) for qo in q_offs}
        for ko in k_offs:
            group = [hd for hd in heads if hd[2] == ko]
            hms = [(lane >= mo) & (lane < mo + 64) for _, mo, _, _ in group]
            qs = jnp.concatenate(
                [jnp.where(hm, qb[:, qo:qo + bw], jnp.zeros((bq, bw), qb.dtype))
                 for hm, (qo, _, _, _) in zip(hms, group)], axis=0)
            s_all = _dot_nt(qs, kw[:, ko:ko + bw])
            es, ms, ls = [], [], []
            for n, (_, _, _, hid) in enumerate(group):
                s = s_all[n * bq:(n + 1) * bq] + bias
                m = jnp.max(s, axis=-1, keepdims=True)
                if with_sink:
                    sk = sink_ref[hid] * LOG2E
                    m = jnp.maximum(m, sk)
                e = jnp.exp2(s - m)
                l = jnp.sum(e, axis=-1, keepdims=True)
                if with_sink:
                    l = l + jnp.exp2(sk - m)
                es.append(e.astype(jnp.bfloat16))
                ms.append(m)
                ls.append(l)
            o_all = _dot(jnp.concatenate(es, axis=0), vw[:, ko:ko + bw])
            for n, (hm, (qo, _, _, _)) in enumerate(zip(hms, group)):
                o = o_all[n * bq:(n + 1) * bq] * (1.0 / ls[n])
                acc[qo] = jnp.where(hm, o, acc[qo])
                if with_lse:
                    lacc[qo] = jnp.where(hm, (ms[n] + jnp.log2(ls[n])) * LN2, lacc[qo])
        for qo in q_offs:
            o_ref[pl.ds(q0, bq), qo:qo + bw] = acc[qo].astype(o_ref.dtype)
            if with_lse:
                lse_ref[pl.ds(q0, bq), qo:qo + bw] = lacc[qo]
        return carry

    lax.fori_loop(0, ch // bq, body, 0, unroll=2)


def _band_call(q, k, v, *, heads, bw, bq, hw, seg_len, sink=None, with_lse, out_dtype, name):
    t, wq = q.shape
    wk = k.shape[1]
    ch = SEQ
    kern = functools.partial(_band_kernel, heads=heads, bw=bw, bq=bq, hw=hw, seg_len=seg_len,
                             with_sink=sink is not None, with_lse=with_lse)
    in_specs = [
        pl.BlockSpec((ch, wq), lambda i: (i, 0)),
        pl.BlockSpec((ch, wk), lambda i: (i, 0)),
        pl.BlockSpec((ch, wk), lambda i: (i, 0)),
    ]
    args = [q, k, v]
    if sink is not None:
        in_specs = [pl.BlockSpec(memory_space=pltpu.SMEM)] + in_specs
        args = [sink] + args
    out_shape = [jax.ShapeDtypeStruct((t, wq), out_dtype)]
    out_specs = [pl.BlockSpec((ch, wq), lambda i: (i, 0))]
    if with_lse:
        out_shape.append(jax.ShapeDtypeStruct((t, wq), jnp.float32))
        out_specs.append(pl.BlockSpec((ch, wq), lambda i: (i, 0)))
    return pl.pallas_call(
        kern, grid=(t // ch,), in_specs=in_specs, out_specs=out_specs, out_shape=out_shape,
        compiler_params=_params(1), name=name,
    )(*args)


B_HEAD_SPECS = tuple((0, h * B_HD, 0, h) for h in range(B_HEADS))
D_HEAD_SPECS = tuple(((h // 2) * LANES, (h % 2) * D_HD, (h // 4) * LANES, h) for h in range(D_QHEADS))


def _sigmoid(z):
    return 1.0 / (1.0 + jnp.exp(-z))


def _merge_kernel(x_ref, g1_ref, wg_ref, oa_ref, ob0, ob1, ob2, ls0, ls1, ls2, oc_ref, od_ref,
                  wa_ref, wb_ref, wc_ref, wd_ref, wo_ref, out_ref):
    x = x_ref[...]
    h = _rms(x, g1_ref[...]).astype(jnp.bfloat16)
    l0, l1, l2 = ls0[...], ls1[...], ls2[...]
    lm = jnp.maximum(jnp.maximum(l0, l1), l2)
    e0, e1, e2 = jnp.exp(l0 - lm), jnp.exp(l1 - lm), jnp.exp(l2 - lm)
    den = e0 + e1 + e2
    ob = ((e0 / den) * ob0[...] + (e1 / den) * ob1[...] + (e2 / den) * ob2[...]).astype(jnp.bfloat16)
    branches = ((oa_ref[...], wa_ref), (ob, wb_ref), (oc_ref[...], wc_ref), (od_ref[...], wd_ref))
    merged = None
    for i, (o, w_ref) in enumerate(branches):
        gate = _sigmoid(_dot(h, wg_ref[:, i * D_MODEL:(i + 1) * D_MODEL]))
        term = gate * _dot(o, w_ref[...])
        merged = term if merged is None else merged + term
    out_ref[...] = x + _dot(merged.astype(jnp.bfloat16), wo_ref[...])


def _merge_call(x, lw, oa, obs, lses, oc, od):
    t = x.shape[0]
    tm = TM_MERGE

    def tile(w):
        return pl.BlockSpec((tm, w), lambda i: (i, 0))

    in_specs = [tile(D_MODEL), _const_spec((1, D_MODEL)), _const_spec((D_MODEL, N_BRANCH * D_MODEL)),
                tile(512)] + [tile(256)] * 6 + [tile(512), tile(512),
                _const_spec((512, D_MODEL)), _const_spec((256, D_MODEL)), _const_spec((512, D_MODEL)),
                _const_spec((512, D_MODEL)), _const_spec((D_MODEL, D_MODEL))]
    return pl.pallas_call(
        _merge_kernel, grid=(t // tm,), in_specs=in_specs, out_specs=tile(D_MODEL),
        out_shape=jax.ShapeDtypeStruct((t, D_MODEL), jnp.float32),
        compiler_params=_params(1), name="merge",
    )(x, lw["g1"], lw["w_gate"], oa, *obs, *lses, oc, od,
      lw["w_br_a"], lw["w_br_b"], lw["w_br_c"], lw["w_br_d"], lw["w_o"])


def _ffn_kernel(x_ref, g2_ref, wg_ref, wu_ref, wd_ref, out_ref):
    x = x_ref[...]
    hf = _rms(x, g2_ref[...]).astype(jnp.bfloat16)
    acc = x
    for off, width in FFN_CHUNKS:
        a = _dot(hf, wg_ref[:, off:off + width])
        u = _dot(hf, wu_ref[:, off:off + width])
        act = (a * _sigmoid(a) * u).astype(jnp.bfloat16)
        acc = acc + _dot(act, wd_ref[off:off + width, :])
    out_ref[...] = acc


def _ffn_call(x, lw):
    t = x.shape[0]
    tm = TM_FFN
    tile = pl.BlockSpec((tm, D_MODEL), lambda i: (i, 0))
    return pl.pallas_call(
        _ffn_kernel, grid=(t // tm,),
        in_specs=[tile, _const_spec((1, D_MODEL)), _const_spec((D_MODEL, D_FF)),
                  _const_spec((D_MODEL, D_FF)), _const_spec((D_FF, D_MODEL))],
        out_specs=tile, out_shape=jax.ShapeDtypeStruct((t, D_MODEL), jnp.float32),
        compiler_params=_params(1), name="ffn",
    )(x, lw["g2"], lw["w_ffn_gate"], lw["w_ffn_up"], lw["w_ffn_down"])


def _tables():
    pos = jnp.arange(SEQ, dtype=jnp.float32)[:, None]
    lane = jnp.arange(LANES)
    inv64 = jnp.power(ROPE_THETA, -jnp.arange(32, dtype=jnp.float32) / 32)
    ang64 = pos * inv64[lane % 32][None, :]
    sign64 = jnp.where((lane % 64) < 32, -1.0, 1.0)[None, :]
    cos64 = jnp.cos(ang64)
    sin64 = jnp.sin(ang64) * sign64
    invc = jnp.power(ROPE_THETA, -jnp.arange(16, dtype=jnp.float32) / 16)
    angc = pos * invc[lane % 16][None, :]
    is_rope = ((lane >= C_NOPE) & (lane < C_NOPE + C_ROPE))[None, :]
    signc = jnp.where(lane < C_NOPE + C_ROPE // 2, -1.0, 1.0)[None, :]
    cosc = jnp.where(is_rope, jnp.cos(angc), 1.0)
    sinc = jnp.where(is_rope, jnp.sin(angc) * signc, 0.0)
    cs = jnp.stack([cos64, sin64, cosc, sinc]).astype(jnp.float32)
    idx = np.arange(MXU_N)
    bd = np.stack([(idx[:, None] // 64) == (idx[None, :] // 64),
                   (idx[:, None] // 128) == (idx[None, :] // 128)]).astype(np.float32)
    return {"cs": cs, "bd": jnp.asarray(bd, dtype=jnp.bfloat16)}


def _layer_weights(l, p):
    bf = jnp.bfloat16
    f32 = jnp.float32
    cols = jnp.split(p["w_in"][l], SPLIT_IDX, axis=-1)
    zeros = lambda n: jnp.zeros((D_MODEL, n), f32)
    dk, dv = cols[16], cols[17]
    w_in = jnp.concatenate(
        list(cols[0:12]) + [cols[12], cols[13], zeros(64), cols[14], zeros(32), cols[15],
                            dk[:, :64], dk[:, :64], dk[:, 64:], dk[:, 64:],
                            dv[:, :64], dv[:, :64], dv[:, 64:], dv[:, 64:]], axis=-1).astype(bf)
    ones = lambda n: jnp.ones((n,), f32)
    qs = A_HD ** -0.5 * LOG2E
    grow = jnp.concatenate(
        [jnp.tile(p["a_qnorm_g"][l], 8) * qs, jnp.tile(p["a_knorm_g"][l], 8), ones(512)]
        + sum([[jnp.tile(p["b_qnorm_g"][l, g], 4) * qs, jnp.tile(p["b_knorm_g"][l, g], 4), ones(256)]
               for g in range(3)], [])
        + [ones(512), jnp.tile(p["d_qnorm_g"][l], 8) * qs, jnp.tile(p["d_knorm_g"][l], 4), ones(256)]
    )[None, :].astype(f32)
    wuq = p["c_w_uq"][l].reshape(C_Q_RANK, C_HEADS, C_NOPE + C_ROPE)
    wuq = jnp.pad(wuq, ((0, 0), (0, 0), (0, 32))).reshape(C_Q_RANK, C_HEADS * LANES).astype(bf)
    wukv = p["c_w_ukv"][l].reshape(C_KV_RANK, C_HEADS, C_NOPE + C_VD)
    wuk = jnp.pad(wukv[:, :, :C_NOPE], ((0, 0), (0, 0), (0, 64))).reshape(C_KV_RANK, C_HEADS * LANES).astype(bf)
    wuv = wukv[:, :, C_NOPE:].reshape(C_KV_RANK, C_HEADS * C_VD).astype(bf)
    cscale = (C_NOPE + C_ROPE) ** -0.5 * LOG2E
    pad32 = lambda g: jnp.tile(jnp.pad(g, (0, 32)), C_HEADS)[None, :].astype(f32)
    return {
        "g1": p["norm1_g"][l][None, :], "w_in": w_in, "grow": grow,
        "qag": p["c_qa_norm_g"][l][None, :], "kvag": p["c_kva_norm_g"][l][None, :],
        "wuq": wuq, "wuk": wuk, "wuv": wuv,
        "gqc": pad32(p["c_qnorm_g"][l]) * cscale, "gkc": pad32(p["c_knorm_g"][l]),
        "a_lambda": p["a_lambda"][l], "subln": p["a_subln_g"][l][None, :],
        "d_sink": p["d_sink"][l],
        "w_gate": p["w_gate"][l].astype(bf),
        "w_br_a": p["w_br_a"][l].astype(bf), "w_br_b": p["w_br_b"][l].astype(bf),
        "w_br_c": p["w_br_c"][l].astype(bf), "w_br_d": p["w_br_d"][l].astype(bf),
        "w_o": p["w_o"][l].astype(bf), "g2": p["norm2_g"][l][None, :],
        "w_ffn_gate": p["w_ffn_gate"][l].astype(bf), "w_ffn_up": p["w_ffn_up"][l].astype(bf),
        "w_ffn_down": p["w_ffn_down"][l].astype(bf),
    }


def _to_residues(t, dil):
    n, w = t.shape
    return t.reshape(n // SEQ, SEQ // dil, dil, w).swapaxes(1, 2).reshape(n, w)


def _from_residues(t, dil):
    n, w = t.shape
    return t.reshape(n // SEQ, dil, SEQ // dil, w).swapaxes(1, 2).reshape(n, w)


def _layer(x, l, lw, tables):
    (qa, ka, va, qb0, kb0, vb0, qb1, kb1, vb1, qb2, kb2, vb2, qc, kc, vc, qd, kd, vd) = _proj_call(x, lw, tables)
    oa = _attn_a_call(qa, ka, va, lw["a_lambda"], lw["subln"], lambda_init(l))
    obs, lses = [], []
    for g, (qg, kg, vg) in enumerate(((qb0, kb0, vb0), (qb1, kb1, vb1), (qb2, kb2, vb2))):
        window, dil = B_PATTERNS[g]
        if dil > 1:
            qg, kg, vg = (_to_residues(a, dil) for a in (qg, kg, vg))
        o, lse = _band_call(qg, kg, vg, heads=B_HEAD_SPECS, bw=B_W, bq=128, hw=window // (2 * dil),
                            seg_len=SEQ // dil, with_lse=True, out_dtype=jnp.float32, name=f"band_b{g}")
        if dil > 1:
            o, lse = _from_residues(o, dil), _from_residues(lse, dil)
        obs.append(o)
        lses.append(lse)
    oc = _attn_c_call(qc, kc, vc)
    od = _band_call(qd, kd, vd, heads=D_HEAD_SPECS, bw=LANES, bq=128, hw=D_WIN, seg_len=SEQ,
                    sink=lw["d_sink"], with_lse=False, out_dtype=jnp.bfloat16, name="band_d")[0]
    x = _merge_call(x, lw, oa, obs, lses, oc, od)
    return _ffn_call(x, lw)


def kernel(x_prompt, x_sample, norm1_g, w_in, w_gate, a_qnorm_g, a_knorm_g, a_lambda, a_subln_g, b_qnorm_g, b_knorm_g, c_qa_norm_g, c_kva_norm_g, c_w_uq, c_w_ukv, c_qnorm_g, c_knorm_g, d_qnorm_g, d_knorm_g, d_sink, w_br_a, w_br_b, w_br_c, w_br_d, w_o, norm2_g, w_ffn_gate, w_ffn_up, w_ffn_down):
    p = dict(norm1_g=norm1_g, w_in=w_in, w_gate=w_gate, a_qnorm_g=a_qnorm_g, a_knorm_g=a_knorm_g,
             a_lambda=a_lambda, a_subln_g=a_subln_g, b_qnorm_g=b_qnorm_g, b_knorm_g=b_knorm_g,
             c_qa_norm_g=c_qa_norm_g, c_kva_norm_g=c_kva_norm_g, c_w_uq=c_w_uq, c_w_ukv=c_w_ukv,
             c_qnorm_g=c_qnorm_g, c_knorm_g=c_knorm_g, d_qnorm_g=d_qnorm_g, d_knorm_g=d_knorm_g,
             d_sink=d_sink, w_br_a=w_br_a, w_br_b=w_br_b, w_br_c=w_br_c, w_br_d=w_br_d, w_o=w_o,
             norm2_g=norm2_g, w_ffn_gate=w_ffn_gate, w_ffn_up=w_ffn_up, w_ffn_down=w_ffn_down)
    n_prompt = x_prompt.shape[0] * x_prompt.shape[1]
    x = jnp.concatenate([x_prompt.reshape(-1, D_MODEL), x_sample.reshape(-1, D_MODEL)], axis=0)
    tables = _tables()
    for l in range(DEPTH):
        x = _layer(x, l, _layer_weights(l, p), tables)
    return (x[:n_prompt].reshape(x_prompt.shape), x[n_prompt:].reshape(x_sample.shape))
```

```python
import functools
import math

import jax
import jax.numpy as jnp
import numpy as np
from jax import lax
from jax.experimental import pallas as pl
from jax.experimental.pallas import tpu as pltpu

D_MODEL = 1024
SEQ = 4096
DEPTH = 2
ROPE_THETA = 10000.0
EPS = 1e-6
NEG_INF = -1e30
N_BRANCH = 4
LOG2E = 1.4426950408889634
LN2 = 0.6931471805599453

A_HEADS = 4
A_HD = 64
B_PATTERNS = ((128, 1), (512, 4), (2048, 16))
B_HEADS = 4
B_HD = 64
B_W = B_HEADS * B_HD
C_HEADS = 8
C_Q_RANK = 256
C_KV_RANK = 128
C_NOPE = 64
C_ROPE = 32
C_VD = 64
D_QHEADS = 8
D_KVHEADS = 2
D_HD = 64
D_WIN = 128
D_FF = -(-8 * D_MODEL // (3 * 256)) * 256

IN_SIZES = (512, 512, 512) + (B_W,) * 9 + (C_Q_RANK, C_KV_RANK, C_ROPE, 512, 128, 128)
SPLIT_IDX = tuple(int(i) for i in np.cumsum(IN_SIZES)[:-1])

LANES = 128
MXU_N = 256

OFF_AQ, OFF_AK, OFF_AV = 0, 512, 1024
OFF_B = 1536
OFF_CQ = 3840
OFF_CKV = 4096
OFF_KR = 4224
OFF_DQ = 4352
OFF_DK = 4864
OFF_DV = 5120
N_PROJ = 5376

VMEM_LIMIT = 56 * 1024 * 1024

TM_PROJ = 512
TQ_A = 256
TQ_C = 256
HEADS_PER_STEP_A = 4
HEADS_PER_STEP_C = 8
TK_ATTN = 4096
TM_MERGE = 256
TM_FFN = 256
FFN_CHUNKS = ((0, 1024), (1024, 1024), (2048, 768))


def lambda_init(layer):
    return 0.8 - 0.6 * math.exp(-0.3 * layer)


def _const_spec(shape):
    nd = len(shape)
    return pl.BlockSpec(shape, lambda *_: (0,) * nd, pipeline_mode=pl.Buffered(1))


def _params(n_grid):
    return pltpu.CompilerParams(dimension_semantics=("arbitrary",) * n_grid,
                                vmem_limit_bytes=VMEM_LIMIT)


def _dot(a, b):
    return jnp.dot(a, b, preferred_element_type=jnp.float32)


def _dot_nt(a, b):
    return lax.dot_general(a, b, (((1,), (1,)), ((), ())), preferred_element_type=jnp.float32)


def _rms(x, g):
    ms = jnp.mean(x * x, axis=-1, keepdims=True)
    return x * lax.rsqrt(ms + EPS) * g


def _group_sumsq(p, bd):
    outs = []
    for c in range(p.shape[1] // MXU_N):
        pc = p[:, c * MXU_N:(c + 1) * MXU_N]
        outs.append(_dot((pc * pc).astype(jnp.bfloat16), bd))
    return outs[0] if len(outs) == 1 else jnp.concatenate(outs, axis=-1)


def _rope_chunks(y, cos, sin, first_half, shift):
    outs = []
    for c in range(y.shape[1] // LANES):
        yc = y[:, c * LANES:(c + 1) * LANES]
        sw = jnp.where(first_half, pltpu.roll(yc, LANES - shift, 1), pltpu.roll(yc, shift, 1))
        outs.append(yc * cos + sw * sin)
    return outs[0] if len(outs) == 1 else jnp.concatenate(outs, axis=-1)


def _proj_kernel(x_ref, g1_ref, w_ref, grow_ref, bd_ref, cs_ref, qag_ref, kvag_ref,
                 wuq_ref, wuk_ref, wuv_ref, gqc_ref, gkc_ref,
                 qa, ka, va, qb0, kb0, vb0, qb1, kb1, vb1, qb2, kb2, vb2,
                 qc, kc, vc, qd, kd, vd, *res_scr):
    tm = x_ref.shape[0]
    res_scr = list(res_scr)
    h = _rms(x_ref[...], g1_ref[...]).astype(jnp.bfloat16)
    bd64 = bd_ref[0]
    bd128 = bd_ref[1]
    cos64, sin64, cosc, sinc = cs_ref[0], cs_ref[1], cs_ref[2], cs_ref[3]
    lane = lax.broadcasted_iota(jnp.int32, (tm, LANES), 1)
    first64 = (lane & 63) < 32
    firstc = lane < (C_NOPE + C_ROPE // 2)

    def proj(off, width):
        return _dot(h, w_ref[:, off:off + width])

    def store(out_ref, val):
        if len(out_ref.shape) == 2:
            out_ref[...] = val.astype(out_ref.dtype)
            return
        dil, rows = out_ref.shape[1], out_ref.shape[2]
        scr = res_scr.pop()
        n_c = scr.shape[0]
        for c in range(n_c):
            scr[c] = val[:, c * LANES:(c + 1) * LANES]
        for r in range(dil):
            out_ref[0, r] = jnp.concatenate(
                [scr[c, pl.ds(r, rows, stride=dil), :] for c in range(n_c)], axis=-1).astype(out_ref.dtype)

    def norm_rope64(off, width, out_ref):
        p = proj(off, width)
        ss = _group_sumsq(p, bd64)
        y = p * lax.rsqrt(ss * (1.0 / 64) + EPS) * grow_ref[:, off:off + width]
        store(out_ref, _rope_chunks(y, cos64, sin64, first64, 32))

    def plain(off, width, out_ref):
        store(out_ref, proj(off, width))

    norm_rope64(OFF_AQ, 512, qa)
    norm_rope64(OFF_AK, 512, ka)
    plain(OFF_AV, 512, va)
    for g, (qo, ko, vo) in enumerate(((qb0, kb0, vb0), (qb1, kb1, vb1), (qb2, kb2, vb2))):
        base = OFF_B + g * 768
        norm_rope64(base, 256, qo)
        norm_rope64(base + 256, 256, ko)
        plain(base + 512, 256, vo)
    norm_rope64(OFF_DQ, 512, qd)
    norm_rope64(OFF_DK, 256, kd)
    plain(OFF_DV, 256, vd)

    cqn = _rms(proj(OFF_CQ, C_Q_RANK), qag_ref[...]).astype(jnp.bfloat16)
    qfull = _dot(cqn, wuq_ref[...])
    qn = qfull * lax.rsqrt(_group_sumsq(qfull, bd128) * (1.0 / 96) + EPS) * gqc_ref[...]
    qc[...] = _rope_chunks(qn, cosc, sinc, firstc, C_ROPE // 2).astype(qc.dtype)
    ckvn = _rms(proj(OFF_CKV, C_KV_RANK), kvag_ref[...]).astype(jnp.bfloat16)
    kr = proj(OFF_KR, LANES)
    kfull = _dot(ckvn, wuk_ref[...]) + jnp.concatenate([kr] * C_HEADS, axis=-1)
    kn = kfull * lax.rsqrt(_group_sumsq(kfull, bd128) * (1.0 / 96) + EPS) * gkc_ref[...]
    kc[...] = _rope_chunks(kn, cosc, sinc, firstc, C_ROPE // 2).astype(kc.dtype)
    vc[...] = _dot(ckvn, wuv_ref[...]).astype(vc.dtype)


def _proj_call(x, lw, tables):
    t = x.shape[0]
    tm = TM_PROJ
    n_pos = SEQ // tm
    widths = (512, 512, 512) + (256,) * 9 + (1024, 1024, 512, 512, 256, 256)
    out_shape = [jax.ShapeDtypeStruct((t, w), jnp.bfloat16) for w in widths]
    out_specs = [pl.BlockSpec((tm, w), lambda i: (i, 0)) for w in widths]
    for g, (_, dil) in enumerate(B_PATTERNS):
        if dil > 1:
            for j in range(3 + 3 * g, 6 + 3 * g):
                out_shape[j] = jax.ShapeDtypeStruct((t // SEQ, dil, SEQ // dil, B_W), jnp.bfloat16)
                out_specs[j] = pl.BlockSpec((1, dil, tm // dil, B_W), lambda i: (i // n_pos, 0, i % n_pos, 0))
    n_res = 3 * sum(1 for _, dil in B_PATTERNS if dil > 1)
    in_specs = [
        pl.BlockSpec((tm, D_MODEL), lambda i: (i, 0)),
        _const_spec((1, D_MODEL)),
        _const_spec((D_MODEL, N_PROJ)),
        _const_spec((1, N_PROJ)),
        _const_spec((2, MXU_N, MXU_N)),
        pl.BlockSpec((4, tm, LANES), lambda i: (0, i % n_pos, 0)),
        _const_spec((1, C_Q_RANK)),
        _const_spec((1, C_KV_RANK)),
        _const_spec((C_Q_RANK, 1024)),
        _const_spec((C_KV_RANK, 1024)),
        _const_spec((C_KV_RANK, 512)),
        _const_spec((1, 1024)),
        _const_spec((1, 1024)),
    ]
    return pl.pallas_call(
        _proj_kernel, grid=(t // tm,), in_specs=in_specs, out_specs=out_specs, out_shape=out_shape,
        scratch_shapes=[pltpu.VMEM((B_W // LANES, tm, LANES), jnp.float32)] * n_res,
        compiler_params=_params(1), name="proj",
    )(x, lw["g1"], lw["w_in"], lw["grow"], tables["bd"], tables["cs"], lw["qag"], lw["kvag"],
      lw["wuq"], lw["wuk"], lw["wuv"], lw["gqc"], lw["gkc"])


def _lane_fold(x, op):
    acc = x[:, :LANES]
    for t in range(1, x.shape[1] // LANES):
        acc = op(acc, x[:, t * LANES:(t + 1) * LANES])
    return acc


def _online_step(state, s, v):
    r = jnp.max(_lane_fold(s, jnp.maximum), axis=-1, keepdims=True)
    if state is None:
        e = jnp.exp2(s - r)
        return r, _lane_fold(e, jnp.add), _dot(e.astype(jnp.bfloat16), v)
    m, l, acc = state
    m_new = jnp.maximum(m, r)
    alpha = jnp.exp2(m - m_new)
    e = jnp.exp2(s - m_new)
    return m_new, alpha * l + _lane_fold(e, jnp.add), alpha * acc + _dot(e.astype(jnp.bfloat16), v)


def _online_finish(state):
    _, l, acc = state
    return acc * (1.0 / jnp.sum(l, axis=-1, keepdims=True))


def _attn_a_kernel(lam_ref, q_ref, k_ref, v_ref, g_ref, o_ref, *, lam_init):
    lp = lam_ref[...]
    lam = (jnp.exp(jnp.sum(lp[0:1] * lp[1:2], axis=-1, keepdims=True))
           - jnp.exp(jnp.sum(lp[2:3] * lp[3:4], axis=-1, keepdims=True)) + lam_init)
    lane = lax.broadcasted_iota(jnp.int32, (q_ref.shape[0], LANES), 1)
    for h in range(q_ref.shape[1] // LANES):
        hs = slice(h * LANES, (h + 1) * LANES)
        q = q_ref[:, hs]
        k = k_ref[:, hs]
        es, inv = [], []
        for c in range(2):
            qc = jnp.where((lane >= c * A_HD) & (lane < (c + 1) * A_HD), q, jnp.zeros_like(q))
            s = _dot_nt(qc, k)
            m = jnp.max(s, axis=-1, keepdims=True)
            e = jnp.exp2(s - m)
            es.append(e)
            inv.append(1.0 / jnp.sum(e, axis=-1, keepdims=True))
        w = es[0] * inv[0] - es[1] * (lam * inv[1])
        o = _dot(w.astype(jnp.bfloat16), v_ref[:, hs])
        o = _rms(o, g_ref[...]) * (1.0 - lam_init)
        o_ref[:, hs] = o.astype(o_ref.dtype)


def _attn_a_call(q, k, v, a_lambda, subln_row, lam_init):
    t = q.shape[0]
    nseq = t // SEQ
    tq = TQ_A
    nq = SEQ // tq
    w = HEADS_PER_STEP_A * LANES
    return pl.pallas_call(
        functools.partial(_attn_a_kernel, lam_init=lam_init),
        grid=(nseq, A_HEADS // HEADS_PER_STEP_A, nq),
        in_specs=[
            _const_spec((4, A_HD)),
            pl.BlockSpec((tq, w), lambda s, h, i: (s * nq + i, h)),
            pl.BlockSpec((SEQ, w), lambda s, h, i: (s, h)),
            pl.BlockSpec((SEQ, w), lambda s, h, i: (s, h)),
            _const_spec((1, LANES)),
        ],
        out_specs=pl.BlockSpec((tq, w), lambda s, h, i: (s * nq + i, h)),
        out_shape=jax.ShapeDtypeStruct((t, A_HEADS * LANES), jnp.bfloat16),
        compiler_params=_params(3), name="attn_a",
    )(a_lambda, q, k, v, subln_row)


def _attn_c_kernel(q_ref, k_ref, v_ref, o_ref):
    lane = lax.broadcasted_iota(jnp.int32, (o_ref.shape[0], LANES), 1)
    n_heads = q_ref.shape[1] // LANES
    states = [None] * n_heads
    for c in range(SEQ // TK_ATTN):
        ks = slice(c * TK_ATTN, (c + 1) * TK_ATTN)
        for h in range(n_heads):
            hs = slice(h * LANES, (h + 1) * LANES)
            s = _dot_nt(q_ref[:, hs], k_ref[ks, hs])
            states[h] = _online_step(states[h], s, v_ref[ks, (h // 2) * LANES:(h // 2 + 1) * LANES])
    for j in range(n_heads // 2):
        o_ref[:, j * LANES:(j + 1) * LANES] = jnp.where(
            lane < C_VD, _online_finish(states[2 * j]), _online_finish(states[2 * j + 1])).astype(o_ref.dtype)


def _attn_c_call(q, k, v):
    t = q.shape[0]
    nseq = t // SEQ
    tq = TQ_C
    nq = SEQ // tq
    hps = HEADS_PER_STEP_C
    return pl.pallas_call(
        _attn_c_kernel,
        grid=(nseq, C_HEADS // hps, nq),
        in_specs=[
            pl.BlockSpec((tq, hps * LANES), lambda s, j, i: (s * nq + i, j)),
            pl.BlockSpec((SEQ, hps * LANES), lambda s, j, i: (s, j)),
            pl.BlockSpec((SEQ, hps * C_VD), lambda s, j, i: (s, j)),
        ],
        out_specs=pl.BlockSpec((tq, hps * C_VD), lambda s, j, i: (s * nq + i, j)),
        out_shape=jax.ShapeDtypeStruct((t, C_HEADS * C_VD), jnp.bfloat16),
        compiler_params=_params(3), name="attn_c",
    )(q, k, v)


def _band_kernel(*refs, heads, bw, bq, hw, seg_len, with_sink, with_lse):
    if with_sink:
        sink_ref, q_ref, k_ref, v_ref = refs[:4]
        outs = refs[4:]
    else:
        q_ref, k_ref, v_ref = refs[:3]
        outs = refs[3:]
    o_ref = outs[0]
    lse_ref = outs[1] if with_lse else None
    ch = q_ref.shape[0]
    win = bq + 2 * hw
    shift = int(math.log2(seg_len))
    lane = lax.broadcasted_iota(jnp.int32, (bq, bw), 1)
    row = lax.broadcasted_iota(jnp.int32, (bq, win), 0)
    col = lax.broadcasted_iota(jnp.int32, (bq, win), 1)
    q_offs = sorted({hd[0] for hd in heads})
    k_offs = sorted({hd[2] for hd in heads})

    def body(i, carry):
        q0 = pl.multiple_of(i * bq, bq)
        ws = pl.multiple_of(jnp.clip(q0 - hw, 0, ch - win), hw)
        qi = q0 + row
        kj = ws + col
        d = qi - kj
        valid = (d <= hw) & (d >= -hw) & ((qi >> shift) == (kj >> shift))
        bias = jnp.where(valid, 0.0, NEG_INF)
        qb = q_ref[pl.ds(q0, bq), :]
        kw = k_ref[pl.ds(ws, win), :]
        vw = v_ref[pl.ds(ws, win), :]
        acc = {qo: jnp.zeros((bq, bw), jnp.float32) for qo in q_offs}
        lacc = {qo: jnp.zeros((bq, bw), jnp.float32) for qo in q_offs}
        for ko in k_offs:
            group = [hd for hd in heads if hd[2] == ko]
            hms = [(lane >= mo) & (lane < mo + 64) for _, mo, _, _ in group]
            qs = jnp.concatenate(
                [jnp.where(hm, qb[:, qo:qo + bw], jnp.zeros((bq, bw), qb.dtype))
                 for hm, (qo, _, _, _) in zip(hms, group)], axis=0)
            s_all = _dot_nt(qs, kw[:, ko:ko + bw])
            es, ms, ls = [], [], []
            for n, (_, _, _, hid) in enumerate(group):
                s = s_all[n * bq:(n + 1) * bq] + bias
                m = jnp.max(s, axis=-1, keepdims=True)
                if with_sink:
                    sk = sink_ref[hid] * LOG2E
                    m = jnp.maximum(m, sk)
                e = jnp.exp2(s - m)
                l = jnp.sum(e, axis=-1, keepdims=True)
                if with_sink:
                    l = l + jnp.exp2(sk - m)
                es.append(e.astype(jnp.bfloat16))
                ms.append(m)
                ls.append(l)
            o_all = _dot(jnp.concatenate(es, axis=0), vw[:, ko:ko + bw])
            for n, (hm, (qo, _, _, _)) in enumerate(zip(hms, group)):
                o = o_all[n * bq:(n + 1) * bq] * (1.0 / ls[n])
                acc[qo] = jnp.where(hm, o, acc[qo])
                if with_lse:
                    lacc[qo] = jnp.where(hm, (ms[n] + jnp.log2(ls[n])) * LN2, lacc[qo])
        for qo in q_offs:
            o_ref[pl.ds(q0, bq), qo:qo + bw] = acc[qo].astype(o_ref.dtype)
            if with_lse:
                lse_ref[pl.ds(q0, bq), qo:qo + bw] = lacc[qo]
        return carry

    lax.fori_loop(0, ch // bq, body, 0, unroll=2)


def _band_call(q, k, v, *, heads, bw, bq, hw, seg_len, sink=None, with_lse, out_dtype, name):
    t, wq = q.shape
    wk = k.shape[1]
    ch = SEQ
    kern = functools.partial(_band_kernel, heads=heads, bw=bw, bq=bq, hw=hw, seg_len=seg_len,
                             with_sink=sink is not None, with_lse=with_lse)
    in_specs = [
        pl.BlockSpec((ch, wq), lambda i: (i, 0)),
        pl.BlockSpec((ch, wk), lambda i: (i, 0)),
        pl.BlockSpec((ch, wk), lambda i: (i, 0)),
    ]
    args = [q, k, v]
    if sink is not None:
        in_specs = [pl.BlockSpec(memory_space=pltpu.SMEM)] + in_specs
        args = [sink] + args
    out_shape = [jax.ShapeDtypeStruct((t, wq), out_dtype)]
    out_specs = [pl.BlockSpec((ch, wq), lambda i: (i, 0))]
    if with_lse:
        out_shape.append(jax.ShapeDtypeStruct((t, wq), jnp.float32))
        out_specs.append(pl.BlockSpec((ch, wq), lambda i: (i, 0)))
    return pl.pallas_call(
        kern, grid=(t // ch,), in_specs=in_specs, out_specs=out_specs, out_shape=out_shape,
        compiler_params=_params(1), name=name,
    )(*args)


B_HEAD_SPECS = tuple((0, h * B_HD, 0, h) for h in range(B_HEADS))
D_HEAD_SPECS = tuple(((h // 2) * LANES, (h % 2) * D_HD, (h // 4) * LANES, h) for h in range(D_QHEADS))


def _sigmoid(z):
    return 1.0 / (1.0 + jnp.exp(-z))


def _merge_kernel(x_ref, g1_ref, wg_ref, oa_ref, ob0, ob1, ob2, ls0, ls1, ls2, oc_ref, od_ref,
                  wa_ref, wb_ref, wc_ref, wd_ref, wo_ref, out_ref, *res_scr):
    res_scr = list(res_scr)

    def load(ref):
        if len(ref.shape) == 2:
            return ref[...]
        dil, rows = ref.shape[1], ref.shape[2]
        scr = res_scr.pop()
        n_c = scr.shape[0]
        for r in range(dil):
            for c in range(n_c):
                scr[c, pl.ds(r, rows, stride=dil), :] = ref[0, r, :, c * LANES:(c + 1) * LANES]
        return jnp.concatenate([scr[c] for c in range(n_c)], axis=-1)

    x = x_ref[...]
    h = _rms(x, g1_ref[...]).astype(jnp.bfloat16)
    l0, l1, l2 = load(ls0), load(ls1), load(ls2)
    lm = jnp.maximum(jnp.maximum(l0, l1), l2)
    e0, e1, e2 = jnp.exp(l0 - lm), jnp.exp(l1 - lm), jnp.exp(l2 - lm)
    den = e0 + e1 + e2
    ob = ((e0 / den) * load(ob0) + (e1 / den) * load(ob1) + (e2 / den) * load(ob2)).astype(jnp.bfloat16)
    branches = ((oa_ref[...], wa_ref), (ob, wb_ref), (oc_ref[...], wc_ref), (od_ref[...], wd_ref))
    merged = None
    for i, (o, w_ref) in enumerate(branches):
        gate = _sigmoid(_dot(h, wg_ref[:, i * D_MODEL:(i + 1) * D_MODEL]))
        term = gate * _dot(o, w_ref[...])
        merged = term if merged is None else merged + term
    out_ref[...] = x + _dot(merged.astype(jnp.bfloat16), wo_ref[...])


def _merge_call(x, lw, oa, obs, lses, oc, od):
    t = x.shape[0]
    tm = TM_MERGE

    def tile(w):
        return pl.BlockSpec((tm, w), lambda i: (i, 0))

    n_pos = SEQ // tm

    def band_tile(dil):
        if dil == 1:
            return tile(B_W)
        return pl.BlockSpec((1, dil, tm // dil, B_W), lambda i: (i // n_pos, 0, i % n_pos, 0))

    b_specs = [band_tile(dil) for _, dil in B_PATTERNS]
    in_specs = [tile(D_MODEL), _const_spec((1, D_MODEL)), _const_spec((D_MODEL, N_BRANCH * D_MODEL)),
                tile(512)] + b_specs + b_specs + [tile(512), tile(512),
                _const_spec((512, D_MODEL)), _const_spec((256, D_MODEL)), _const_spec((512, D_MODEL)),
                _const_spec((512, D_MODEL)), _const_spec((D_MODEL, D_MODEL))]
    n_res = 2 * sum(1 for _, dil in B_PATTERNS if dil > 1)
    return pl.pallas_call(
        _merge_kernel, grid=(t // tm,), in_specs=in_specs, out_specs=tile(D_MODEL),
        out_shape=jax.ShapeDtypeStruct((t, D_MODEL), jnp.float32),
        scratch_shapes=[pltpu.VMEM((B_W // LANES, tm, LANES), jnp.float32)] * n_res,
        compiler_params=_params(1), name="merge",
    )(x, lw["g1"], lw["w_gate"], oa, *obs, *lses, oc, od,
      lw["w_br_a"], lw["w_br_b"], lw["w_br_c"], lw["w_br_d"], lw["w_o"])


def _ffn_kernel(x_ref, g2_ref, wg_ref, wu_ref, wd_ref, out_ref):
    x = x_ref[...]
    hf = _rms(x, g2_ref[...]).astype(jnp.bfloat16)
    acc = x
    for off, width in FFN_CHUNKS:
        a = _dot(hf, wg_ref[:, off:off + width])
        u = _dot(hf, wu_ref[:, off:off + width])
        act = (a * _sigmoid(a) * u).astype(jnp.bfloat16)
        acc = acc + _dot(act, wd_ref[off:off + width, :])
    out_ref[...] = acc


def _ffn_call(x, lw):
    t = x.shape[0]
    tm = TM_FFN
    tile = pl.BlockSpec((tm, D_MODEL), lambda i: (i, 0))
    return pl.pallas_call(
        _ffn_kernel, grid=(t // tm,),
        in_specs=[tile, _const_spec((1, D_MODEL)), _const_spec((D_MODEL, D_FF)),
                  _const_spec((D_MODEL, D_FF)), _const_spec((D_FF, D_MODEL))],
        out_specs=tile, out_shape=jax.ShapeDtypeStruct((t, D_MODEL), jnp.float32),
        compiler_params=_params(1), name="ffn",
    )(x, lw["g2"], lw["w_ffn_gate"], lw["w_ffn_up"], lw["w_ffn_down"])


def _tables():
    pos = jnp.arange(SEQ, dtype=jnp.float32)[:, None]
    lane = jnp.arange(LANES)
    inv64 = jnp.power(ROPE_THETA, -jnp.arange(32, dtype=jnp.float32) / 32)
    ang64 = pos * inv64[lane % 32][None, :]
    sign64 = jnp.where((lane % 64) < 32, -1.0, 1.0)[None, :]
    cos64 = jnp.cos(ang64)
    sin64 = jnp.sin(ang64) * sign64
    invc = jnp.power(ROPE_THETA, -jnp.arange(16, dtype=jnp.float32) / 16)
    angc = pos * invc[lane % 16][None, :]
    is_rope = ((lane >= C_NOPE) & (lane < C_NOPE + C_ROPE))[None, :]
    signc = jnp.where(lane < C_NOPE + C_ROPE // 2, -1.0, 1.0)[None, :]
    cosc = jnp.where(is_rope, jnp.cos(angc), 1.0)
    sinc = jnp.where(is_rope, jnp.sin(angc) * signc, 0.0)
    cs = jnp.stack([cos64, sin64, cosc, sinc]).astype(jnp.float32)
    idx = np.arange(MXU_N)
    bd = np.stack([(idx[:, None] // 64) == (idx[None, :] // 64),
                   (idx[:, None] // 128) == (idx[None, :] // 128)]).astype(np.float32)
    return {"cs": cs, "bd": jnp.asarray(bd, dtype=jnp.bfloat16)}


def _layer_weights(l, p):
    bf = jnp.bfloat16
    f32 = jnp.float32
    cols = jnp.split(p["w_in"][l], SPLIT_IDX, axis=-1)
    zeros = lambda n: jnp.zeros((D_MODEL, n), f32)
    dk, dv = cols[16], cols[17]
    w_in = jnp.concatenate(
        list(cols[0:12]) + [cols[12], cols[13], zeros(64), cols[14], zeros(32), cols[15],
                            dk[:, :64], dk[:, :64], dk[:, 64:], dk[:, 64:],
                            dv[:, :64], dv[:, :64], dv[:, 64:], dv[:, 64:]], axis=-1).astype(bf)
    ones = lambda n: jnp.ones((n,), f32)
    qs = A_HD ** -0.5 * LOG2E
    grow = jnp.concatenate(
        [jnp.tile(p["a_qnorm_g"][l], 8) * qs, jnp.tile(p["a_knorm_g"][l], 8), ones(512)]
        + sum([[jnp.tile(p["b_qnorm_g"][l, g], 4) * qs, jnp.tile(p["b_knorm_g"][l, g], 4), ones(256)]
               for g in range(3)], [])
        + [ones(512), jnp.tile(p["d_qnorm_g"][l], 8) * qs, jnp.tile(p["d_knorm_g"][l], 4), ones(256)]
    )[None, :].astype(f32)
    wuq = p["c_w_uq"][l].reshape(C_Q_RANK, C_HEADS, C_NOPE + C_ROPE)
    wuq = jnp.pad(wuq, ((0, 0), (0, 0), (0, 32))).reshape(C_Q_RANK, C_HEADS * LANES).astype(bf)
    wukv = p["c_w_ukv"][l].reshape(C_KV_RANK, C_HEADS, C_NOPE + C_VD)
    wuk = jnp.pad(wukv[:, :, :C_NOPE], ((0, 0), (0, 0), (0, 64))).reshape(C_KV_RANK, C_HEADS * LANES).astype(bf)
    wuv = wukv[:, :, C_NOPE:].reshape(C_KV_RANK, C_HEADS * C_VD).astype(bf)
    cscale = (C_NOPE + C_ROPE) ** -0.5 * LOG2E
    pad32 = lambda g: jnp.tile(jnp.pad(g, (0, 32)), C_HEADS)[None, :].astype(f32)
    return {
        "g1": p["norm1_g"][l][None, :], "w_in": w_in, "grow": grow,
        "qag": p["c_qa_norm_g"][l][None, :], "kvag": p["c_kva_norm_g"][l][None, :],
        "wuq": wuq, "wuk": wuk, "wuv": wuv,
        "gqc": pad32(p["c_qnorm_g"][l]) * cscale, "gkc": pad32(p["c_knorm_g"][l]),
        "a_lambda": p["a_lambda"][l], "subln": p["a_subln_g"][l][None, :],
        "d_sink": p["d_sink"][l],
        "w_gate": p["w_gate"][l].astype(bf),
        "w_br_a": p["w_br_a"][l].astype(bf), "w_br_b": p["w_br_b"][l].astype(bf),
        "w_br_c": p["w_br_c"][l].astype(bf), "w_br_d": p["w_br_d"][l].astype(bf),
        "w_o": p["w_o"][l].astype(bf), "g2": p["norm2_g"][l][None, :],
        "w_ffn_gate": p["w_ffn_gate"][l].astype(bf), "w_ffn_up": p["w_ffn_up"][l].astype(bf),
        "w_ffn_down": p["w_ffn_down"][l].astype(bf),
    }


def _layer(x, l, lw, tables):
    (qa, ka, va, qb0, kb0, vb0, qb1, kb1, vb1, qb2, kb2, vb2, qc, kc, vc, qd, kd, vd) = _proj_call(x, lw, tables)
    oa = _attn_a_call(qa, ka, va, lw["a_lambda"], lw["subln"], lambda_init(l))
    obs, lses = [], []
    for g, (qg, kg, vg) in enumerate(((qb0, kb0, vb0), (qb1, kb1, vb1), (qb2, kb2, vb2))):
        window, dil = B_PATTERNS[g]
        res_shape = qg.shape
        qg, kg, vg = (a.reshape(-1, B_W) for a in (qg, kg, vg))
        o, lse = _band_call(qg, kg, vg, heads=B_HEAD_SPECS, bw=B_W, bq=128, hw=window // (2 * dil),
                            seg_len=SEQ // dil, with_lse=True, out_dtype=jnp.float32, name=f"band_b{g}")
        obs.append(o.reshape(res_shape))
        lses.append(lse.reshape(res_shape))
    oc = _attn_c_call(qc, kc, vc)
    od = _band_call(qd, kd, vd, heads=D_HEAD_SPECS, bw=LANES, bq=128, hw=D_WIN, seg_len=SEQ,
                    sink=lw["d_sink"], with_lse=False, out_dtype=jnp.bfloat16, name="band_d")[0]
    x = _merge_call(x, lw, oa, obs, lses, oc, od)
    return _ffn_call(x, lw)


def kernel(x_prompt, x_sample, norm1_g, w_in, w_gate, a_qnorm_g, a_knorm_g, a_lambda, a_subln_g, b_qnorm_g, b_knorm_g, c_qa_norm_g, c_kva_norm_g, c_w_uq, c_w_ukv, c_qnorm_g, c_knorm_g, d_qnorm_g, d_knorm_g, d_sink, w_br_a, w_br_b, w_br_c, w_br_d, w_o, norm2_g, w_ffn_gate, w_ffn_up, w_ffn_down):
    p = dict(norm1_g=norm1_g, w_in=w_in, w_gate=w_gate, a_qnorm_g=a_qnorm_g, a_knorm_g=a_knorm_g,
             a_lambda=a_lambda, a_subln_g=a_subln_g, b_qnorm_g=b_qnorm_g, b_knorm_g=b_knorm_g,
             c_qa_norm_g=c_qa_norm_g, c_kva_norm_g=c_kva_norm_g, c_w_uq=c_w_uq, c_w_ukv=c_w_ukv,
             c_qnorm_g=c_qnorm_g, c_knorm_g=c_knorm_g, d_qnorm_g=d_qnorm_g, d_knorm_g=d_knorm_g,
             d_sink=d_sink, w_br_a=w_br_a, w_br_b=w_br_b, w_br_c=w_br_c, w_br_d=w_br_d, w_o=w_o,
             norm2_g=norm2_g, w_ffn_gate=w_ffn_gate, w_ffn_up=w_ffn_up, w_ffn_down=w_ffn_down)
    n_prompt = x_prompt.shape[0] * x_prompt.shape[1]
    x = jnp.concatenate([x_prompt.reshape(-1, D_MODEL), x_sample.reshape(-1, D_MODEL)], axis=0)
    tables = _tables()
    for l in range(DEPTH):
        x = _layer(x, l, _layer_weights(l, p), tables)
    return (x[:n_prompt].reshape(x_prompt.shape), x[n_prompt:].reshape(x_sample.shape))
```

```python
import functools
import math

import jax
import jax.numpy as jnp
import numpy as np
from jax import lax
from jax.experimental import pallas as pl
from jax.experimental.pallas import tpu as pltpu

D_MODEL = 1024
SEQ = 4096
DEPTH = 2
ROPE_THETA = 10000.0
EPS = 1e-6
NEG_INF = -1e30
N_BRANCH = 4
LOG2E = 1.4426950408889634
LN2 = 0.6931471805599453

A_HEADS = 4
A_HD = 64
B_PATTERNS = ((128, 1), (512, 4), (2048, 16))
B_HEADS = 4
B_HD = 64
B_W = B_HEADS * B_HD
C_HEADS = 8
C_Q_RANK = 256
C_KV_RANK = 128
C_NOPE = 64
C_ROPE = 32
C_VD = 64
D_QHEADS = 8
D_KVHEADS = 2
D_HD = 64
D_WIN = 128
D_FF = -(-8 * D_MODEL // (3 * 256)) * 256

IN_SIZES = (512, 512, 512) + (B_W,) * 9 + (C_Q_RANK, C_KV_RANK, C_ROPE, 512, 128, 128)
SPLIT_IDX = tuple(int(i) for i in np.cumsum(IN_SIZES)[:-1])

LANES = 128
MXU_N = 256

OFF_AQ, OFF_AK, OFF_AV = 0, 512, 1024
OFF_B = 1536
OFF_CQ = 3840
OFF_CKV = 4096
OFF_KR = 4224
OFF_DQ = 4352
OFF_DK = 4864
OFF_DV = 5120
N_PROJ = 5376

VMEM_LIMIT = 56 * 1024 * 1024

TM_PROJ = 512
TQ_A = 256
TQ_C = 256
HEADS_PER_STEP_A = 4
HEADS_PER_STEP_C = 4
TK_ATTN = 4096
TM_MERGE = 256
TM_FFN = 256
FFN_CHUNKS = ((0, 1024), (1024, 1024), (2048, 768))


def lambda_init(layer):
    return 0.8 - 0.6 * math.exp(-0.3 * layer)


def _const_spec(shape):
    nd = len(shape)
    return pl.BlockSpec(shape, lambda *_: (0,) * nd, pipeline_mode=pl.Buffered(1))


def _params(n_grid):
    return pltpu.CompilerParams(dimension_semantics=("arbitrary",) * n_grid,
                                vmem_limit_bytes=VMEM_LIMIT)


def _dot(a, b):
    return jnp.dot(a, b, preferred_element_type=jnp.float32)


def _dot_nt(a, b):
    return lax.dot_general(a, b, (((1,), (1,)), ((), ())), preferred_element_type=jnp.float32)


def _rms(x, g):
    ms = jnp.mean(x * x, axis=-1, keepdims=True)
    return x * lax.rsqrt(ms + EPS) * g


def _row_specs(row_counts, tm, width):
    if len(row_counts) == 1:
        return [pl.BlockSpec((tm, width), lambda i: (i, 0))]
    nb = row_counts[0] // tm
    return [pl.BlockSpec((tm, width), lambda i: (jnp.minimum(i, nb - 1), 0)),
            pl.BlockSpec((tm, width), lambda i: (jnp.maximum(i - nb, 0), 0))]


def _row_tile(refs, first_blocks):
    if len(refs) == 1:
        return refs[0][...]
    return jnp.where(pl.program_id(0) < first_blocks, refs[0][...], refs[1][...])


def _group_sumsq(p, bd):
    outs = []
    for c in range(p.shape[1] // MXU_N):
        pc = p[:, c * MXU_N:(c + 1) * MXU_N]
        outs.append(_dot((pc * pc).astype(jnp.bfloat16), bd))
    return outs[0] if len(outs) == 1 else jnp.concatenate(outs, axis=-1)


def _rope_chunks(y, cos, sin, first_half, shift):
    outs = []
    for c in range(y.shape[1] // LANES):
        yc = y[:, c * LANES:(c + 1) * LANES]
        sw = jnp.where(first_half, pltpu.roll(yc, LANES - shift, 1), pltpu.roll(yc, shift, 1))
        outs.append(yc * cos + sw * sin)
    return outs[0] if len(outs) == 1 else jnp.concatenate(outs, axis=-1)


def _proj_kernel(*refs, n_x, first_blocks):
    x_refs, refs = refs[:n_x], refs[n_x:]
    (g1_ref, w_ref, grow_ref, bd_ref, cs_ref, qag_ref, kvag_ref,
     wuq_ref, wuk_ref, wuv_ref, gqc_ref, gkc_ref,
     qa, ka, va, qb0, kb0, vb0, qb1, kb1, vb1, qb2, kb2, vb2,
     qc, kc, vc, qd, kd, vd) = refs[:30]
    res_scr = list(refs[30:])
    tm = x_refs[0].shape[0]
    h = _rms(_row_tile(x_refs, first_blocks), g1_ref[...]).astype(jnp.bfloat16)
    bd64 = bd_ref[0]
    bd128 = bd_ref[1]
    cos64, sin64, cosc, sinc = cs_ref[0], cs_ref[1], cs_ref[2], cs_ref[3]
    lane = lax.broadcasted_iota(jnp.int32, (tm, LANES), 1)
    first64 = (lane & 63) < 32
    firstc = lane < (C_NOPE + C_ROPE // 2)

    def proj(off, width):
        return _dot(h, w_ref[:, off:off + width])

    def store(out_ref, val):
        if len(out_ref.shape) == 2:
            out_ref[...] = val.astype(out_ref.dtype)
            return
        dil, rows = out_ref.shape[1], out_ref.shape[2]
        scr = res_scr.pop()
        n_c = scr.shape[0]
        for c in range(n_c):
            scr[c] = val[:, c * LANES:(c + 1) * LANES]
        for r in range(dil):
            out_ref[0, r] = jnp.concatenate(
                [scr[c, pl.ds(r, rows, stride=dil), :] for c in range(n_c)], axis=-1).astype(out_ref.dtype)

    def norm_rope64(off, width, out_ref):
        p = proj(off, width)
        ss = _group_sumsq(p, bd64)
        y = p * lax.rsqrt(ss * (1.0 / 64) + EPS) * grow_ref[:, off:off + width]
        store(out_ref, _rope_chunks(y, cos64, sin64, first64, 32))

    def plain(off, width, out_ref):
        store(out_ref, proj(off, width))

    norm_rope64(OFF_AQ, 512, qa)
    norm_rope64(OFF_AK, 512, ka)
    plain(OFF_AV, 512, va)
    for g, (qo, ko, vo) in enumerate(((qb0, kb0, vb0), (qb1, kb1, vb1), (qb2, kb2, vb2))):
        base = OFF_B + g * 768
        norm_rope64(base, 256, qo)
        norm_rope64(base + 256, 256, ko)
        plain(base + 512, 256, vo)
    norm_rope64(OFF_DQ, 512, qd)
    norm_rope64(OFF_DK, 256, kd)
    plain(OFF_DV, 256, vd)

    cqn = _rms(proj(OFF_CQ, C_Q_RANK), qag_ref[...]).astype(jnp.bfloat16)
    qfull = _dot(cqn, wuq_ref[...])
    qn = qfull * lax.rsqrt(_group_sumsq(qfull, bd128) * (1.0 / 96) + EPS) * gqc_ref[...]
    qc[...] = _rope_chunks(qn, cosc, sinc, firstc, C_ROPE // 2).astype(qc.dtype)
    ckvn = _rms(proj(OFF_CKV, C_KV_RANK), kvag_ref[...]).astype(jnp.bfloat16)
    kr = proj(OFF_KR, LANES)
    kfull = _dot(ckvn, wuk_ref[...]) + jnp.concatenate([kr] * C_HEADS, axis=-1)
    kn = kfull * lax.rsqrt(_group_sumsq(kfull, bd128) * (1.0 / 96) + EPS) * gkc_ref[...]
    kc[...] = _rope_chunks(kn, cosc, sinc, firstc, C_ROPE // 2).astype(kc.dtype)
    vc[...] = _dot(ckvn, wuv_ref[...]).astype(vc.dtype)


def _proj_call(xs, lw, tables):
    rows = [x.shape[0] for x in xs]
    t = sum(rows)
    tm = TM_PROJ
    n_pos = SEQ // tm
    widths = (512, 512, 512) + (256,) * 9 + (1024, 1024, 512, 512, 256, 256)
    out_shape = [jax.ShapeDtypeStruct((t, w), jnp.bfloat16) for w in widths]
    out_specs = [pl.BlockSpec((tm, w), lambda i: (i, 0)) for w in widths]
    for g, (_, dil) in enumerate(B_PATTERNS):
        if dil > 1:
            for j in range(3 + 3 * g, 6 + 3 * g):
                out_shape[j] = jax.ShapeDtypeStruct((t // SEQ, dil, SEQ // dil, B_W), jnp.bfloat16)
                out_specs[j] = pl.BlockSpec((1, dil, tm // dil, B_W), lambda i: (i // n_pos, 0, i % n_pos, 0))
    n_res = 3 * sum(1 for _, dil in B_PATTERNS if dil > 1)
    in_specs = _row_specs(rows, tm, D_MODEL) + [
        _const_spec((1, D_MODEL)),
        _const_spec((D_MODEL, N_PROJ)),
        _const_spec((1, N_PROJ)),
        _const_spec((2, MXU_N, MXU_N)),
        pl.BlockSpec((4, tm, LANES), lambda i: (0, i % n_pos, 0)),
        _const_spec((1, C_Q_RANK)),
        _const_spec((1, C_KV_RANK)),
        _const_spec((C_Q_RANK, 1024)),
        _const_spec((C_KV_RANK, 1024)),
        _const_spec((C_KV_RANK, 512)),
        _const_spec((1, 1024)),
        _const_spec((1, 1024)),
    ]
    return pl.pallas_call(
        functools.partial(_proj_kernel, n_x=len(xs), first_blocks=rows[0] // tm),
        grid=(t // tm,), in_specs=in_specs, out_specs=out_specs, out_shape=out_shape,
        scratch_shapes=[pltpu.VMEM((B_W // LANES, tm, LANES), jnp.float32)] * n_res,
        compiler_params=_params(1), name="proj",
    )(*xs, lw["g1"], lw["w_in"], lw["grow"], tables["bd"], tables["cs"], lw["qag"], lw["kvag"],
      lw["wuq"], lw["wuk"], lw["wuv"], lw["gqc"], lw["gkc"])


def _lane_fold(x, op):
    acc = x[:, :LANES]
    for t in range(1, x.shape[1] // LANES):
        acc = op(acc, x[:, t * LANES:(t + 1) * LANES])
    return acc


def _online_step(state, s, v):
    r = jnp.max(_lane_fold(s, jnp.maximum), axis=-1, keepdims=True)
    if state is None:
        e = jnp.exp2(s - r)
        return r, _lane_fold(e, jnp.add), _dot(e.astype(jnp.bfloat16), v)
    m, l, acc = state
    m_new = jnp.maximum(m, r)
    alpha = jnp.exp2(m - m_new)
    e = jnp.exp2(s - m_new)
    return m_new, alpha * l + _lane_fold(e, jnp.add), alpha * acc + _dot(e.astype(jnp.bfloat16), v)


def _online_finish(state):
    _, l, acc = state
    return acc * (1.0 / jnp.sum(l, axis=-1, keepdims=True))


def _attn_a_kernel(lam_ref, q_ref, k_ref, v_ref, g_ref, o_ref, *, lam_init):
    lp = lam_ref[...]
    lam = (jnp.exp(jnp.sum(lp[0:1] * lp[1:2], axis=-1, keepdims=True))
           - jnp.exp(jnp.sum(lp[2:3] * lp[3:4], axis=-1, keepdims=True)) + lam_init)
    lane = lax.broadcasted_iota(jnp.int32, (q_ref.shape[0], LANES), 1)
    n_heads = q_ref.shape[1] // LANES

    def scores(h):
        hs = slice(h * LANES, (h + 1) * LANES)
        q = q_ref[:, hs]
        k = k_ref[:, hs]
        return [_dot_nt(jnp.where((lane >= c * A_HD) & (lane < (c + 1) * A_HD), q, jnp.zeros_like(q)), k)
                for c in range(2)]

    s_next = scores(0)
    for h in range(n_heads):
        hs = slice(h * LANES, (h + 1) * LANES)
        s_cur = s_next
        if h + 1 < n_heads:
            s_next = scores(h + 1)
        es, inv = [], []
        for s in s_cur:
            e = jnp.exp2(s - jnp.max(s, axis=-1, keepdims=True))
            es.append(e)
            inv.append(1.0 / jnp.sum(e, axis=-1, keepdims=True))
        w = es[0] * inv[0] - es[1] * (lam * inv[1])
        o = _dot(w.astype(jnp.bfloat16), v_ref[:, hs])
        o = _rms(o, g_ref[...]) * (1.0 - lam_init)
        o_ref[:, hs] = o.astype(o_ref.dtype)


def _attn_a_call(q, k, v, a_lambda, subln_row, lam_init):
    t = q.shape[0]
    nseq = t // SEQ
    tq = TQ_A
    nq = SEQ // tq
    w = HEADS_PER_STEP_A * LANES
    return pl.pallas_call(
        functools.partial(_attn_a_kernel, lam_init=lam_init),
        grid=(nseq, A_HEADS // HEADS_PER_STEP_A, nq),
        in_specs=[
            _const_spec((4, A_HD)),
            pl.BlockSpec((tq, w), lambda s, h, i: (s * nq + i, h)),
            pl.BlockSpec((SEQ, w), lambda s, h, i: (s, h)),
            pl.BlockSpec((SEQ, w), lambda s, h, i: (s, h)),
            _const_spec((1, LANES)),
        ],
        out_specs=pl.BlockSpec((tq, w), lambda s, h, i: (s * nq + i, h)),
        out_shape=jax.ShapeDtypeStruct((t, A_HEADS * LANES), jnp.bfloat16),
        compiler_params=_params(3), name="attn_a",
    )(a_lambda, q, k, v, subln_row)


def _attn_c_kernel(q_ref, k_ref, v_ref, o_ref):
    lane = lax.broadcasted_iota(jnp.int32, (o_ref.shape[0], LANES), 1)
    n_heads = q_ref.shape[1] // LANES
    def scores(h):
        hs = slice(h * LANES, (h + 1) * LANES)
        return _dot_nt(q_ref[:, hs], k_ref[:, hs])

    states = []
    s_next = scores(0)
    for h in range(n_heads):
        s = s_next
        if h + 1 < n_heads:
            s_next = scores(h + 1)
        v = v_ref[:, (h // 2) * LANES:(h // 2 + 1) * LANES]
        half = s.shape[0] // 2
        parts = [_online_step(None, s[r:r + half], v) for r in (0, half)]
        states.append(tuple(jnp.concatenate([a, b], axis=0) for a, b in zip(*parts)))
    for j in range(n_heads // 2):
        o_ref[:, j * LANES:(j + 1) * LANES] = jnp.where(
            lane < C_VD, _online_finish(states[2 * j]), _online_finish(states[2 * j + 1])).astype(o_ref.dtype)


def _attn_c_call(q, k, v):
    t = q.shape[0]
    nseq = t // SEQ
    tq = TQ_C
    nq = SEQ // tq
    hps = HEADS_PER_STEP_C
    return pl.pallas_call(
        _attn_c_kernel,
        grid=(nseq, C_HEADS // hps, nq),
        in_specs=[
            pl.BlockSpec((tq, hps * LANES), lambda s, j, i: (s * nq + i, j)),
            pl.BlockSpec((SEQ, hps * LANES), lambda s, j, i: (s, j)),
            pl.BlockSpec((SEQ, hps * C_VD), lambda s, j, i: (s, j)),
        ],
        out_specs=pl.BlockSpec((tq, hps * C_VD), lambda s, j, i: (s * nq + i, j)),
        out_shape=jax.ShapeDtypeStruct((t, C_HEADS * C_VD), jnp.bfloat16),
        compiler_params=_params(3), name="attn_c",
    )(q, k, v)


def _band_kernel(*refs, heads, bw, bq, hw, seg_len, with_sink, with_lse):
    if with_sink:
        sink_ref, q_ref, k_ref, v_ref = refs[:4]
        outs = refs[4:]
    else:
        q_ref, k_ref, v_ref = refs[:3]
        outs = refs[3:]
    o_ref = outs[0]
    lse_ref = outs[1] if with_lse else None
    ch = q_ref.shape[0]
    win = bq + 2 * hw
    shift = int(math.log2(seg_len))
    lane = lax.broadcasted_iota(jnp.int32, (bq, bw), 1)
    row = lax.broadcasted_iota(jnp.int32, (bq, win), 0)
    col = lax.broadcasted_iota(jnp.int32, (bq, win), 1)
    q_offs = sorted({hd[0] for hd in heads})
    k_offs = sorted({hd[2] for hd in heads})

    def body(i, carry):
        q0 = pl.multiple_of(i * bq, bq)
        ws = pl.multiple_of(jnp.clip(q0 - hw, 0, ch - win), hw)
        qi = q0 + row
        kj = ws + col
        d = qi - kj
        valid = (d <= hw) & (d >= -hw) & ((qi >> shift) == (kj >> shift))
        bias = jnp.where(valid, 0.0, NEG_INF)
        qb = q_ref[pl.ds(q0, bq), :]
        kw = k_ref[pl.ds(ws, win), :]
        vw = v_ref[pl.ds(ws, win), :]
        acc = {qo: jnp.zeros((bq, bw), jnp.float32) for qo in q_offs}
        lacc = {qo: jnp.zeros((bq, bw), jnp.float32) for qo in q_offs}
        for ko in k_offs:
            group = [hd for hd in heads if hd[2] == ko]
            hms = [(lane >= mo) & (lane < mo + 64) for _, mo, _, _ in group]
            qs = jnp.concatenate(
                [jnp.where(hm, qb[:, qo:qo + bw], jnp.zeros((bq, bw), qb.dtype))
                 for hm, (qo, _, _, _) in zip(hms, group)], axis=0)
            s_all = _dot_nt(qs, kw[:, ko:ko + bw])
            es, ms, ls = [], [], []
            for n, (_, _, _, hid) in enumerate(group):
                s = s_all[n * bq:(n + 1) * bq] + bias
                m = jnp.max(s, axis=-1, keepdims=True)
                if with_sink:
                    sk = sink_ref[hid] * LOG2E
                    m = jnp.maximum(m, sk)
                e = jnp.exp2(s - m)
                l = jnp.sum(e, axis=-1, keepdims=True)
                if with_sink:
                    l = l + jnp.exp2(sk - m)
                es.append(e.astype(jnp.bfloat16))
                ms.append(m)
                ls.append(l)
            o_all = _dot(jnp.concatenate(es, axis=0), vw[:, ko:ko + bw])
            for n, (hm, (qo, _, _, _)) in enumerate(zip(hms, group)):
                o = o_all[n * bq:(n + 1) * bq] * (1.0 / ls[n])
                acc[qo] = jnp.where(hm, o, acc[qo])
                if with_lse:
                    lacc[qo] = jnp.where(hm, (ms[n] + jnp.log2(ls[n])) * LN2, lacc[qo])
        for qo in q_offs:
            o_ref[pl.ds(q0, bq), qo:qo + bw] = acc[qo].astype(o_ref.dtype)
            if with_lse:
                lse_ref[pl.ds(q0, bq), qo:qo + bw] = lacc[qo]
        return carry

    lax.fori_loop(0, ch // bq, body, 0, unroll=2)


def _band_call(q, k, v, *, heads, bw, bq, hw, seg_len, sink=None, with_lse, out_dtype, name):
    t, wq = q.shape
    wk = k.shape[1]
    ch = SEQ
    kern = functools.partial(_band_kernel, heads=heads, bw=bw, bq=bq, hw=hw, seg_len=seg_len,
                             with_sink=sink is not None, with_lse=with_lse)
    in_specs = [
        pl.BlockSpec((ch, wq), lambda i: (i, 0)),
        pl.BlockSpec((ch, wk), lambda i: (i, 0)),
        pl.BlockSpec((ch, wk), lambda i: (i, 0)),
    ]
    args = [q, k, v]
    if sink is not None:
        in_specs = [pl.BlockSpec(memory_space=pltpu.SMEM)] + in_specs
        args = [sink] + args
    out_shape = [jax.ShapeDtypeStruct((t, wq), out_dtype)]
    out_specs = [pl.BlockSpec((ch, wq), lambda i: (i, 0))]
    if with_lse:
        out_shape.append(jax.ShapeDtypeStruct((t, wq), jnp.float32))
        out_specs.append(pl.BlockSpec((ch, wq), lambda i: (i, 0)))
    return pl.pallas_call(
        kern, grid=(t // ch,), in_specs=in_specs, out_specs=out_specs, out_shape=out_shape,
        compiler_params=_params(1), name=name,
    )(*args)


B_HEAD_SPECS = tuple((0, h * B_HD, 0, h) for h in range(B_HEADS))
D_HEAD_SPECS = tuple(((h // 2) * LANES, (h % 2) * D_HD, (h // 4) * LANES, h) for h in range(D_QHEADS))


def _sigmoid(z):
    return 1.0 / (1.0 + jnp.exp(-z))


def _merge_kernel(*refs, n_x, first_blocks):
    x_refs, refs = refs[:n_x], refs[n_x:]
    (g1_ref, wg_ref, oa_ref, ob0, ob1, ob2, ls0, ls1, ls2, oc_ref, od_ref,
     wa_ref, wb_ref, wc_ref, wd_ref, wo_ref, out_ref) = refs[:17]
    res_scr = list(refs[17:])

    def load(ref):
        if len(ref.shape) == 2:
            return ref[...]
        dil, rows = ref.shape[1], ref.shape[2]
        scr = res_scr.pop()
        n_c = scr.shape[0]
        for r in range(dil):
            for c in range(n_c):
                scr[c, pl.ds(r, rows, stride=dil), :] = ref[0, r, :, c * LANES:(c + 1) * LANES]
        return jnp.concatenate([scr[c] for c in range(n_c)], axis=-1)

    x = _row_tile(x_refs, first_blocks)
    h = _rms(x, g1_ref[...]).astype(jnp.bfloat16)
    l0, l1, l2 = load(ls0), load(ls1), load(ls2)
    lm = jnp.maximum(jnp.maximum(l0, l1), l2)
    e0, e1, e2 = jnp.exp(l0 - lm), jnp.exp(l1 - lm), jnp.exp(l2 - lm)
    den = e0 + e1 + e2
    ob = ((e0 / den) * load(ob0) + (e1 / den) * load(ob1) + (e2 / den) * load(ob2)).astype(jnp.bfloat16)
    branches = ((oa_ref[...], wa_ref), (ob, wb_ref), (oc_ref[...], wc_ref), (od_ref[...], wd_ref))
    merged = None
    for i, (o, w_ref) in enumerate(branches):
        gate = _sigmoid(_dot(h, wg_ref[:, i * D_MODEL:(i + 1) * D_MODEL]))
        term = gate * _dot(o, w_ref[...])
        merged = term if merged is None else merged + term
    out_ref[...] = x + _dot(merged.astype(jnp.bfloat16), wo_ref[...])


def _merge_call(xs, lw, oa, obs, lses, oc, od):
    rows = [x.shape[0] for x in xs]
    t = sum(rows)
    tm = TM_MERGE

    def tile(w):
        return pl.BlockSpec((tm, w), lambda i: (i, 0))

    n_pos = SEQ // tm

    def band_tile(dil):
        if dil == 1:
            return tile(B_W)
        return pl.BlockSpec((1, dil, tm // dil, B_W), lambda i: (i // n_pos, 0, i % n_pos, 0))

    b_specs = [band_tile(dil) for _, dil in B_PATTERNS]
    in_specs = _row_specs(rows, tm, D_MODEL) + [
                _const_spec((1, D_MODEL)), _const_spec((D_MODEL, N_BRANCH * D_MODEL)),
                tile(512)] + b_specs + b_specs + [tile(512), tile(512),
                _const_spec((512, D_MODEL)), _const_spec((256, D_MODEL)), _const_spec((512, D_MODEL)),
                _const_spec((512, D_MODEL)), _const_spec((D_MODEL, D_MODEL))]
    n_res = 2 * sum(1 for _, dil in B_PATTERNS if dil > 1)
    return pl.pallas_call(
        functools.partial(_merge_kernel, n_x=len(xs), first_blocks=rows[0] // tm),
        grid=(t // tm,), in_specs=in_specs, out_specs=tile(D_MODEL),
        out_shape=jax.ShapeDtypeStruct((t, D_MODEL), jnp.float32),
        scratch_shapes=[pltpu.VMEM((B_W // LANES, tm, LANES), jnp.float32)] * n_res,
        compiler_params=_params(1), name="merge",
    )(*xs, lw["g1"], lw["w_gate"], oa, *obs, *lses, oc, od,
      lw["w_br_a"], lw["w_br_b"], lw["w_br_c"], lw["w_br_d"], lw["w_o"])


def _ffn_kernel(x_ref, g2_ref, wg_ref, wu_ref, wd_ref, *out_refs, first_blocks):
    x = x_ref[...]
    hf = _rms(x, g2_ref[...]).astype(jnp.bfloat16)
    acc = x
    for off, width in FFN_CHUNKS:
        a = _dot(hf, wg_ref[:, off:off + width])
        u = _dot(hf, wu_ref[:, off:off + width])
        act = (a * _sigmoid(a) * u).astype(jnp.bfloat16)
        acc = acc + _dot(act, wd_ref[off:off + width, :])
    if len(out_refs) == 1:
        out_refs[0][...] = acc
    else:
        @pl.when(pl.program_id(0) < first_blocks)
        def _():
            out_refs[0][...] = acc

        @pl.when(pl.program_id(0) >= first_blocks)
        def _():
            out_refs[1][...] = acc


def _ffn_call(x, lw, out_rows):
    t = x.shape[0]
    tm = TM_FFN
    tile = pl.BlockSpec((tm, D_MODEL), lambda i: (i, 0))
    return pl.pallas_call(
        functools.partial(_ffn_kernel, first_blocks=out_rows[0] // tm), grid=(t // tm,),
        in_specs=[tile, _const_spec((1, D_MODEL)), _const_spec((D_MODEL, D_FF)),
                  _const_spec((D_MODEL, D_FF)), _const_spec((D_FF, D_MODEL))],
        out_specs=_row_specs(out_rows, tm, D_MODEL),
        out_shape=[jax.ShapeDtypeStruct((r, D_MODEL), jnp.float32) for r in out_rows],
        compiler_params=_params(1), name="ffn",
    )(x, lw["g2"], lw["w_ffn_gate"], lw["w_ffn_up"], lw["w_ffn_down"])


def _tables():
    pos = jnp.arange(SEQ, dtype=jnp.float32)[:, None]
    lane = jnp.arange(LANES)
    inv64 = jnp.power(ROPE_THETA, -jnp.arange(32, dtype=jnp.float32) / 32)
    ang64 = pos * inv64[lane % 32][None, :]
    sign64 = jnp.where((lane % 64) < 32, -1.0, 1.0)[None, :]
    cos64 = jnp.cos(ang64)
    sin64 = jnp.sin(ang64) * sign64
    invc = jnp.power(ROPE_THETA, -jnp.arange(16, dtype=jnp.float32) / 16)
    angc = pos * invc[lane % 16][None, :]
    is_rope = ((lane >= C_NOPE) & (lane < C_NOPE + C_ROPE))[None, :]
    signc = jnp.where(lane < C_NOPE + C_ROPE // 2, -1.0, 1.0)[None, :]
    cosc = jnp.where(is_rope, jnp.cos(angc), 1.0)
    sinc = jnp.where(is_rope, jnp.sin(angc) * signc, 0.0)
    cs = jnp.stack([cos64, sin64, cosc, sinc]).astype(jnp.float32)
    idx = np.arange(MXU_N)
    bd = np.stack([(idx[:, None] // 64) == (idx[None, :] // 64),
                   (idx[:, None] // 128) == (idx[None, :] // 128)]).astype(np.float32)
    return {"cs": cs, "bd": jnp.asarray(bd, dtype=jnp.bfloat16)}


def _layer_weights(l, p):
    bf = jnp.bfloat16
    f32 = jnp.float32
    cols = jnp.split(p["w_in"][l], SPLIT_IDX, axis=-1)
    zeros = lambda n: jnp.zeros((D_MODEL, n), f32)
    dk, dv = cols[16], cols[17]
    w_in = jnp.concatenate(
        list(cols[0:12]) + [cols[12], cols[13], zeros(64), cols[14], zeros(32), cols[15],
                            dk[:, :64], dk[:, :64], dk[:, 64:], dk[:, 64:],
                            dv[:, :64], dv[:, :64], dv[:, 64:], dv[:, 64:]], axis=-1).astype(bf)
    ones = lambda n: jnp.ones((n,), f32)
    qs = A_HD ** -0.5 * LOG2E
    grow = jnp.concatenate(
        [jnp.tile(p["a_qnorm_g"][l], 8) * qs, jnp.tile(p["a_knorm_g"][l], 8), ones(512)]
        + sum([[jnp.tile(p["b_qnorm_g"][l, g], 4) * qs, jnp.tile(p["b_knorm_g"][l, g], 4), ones(256)]
               for g in range(3)], [])
        + [ones(512), jnp.tile(p["d_qnorm_g"][l], 8) * qs, jnp.tile(p["d_knorm_g"][l], 4), ones(256)]
    )[None, :].astype(f32)
    wuq = p["c_w_uq"][l].reshape(C_Q_RANK, C_HEADS, C_NOPE + C_ROPE)
    wuq = jnp.pad(wuq, ((0, 0), (0, 0), (0, 32))).reshape(C_Q_RANK, C_HEADS * LANES).astype(bf)
    wukv = p["c_w_ukv"][l].reshape(C_KV_RANK, C_HEADS, C_NOPE + C_VD)
    wuk = jnp.pad(wukv[:, :, :C_NOPE], ((0, 0), (0, 0), (0, 64))).reshape(C_KV_RANK, C_HEADS * LANES).astype(bf)
    wuv = wukv[:, :, C_NOPE:].reshape(C_KV_RANK, C_HEADS * C_VD).astype(bf)
    cscale = (C_NOPE + C_ROPE) ** -0.5 * LOG2E
    pad32 = lambda g: jnp.tile(jnp.pad(g, (0, 32)), C_HEADS)[None, :].astype(f32)
    return {
        "g1": p["norm1_g"][l][None, :], "w_in": w_in, "grow": grow,
        "qag": p["c_qa_norm_g"][l][None, :], "kvag": p["c_kva_norm_g"][l][None, :],
        "wuq": wuq, "wuk": wuk, "wuv": wuv,
        "gqc": pad32(p["c_qnorm_g"][l]) * cscale, "gkc": pad32(p["c_knorm_g"][l]),
        "a_lambda": p["a_lambda"][l], "subln": p["a_subln_g"][l][None, :],
        "d_sink": p["d_sink"][l],
        "w_gate": p["w_gate"][l].astype(bf),
        "w_br_a": p["w_br_a"][l].astype(bf), "w_br_b": p["w_br_b"][l].astype(bf),
        "w_br_c": p["w_br_c"][l].astype(bf), "w_br_d": p["w_br_d"][l].astype(bf),
        "w_o": p["w_o"][l].astype(bf), "g2": p["norm2_g"][l][None, :],
        "w_ffn_gate": p["w_ffn_gate"][l].astype(bf), "w_ffn_up": p["w_ffn_up"][l].astype(bf),
        "w_ffn_down": p["w_ffn_down"][l].astype(bf),
    }


def _layer(xs, l, lw, tables, out_rows):
    (qa, ka, va, qb0, kb0, vb0, qb1, kb1, vb1, qb2, kb2, vb2, qc, kc, vc, qd, kd, vd) = _proj_call(xs, lw, tables)
    oa = _attn_a_call(qa, ka, va, lw["a_lambda"], lw["subln"], lambda_init(l))
    obs, lses = [], []
    for g, (qg, kg, vg) in enumerate(((qb0, kb0, vb0), (qb1, kb1, vb1), (qb2, kb2, vb2))):
        window, dil = B_PATTERNS[g]
        res_shape = qg.shape
        qg, kg, vg = (a.reshape(-1, B_W) for a in (qg, kg, vg))
        o, lse = _band_call(qg, kg, vg, heads=B_HEAD_SPECS, bw=B_W, bq=128, hw=window // (2 * dil),
                            seg_len=SEQ // dil, with_lse=True, out_dtype=jnp.float32, name=f"band_b{g}")
        obs.append(o.reshape(res_shape))
        lses.append(lse.reshape(res_shape))
    oc = _attn_c_call(qc, kc, vc)
    od = _band_call(qd, kd, vd, heads=D_HEAD_SPECS, bw=LANES, bq=128, hw=D_WIN, seg_len=SEQ,
                    sink=lw["d_sink"], with_lse=False, out_dtype=jnp.bfloat16, name="band_d")[0]
    x = _merge_call(xs, lw, oa, obs, lses, oc, od)
    return _ffn_call(x, lw, out_rows)


def kernel(x_prompt, x_sample, norm1_g, w_in, w_gate, a_qnorm_g, a_knorm_g, a_lambda, a_subln_g, b_qnorm_g, b_knorm_g, c_qa_norm_g, c_kva_norm_g, c_w_uq, c_w_ukv, c_qnorm_g, c_knorm_g, d_qnorm_g, d_knorm_g, d_sink, w_br_a, w_br_b, w_br_c, w_br_d, w_o, norm2_g, w_ffn_gate, w_ffn_up, w_ffn_down):
    p = dict(norm1_g=norm1_g, w_in=w_in, w_gate=w_gate, a_qnorm_g=a_qnorm_g, a_knorm_g=a_knorm_g,
             a_lambda=a_lambda, a_subln_g=a_subln_g, b_qnorm_g=b_qnorm_g, b_knorm_g=b_knorm_g,
             c_qa_norm_g=c_qa_norm_g, c_kva_norm_g=c_kva_norm_g, c_w_uq=c_w_uq, c_w_ukv=c_w_ukv,
             c_qnorm_g=c_qnorm_g, c_knorm_g=c_knorm_g, d_qnorm_g=d_qnorm_g, d_knorm_g=d_knorm_g,
             d_sink=d_sink, w_br_a=w_br_a, w_br_b=w_br_b, w_br_c=w_br_c, w_br_d=w_br_d, w_o=w_o,
             norm2_g=norm2_g, w_ffn_gate=w_ffn_gate, w_ffn_up=w_ffn_up, w_ffn_down=w_ffn_down)
    xs = [x_prompt.reshape(-1, D_MODEL), x_sample.reshape(-1, D_MODEL)]
    rows = [x.shape[0] for x in xs]
    tables = _tables()
    for l in range(DEPTH):
        last = l == DEPTH - 1
        xs = _layer(xs, l, _layer_weights(l, p), tables, rows if last else [sum(rows)])
    return (xs[0].reshape(x_prompt.shape), xs[1].reshape(x_sample.shape))
```

```python
import functools
import math

import jax
import jax.numpy as jnp
import numpy as np
from jax import lax
from jax.experimental import pallas as pl
from jax.experimental.pallas import tpu as pltpu

D_MODEL = 1024
SEQ = 4096
DEPTH = 2
ROPE_THETA = 10000.0
EPS = 1e-6
NEG_INF = -1e30
N_BRANCH = 4
LOG2E = 1.4426950408889634
LN2 = 0.6931471805599453

A_HEADS = 4
A_HD = 64
B_PATTERNS = ((128, 1), (512, 4), (2048, 16))
B_HEADS = 4
B_HD = 64
B_W = B_HEADS * B_HD
C_HEADS = 8
C_Q_RANK = 256
C_KV_RANK = 128
C_NOPE = 64
C_ROPE = 32
C_VD = 64
D_QHEADS = 8
D_KVHEADS = 2
D_HD = 64
D_WIN = 128
D_FF = -(-8 * D_MODEL // (3 * 256)) * 256

IN_SIZES = (512, 512, 512) + (B_W,) * 9 + (C_Q_RANK, C_KV_RANK, C_ROPE, 512, 128, 128)
SPLIT_IDX = tuple(int(i) for i in np.cumsum(IN_SIZES)[:-1])

LANES = 128
MXU_N = 256

OFF_AQ, OFF_AK, OFF_AV = 0, 512, 1024
OFF_B = 1536
OFF_CQ = 3840
OFF_CKV = 4096
OFF_KR = 4224
OFF_DQ = 4352
OFF_DK = 4864
OFF_DV = 5120
N_PROJ = 5376

VMEM_LIMIT = 56 * 1024 * 1024

TM_PROJ = 512
TQ_A = 256
TQ_C = 256
HEADS_PER_STEP_A = 4
HEADS_PER_STEP_C = 4
MAX_SAFE_BOUND = 50.0
TM_MERGE = 256
TM_FFN = 256
FFN_CHUNKS = ((0, 1024), (1024, 1024), (2048, 768))


def lambda_init(layer):
    return 0.8 - 0.6 * math.exp(-0.3 * layer)


def _const_spec(shape):
    nd = len(shape)
    return pl.BlockSpec(shape, lambda *_: (0,) * nd, pipeline_mode=pl.Buffered(1))


def _params(n_grid):
    return pltpu.CompilerParams(dimension_semantics=("arbitrary",) * n_grid,
                                vmem_limit_bytes=VMEM_LIMIT)


def _dot(a, b):
    return jnp.dot(a, b, preferred_element_type=jnp.float32)


def _dot_nt(a, b):
    return lax.dot_general(a, b, (((1,), (1,)), ((), ())), preferred_element_type=jnp.float32)


def _rms(x, g):
    ms = jnp.mean(x * x, axis=-1, keepdims=True)
    return x * lax.rsqrt(ms + EPS) * g


def _row_specs(row_counts, tm, width):
    if len(row_counts) == 1:
        return [pl.BlockSpec((tm, width), lambda i: (i, 0))]
    nb = row_counts[0] // tm
    return [pl.BlockSpec((tm, width), lambda i: (jnp.minimum(i, nb - 1), 0)),
            pl.BlockSpec((tm, width), lambda i: (jnp.maximum(i - nb, 0), 0))]


def _row_tile(refs, first_blocks):
    if len(refs) == 1:
        return refs[0][...]
    return jnp.where(pl.program_id(0) < first_blocks, refs[0][...], refs[1][...])


def _group_sumsq(p, bd):
    outs = []
    for c in range(p.shape[1] // MXU_N):
        pc = p[:, c * MXU_N:(c + 1) * MXU_N]
        outs.append(_dot((pc * pc).astype(jnp.bfloat16), bd))
    return outs[0] if len(outs) == 1 else jnp.concatenate(outs, axis=-1)


def _rope_chunks(y, cos, sin, first_half, shift):
    outs = []
    for c in range(y.shape[1] // LANES):
        yc = y[:, c * LANES:(c + 1) * LANES]
        sw = jnp.where(first_half, pltpu.roll(yc, LANES - shift, 1), pltpu.roll(yc, shift, 1))
        outs.append(yc * cos + sw * sin)
    return outs[0] if len(outs) == 1 else jnp.concatenate(outs, axis=-1)


def _proj_kernel(*refs, n_x, first_blocks):
    x_refs, refs = refs[:n_x], refs[n_x:]
    (g1_ref, w_ref, grow_ref, bd_ref, cs_ref, qag_ref, kvag_ref,
     wuq_ref, wuk_ref, wuv_ref, gqc_ref, gkc_ref,
     qa, ka, va, qb0, kb0, vb0, qb1, kb1, vb1, qb2, kb2, vb2,
     qc, kc, vc, qd, kd, vd) = refs[:30]
    res_scr = list(refs[30:])
    tm = x_refs[0].shape[0]
    h = _rms(_row_tile(x_refs, first_blocks), g1_ref[...]).astype(jnp.bfloat16)
    bd64 = bd_ref[0]
    bd128 = bd_ref[1]
    cos64, sin64, cosc, sinc = cs_ref[0], cs_ref[1], cs_ref[2], cs_ref[3]
    lane = lax.broadcasted_iota(jnp.int32, (tm, LANES), 1)
    first64 = (lane & 63) < 32
    firstc = lane < (C_NOPE + C_ROPE // 2)

    def proj(off, width):
        return _dot(h, w_ref[:, off:off + width])

    def store(out_ref, val):
        if len(out_ref.shape) == 2:
            out_ref[...] = val.astype(out_ref.dtype)
            return
        dil, rows = out_ref.shape[1], out_ref.shape[2]
        scr = res_scr.pop()
        n_c = scr.shape[0]
        for c in range(n_c):
            scr[c] = val[:, c * LANES:(c + 1) * LANES]
        for r in range(dil):
            out_ref[0, r] = jnp.concatenate(
                [scr[c, pl.ds(r, rows, stride=dil), :] for c in range(n_c)], axis=-1).astype(out_ref.dtype)

    def norm_rope64(off, width, out_ref):
        p = proj(off, width)
        ss = _group_sumsq(p, bd64)
        y = p * lax.rsqrt(ss * (1.0 / 64) + EPS) * grow_ref[:, off:off + width]
        store(out_ref, _rope_chunks(y, cos64, sin64, first64, 32))

    def plain(off, width, out_ref):
        store(out_ref, proj(off, width))

    norm_rope64(OFF_AQ, 512, qa)
    norm_rope64(OFF_AK, 512, ka)
    plain(OFF_AV, 512, va)
    for g, (qo, ko, vo) in enumerate(((qb0, kb0, vb0), (qb1, kb1, vb1), (qb2, kb2, vb2))):
        base = OFF_B + g * 768
        norm_rope64(base, 256, qo)
        norm_rope64(base + 256, 256, ko)
        plain(base + 512, 256, vo)
    norm_rope64(OFF_DQ, 512, qd)
    norm_rope64(OFF_DK, 256, kd)
    plain(OFF_DV, 256, vd)

    cqn = _rms(proj(OFF_CQ, C_Q_RANK), qag_ref[...]).astype(jnp.bfloat16)
    qfull = _dot(cqn, wuq_ref[...])
    qn = qfull * lax.rsqrt(_group_sumsq(qfull, bd128) * (1.0 / 96) + EPS) * gqc_ref[...]
    qc[...] = _rope_chunks(qn, cosc, sinc, firstc, C_ROPE // 2).astype(qc.dtype)
    ckvn = _rms(proj(OFF_CKV, C_KV_RANK), kvag_ref[...]).astype(jnp.bfloat16)
    kr = proj(OFF_KR, LANES)
    kfull = _dot(ckvn, wuk_ref[...]) + jnp.concatenate([kr] * C_HEADS, axis=-1)
    kn = kfull * lax.rsqrt(_group_sumsq(kfull, bd128) * (1.0 / 96) + EPS) * gkc_ref[...]
    kc[...] = _rope_chunks(kn, cosc, sinc, firstc, C_ROPE // 2).astype(kc.dtype)
    vc[...] = _dot(ckvn, wuv_ref[...]).astype(vc.dtype)


def _proj_call(xs, lw, tables):
    rows = [x.shape[0] for x in xs]
    t = sum(rows)
    tm = TM_PROJ
    n_pos = SEQ // tm
    widths = (512, 512, 512) + (256,) * 9 + (1024, 1024, 512, 512, 256, 256)
    out_shape = [jax.ShapeDtypeStruct((t, w), jnp.bfloat16) for w in widths]
    out_specs = [pl.BlockSpec((tm, w), lambda i: (i, 0)) for w in widths]
    for g, (_, dil) in enumerate(B_PATTERNS):
        if dil > 1:
            for j in range(3 + 3 * g, 6 + 3 * g):
                out_shape[j] = jax.ShapeDtypeStruct((t // SEQ, dil, SEQ // dil, B_W), jnp.bfloat16)
                out_specs[j] = pl.BlockSpec((1, dil, tm // dil, B_W), lambda i: (i // n_pos, 0, i % n_pos, 0))
    n_res = 3 * sum(1 for _, dil in B_PATTERNS if dil > 1)
    in_specs = _row_specs(rows, tm, D_MODEL) + [
        _const_spec((1, D_MODEL)),
        _const_spec((D_MODEL, N_PROJ)),
        _const_spec((1, N_PROJ)),
        _const_spec((2, MXU_N, MXU_N)),
        pl.BlockSpec((4, tm, LANES), lambda i: (0, i % n_pos, 0)),
        _const_spec((1, C_Q_RANK)),
        _const_spec((1, C_KV_RANK)),
        _const_spec((C_Q_RANK, 1024)),
        _const_spec((C_KV_RANK, 1024)),
        _const_spec((C_KV_RANK, 512)),
        _const_spec((1, 1024)),
        _const_spec((1, 1024)),
    ]
    return pl.pallas_call(
        functools.partial(_proj_kernel, n_x=len(xs), first_blocks=rows[0] // tm),
        grid=(t // tm,), in_specs=in_specs, out_specs=out_specs, out_shape=out_shape,
        scratch_shapes=[pltpu.VMEM((B_W // LANES, tm, LANES), jnp.float32)] * n_res,
        compiler_params=_params(1), name="proj",
    )(*xs, lw["g1"], lw["w_in"], lw["grow"], tables["bd"], tables["cs"], lw["qag"], lw["kvag"],
      lw["wuq"], lw["wuk"], lw["wuv"], lw["gqc"], lw["gkc"])


def _lane_fold(x, op):
    acc = x[:, :LANES]
    for t in range(1, x.shape[1] // LANES):
        acc = op(acc, x[:, t * LANES:(t + 1) * LANES])
    return acc


def _online_step(state, s, v):
    r = jnp.max(_lane_fold(s, jnp.maximum), axis=-1, keepdims=True)
    if state is None:
        e = jnp.exp2(s - r)
        return r, _lane_fold(e, jnp.add), _dot(e.astype(jnp.bfloat16), v)
    m, l, acc = state
    m_new = jnp.maximum(m, r)
    alpha = jnp.exp2(m - m_new)
    e = jnp.exp2(s - m_new)
    return m_new, alpha * l + _lane_fold(e, jnp.add), alpha * acc + _dot(e.astype(jnp.bfloat16), v)


def _online_finish(state):
    _, l, acc = state
    return acc * (1.0 / jnp.sum(l, axis=-1, keepdims=True))


def _attn_a_kernel(bound_ref, lam_ref, q_ref, k_ref, v_ref, g_ref, o_ref, *, lam_init, bounded):
    lp = lam_ref[...]
    lam = (jnp.exp(jnp.sum(lp[0:1] * lp[1:2], axis=-1, keepdims=True))
           - jnp.exp(jnp.sum(lp[2:3] * lp[3:4], axis=-1, keepdims=True)) + lam_init)
    lane = lax.broadcasted_iota(jnp.int32, (q_ref.shape[0], LANES), 1)
    n_heads = q_ref.shape[1] // LANES

    def scores(h):
        hs = slice(h * LANES, (h + 1) * LANES)
        q = q_ref[:, hs]
        k = k_ref[:, hs]
        return [_dot_nt(jnp.where((lane >= c * A_HD) & (lane < (c + 1) * A_HD), q, jnp.zeros_like(q)), k)
                for c in range(2)]

    def finish(h, o):
        o = _rms(o, g_ref[...]) * (1.0 - lam_init)
        o_ref[:, h * LANES:(h + 1) * LANES] = o.astype(o_ref.dtype)

    if bounded:
        bound = bound_ref[0]
        for h in range(n_heads):
            v = v_ref[:, h * LANES:(h + 1) * LANES]
            parts = []
            for s in scores(h):
                e = jnp.exp2(s - bound)
                l = jnp.sum(_lane_fold(e, jnp.add), axis=-1, keepdims=True)
                parts.append((_dot(e.astype(jnp.bfloat16), v), 1.0 / l))
            finish(h, parts[0][0] * parts[0][1] - parts[1][0] * (lam * parts[1][1]))
        return

    s_next = scores(0)
    for h in range(n_heads):
        s_cur = s_next
        if h + 1 < n_heads:
            s_next = scores(h + 1)
        es, inv = [], []
        for s in s_cur:
            e = jnp.exp2(s - jnp.max(s, axis=-1, keepdims=True))
            es.append(e)
            inv.append(1.0 / jnp.sum(e, axis=-1, keepdims=True))
        w = es[0] * inv[0] - es[1] * (lam * inv[1])
        finish(h, _dot(w.astype(jnp.bfloat16), v_ref[:, h * LANES:(h + 1) * LANES]))


def _attn_a_call(q, k, v, a_lambda, subln_row, lam_init, bound, bounded):
    t = q.shape[0]
    nseq = t // SEQ
    tq = TQ_A
    nq = SEQ // tq
    w = HEADS_PER_STEP_A * LANES
    return pl.pallas_call(
        functools.partial(_attn_a_kernel, lam_init=lam_init, bounded=bounded),
        grid=(nseq, A_HEADS // HEADS_PER_STEP_A, nq),
        in_specs=[
            pl.BlockSpec(memory_space=pltpu.SMEM),
            _const_spec((4, A_HD)),
            pl.BlockSpec((tq, w), lambda s, h, i: (s * nq + i, h)),
            pl.BlockSpec((SEQ, w), lambda s, h, i: (s, h)),
            pl.BlockSpec((SEQ, w), lambda s, h, i: (s, h)),
            _const_spec((1, LANES)),
        ],
        out_specs=pl.BlockSpec((tq, w), lambda s, h, i: (s * nq + i, h)),
        out_shape=jax.ShapeDtypeStruct((t, A_HEADS * LANES), jnp.bfloat16),
        compiler_params=_params(3), name="attn_a",
    )(bound, a_lambda, q, k, v, subln_row)


def _attn_c_kernel(bound_ref, q_ref, k_ref, v_ref, o_ref, *, bounded):
    lane = lax.broadcasted_iota(jnp.int32, (o_ref.shape[0], LANES), 1)
    n_heads = q_ref.shape[1] // LANES

    def scores(h):
        hs = slice(h * LANES, (h + 1) * LANES)
        return _dot_nt(q_ref[:, hs], k_ref[:, hs])

    outs = []
    if bounded:
        bound = bound_ref[0]
        for h in range(n_heads):
            e = jnp.exp2(scores(h) - bound)
            l = jnp.sum(_lane_fold(e, jnp.add), axis=-1, keepdims=True)
            outs.append(_dot(e.astype(jnp.bfloat16), v_ref[:, (h // 2) * LANES:(h // 2 + 1) * LANES]) * (1.0 / l))
    else:
        s_next = scores(0)
        for h in range(n_heads):
            s = s_next
            if h + 1 < n_heads:
                s_next = scores(h + 1)
            v = v_ref[:, (h // 2) * LANES:(h // 2 + 1) * LANES]
            half = s.shape[0] // 2
            parts = [_online_step(None, s[r:r + half], v) for r in (0, half)]
            outs.append(_online_finish(tuple(jnp.concatenate([a, b], axis=0) for a, b in zip(*parts))))
    for j in range(n_heads // 2):
        o_ref[:, j * LANES:(j + 1) * LANES] = jnp.where(
            lane < C_VD, outs[2 * j], outs[2 * j + 1]).astype(o_ref.dtype)


def _attn_c_call(q, k, v, bound, bounded):
    t = q.shape[0]
    nseq = t // SEQ
    tq = TQ_C
    nq = SEQ // tq
    hps = HEADS_PER_STEP_C
    return pl.pallas_call(
        functools.partial(_attn_c_kernel, bounded=bounded),
        grid=(nseq, C_HEADS // hps, nq),
        in_specs=[
            pl.BlockSpec(memory_space=pltpu.SMEM),
            pl.BlockSpec((tq, hps * LANES), lambda s, j, i: (s * nq + i, j)),
            pl.BlockSpec((SEQ, hps * LANES), lambda s, j, i: (s, j)),
            pl.BlockSpec((SEQ, hps * C_VD), lambda s, j, i: (s, j)),
        ],
        out_specs=pl.BlockSpec((tq, hps * C_VD), lambda s, j, i: (s * nq + i, j)),
        out_shape=jax.ShapeDtypeStruct((t, C_HEADS * C_VD), jnp.bfloat16),
        compiler_params=_params(3), name="attn_c",
    )(bound, q, k, v)


def _full_attention(call, bound):
    return lax.cond(bound[0] <= MAX_SAFE_BOUND, lambda: call(True), lambda: call(False))


def _band_kernel(*refs, heads, bw, bq, hw, seg_len, with_sink, with_lse):
    if with_sink:
        sink_ref, q_ref, k_ref, v_ref = refs[:4]
        outs = refs[4:]
    else:
        q_ref, k_ref, v_ref = refs[:3]
        outs = refs[3:]
    o_ref = outs[0]
    lse_ref = outs[1] if with_lse else None
    ch = q_ref.shape[0]
    win = bq + 2 * hw
    shift = int(math.log2(seg_len))
    lane = lax.broadcasted_iota(jnp.int32, (bq, bw), 1)
    row = lax.broadcasted_iota(jnp.int32, (bq, win), 0)
    col = lax.broadcasted_iota(jnp.int32, (bq, win), 1)
    q_offs = sorted({hd[0] for hd in heads})
    k_offs = sorted({hd[2] for hd in heads})

    def body(i, carry):
        q0 = pl.multiple_of(i * bq, bq)
        ws = pl.multiple_of(jnp.clip(q0 - hw, 0, ch - win), hw)
        qi = q0 + row
        kj = ws + col
        d = qi - kj
        valid = (d <= hw) & (d >= -hw) & ((qi >> shift) == (kj >> shift))
        bias = jnp.where(valid, 0.0, NEG_INF)
        qb = q_ref[pl.ds(q0, bq), :]
        kw = k_ref[pl.ds(ws, win), :]
        vw = v_ref[pl.ds(ws, win), :]
        acc = {qo: jnp.zeros((bq, bw), jnp.float32) for qo in q_offs}
        lacc = {qo: jnp.zeros((bq, bw), jnp.float32) for qo in q_offs}
        for ko in k_offs:
            group = [hd for hd in heads if hd[2] == ko]
            hms = [(lane >= mo) & (lane < mo + 64) for _, mo, _, _ in group]
            qs = jnp.concatenate(
                [jnp.where(hm, qb[:, qo:qo + bw], jnp.zeros((bq, bw), qb.dtype))
                 for hm, (qo, _, _, _) in zip(hms, group)], axis=0)
            s_all = _dot_nt(qs, kw[:, ko:ko + bw])
            es, ms, ls = [], [], []
            for n, (_, _, _, hid) in enumerate(group):
                s = s_all[n * bq:(n + 1) * bq] + bias
                m = jnp.max(s, axis=-1, keepdims=True)
                if with_sink:
                    sk = sink_ref[hid] * LOG2E
                    m = jnp.maximum(m, sk)
                e = jnp.exp2(s - m)
                l = jnp.sum(e, axis=-1, keepdims=True)
                if with_sink:
                    l = l + jnp.exp2(sk - m)
                es.append(e.astype(jnp.bfloat16))
                ms.append(m)
                ls.append(l)
            o_all = _dot(jnp.concatenate(es, axis=0), vw[:, ko:ko + bw])
            for n, (hm, (qo, _, _, _)) in enumerate(zip(hms, group)):
                o = o_all[n * bq:(n + 1) * bq] * (1.0 / ls[n])
                acc[qo] = jnp.where(hm, o, acc[qo])
                if with_lse:
                    lacc[qo] = jnp.where(hm, (ms[n] + jnp.log2(ls[n])) * LN2, lacc[qo])
        for qo in q_offs:
            o_ref[pl.ds(q0, bq), qo:qo + bw] = acc[qo].astype(o_ref.dtype)
            if with_lse:
                lse_ref[pl.ds(q0, bq), qo:qo + bw] = lacc[qo]
        return carry

    lax.fori_loop(0, ch // bq, body, 0, unroll=2)


def _band_call(q, k, v, *, heads, bw, bq, hw, seg_len, sink=None, with_lse, out_dtype, name):
    t, wq = q.shape
    wk = k.shape[1]
    ch = SEQ
    kern = functools.partial(_band_kernel, heads=heads, bw=bw, bq=bq, hw=hw, seg_len=seg_len,
                             with_sink=sink is not None, with_lse=with_lse)
    in_specs = [
        pl.BlockSpec((ch, wq), lambda i: (i, 0)),
        pl.BlockSpec((ch, wk), lambda i: (i, 0)),
        pl.BlockSpec((ch, wk), lambda i: (i, 0)),
    ]
    args = [q, k, v]
    if sink is not None:
        in_specs = [pl.BlockSpec(memory_space=pltpu.SMEM)] + in_specs
        args = [sink] + args
    out_shape = [jax.ShapeDtypeStruct((t, wq), out_dtype)]
    out_specs = [pl.BlockSpec((ch, wq), lambda i: (i, 0))]
    if with_lse:
        out_shape.append(jax.ShapeDtypeStruct((t, wq), jnp.float32))
        out_specs.append(pl.BlockSpec((ch, wq), lambda i: (i, 0)))
    return pl.pallas_call(
        kern, grid=(t // ch,), in_specs=in_specs, out_specs=out_specs, out_shape=out_shape,
        compiler_params=_params(1), name=name,
    )(*args)


B_HEAD_SPECS = tuple((0, h * B_HD, 0, h) for h in range(B_HEADS))
D_HEAD_SPECS = tuple(((h // 2) * LANES, (h % 2) * D_HD, (h // 4) * LANES, h) for h in range(D_QHEADS))


def _sigmoid(z):
    return 1.0 / (1.0 + jnp.exp(-z))


def _merge_kernel(*refs, n_x, first_blocks):
    x_refs, refs = refs[:n_x], refs[n_x:]
    (g1_ref, wg_ref, oa_ref, ob0, ob1, ob2, ls0, ls1, ls2, oc_ref, od_ref,
     wa_ref, wb_ref, wc_ref, wd_ref, wo_ref, out_ref) = refs[:17]
    res_scr = list(refs[17:])

    def load(ref):
        if len(ref.shape) == 2:
            return ref[...]
        dil, rows = ref.shape[1], ref.shape[2]
        scr = res_scr.pop()
        n_c = scr.shape[0]
        for r in range(dil):
            for c in range(n_c):
                scr[c, pl.ds(r, rows, stride=dil), :] = ref[0, r, :, c * LANES:(c + 1) * LANES]
        return jnp.concatenate([scr[c] for c in range(n_c)], axis=-1)

    x = _row_tile(x_refs, first_blocks)
    h = _rms(x, g1_ref[...]).astype(jnp.bfloat16)
    l0, l1, l2 = load(ls0), load(ls1), load(ls2)
    lm = jnp.maximum(jnp.maximum(l0, l1), l2)
    e0, e1, e2 = jnp.exp(l0 - lm), jnp.exp(l1 - lm), jnp.exp(l2 - lm)
    den = e0 + e1 + e2
    ob = ((e0 / den) * load(ob0) + (e1 / den) * load(ob1) + (e2 / den) * load(ob2)).astype(jnp.bfloat16)
    branches = ((oa_ref[...], wa_ref), (ob, wb_ref), (oc_ref[...], wc_ref), (od_ref[...], wd_ref))
    merged = None
    for i, (o, w_ref) in enumerate(branches):
        gate = _sigmoid(_dot(h, wg_ref[:, i * D_MODEL:(i + 1) * D_MODEL]))
        term = gate * _dot(o, w_ref[...])
        merged = term if merged is None else merged + term
    out_ref[...] = x + _dot(merged.astype(jnp.bfloat16), wo_ref[...])


def _merge_call(xs, lw, oa, obs, lses, oc, od):
    rows = [x.shape[0] for x in xs]
    t = sum(rows)
    tm = TM_MERGE

    def tile(w):
        return pl.BlockSpec((tm, w), lambda i: (i, 0))

    n_pos = SEQ // tm

    def band_tile(dil):
        if dil == 1:
            return tile(B_W)
        return pl.BlockSpec((1, dil, tm // dil, B_W), lambda i: (i // n_pos, 0, i % n_pos, 0))

    b_specs = [band_tile(dil) for _, dil in B_PATTERNS]
    in_specs = _row_specs(rows, tm, D_MODEL) + [
                _const_spec((1, D_MODEL)), _const_spec((D_MODEL, N_BRANCH * D_MODEL)),
                tile(512)] + b_specs + b_specs + [tile(512), tile(512),
                _const_spec((512, D_MODEL)), _const_spec((256, D_MODEL)), _const_spec((512, D_MODEL)),
                _const_spec((512, D_MODEL)), _const_spec((D_MODEL, D_MODEL))]
    n_res = 2 * sum(1 for _, dil in B_PATTERNS if dil > 1)
    return pl.pallas_call(
        functools.partial(_merge_kernel, n_x=len(xs), first_blocks=rows[0] // tm),
        grid=(t // tm,), in_specs=in_specs, out_specs=tile(D_MODEL),
        out_shape=jax.ShapeDtypeStruct((t, D_MODEL), jnp.float32),
        scratch_shapes=[pltpu.VMEM((B_W // LANES, tm, LANES), jnp.float32)] * n_res,
        compiler_params=_params(1), name="merge",
    )(*xs, lw["g1"], lw["w_gate"], oa, *obs, *lses, oc, od,
      lw["w_br_a"], lw["w_br_b"], lw["w_br_c"], lw["w_br_d"], lw["w_o"])


def _ffn_kernel(x_ref, g2_ref, wg_ref, wu_ref, wd_ref, *out_refs, first_blocks):
    x = x_ref[...]
    hf = _rms(x, g2_ref[...]).astype(jnp.bfloat16)
    acc = x
    for off, width in FFN_CHUNKS:
        a = _dot(hf, wg_ref[:, off:off + width])
        u = _dot(hf, wu_ref[:, off:off + width])
        act = (a * _sigmoid(a) * u).astype(jnp.bfloat16)
        acc = acc + _dot(act, wd_ref[off:off + width, :])
    if len(out_refs) == 1:
        out_refs[0][...] = acc
    else:
        @pl.when(pl.program_id(0) < first_blocks)
        def _():
            out_refs[0][...] = acc

        @pl.when(pl.program_id(0) >= first_blocks)
        def _():
            out_refs[1][...] = acc


def _ffn_call(x, lw, out_rows):
    t = x.shape[0]
    tm = TM_FFN
    tile = pl.BlockSpec((tm, D_MODEL), lambda i: (i, 0))
    return pl.pallas_call(
        functools.partial(_ffn_kernel, first_blocks=out_rows[0] // tm), grid=(t // tm,),
        in_specs=[tile, _const_spec((1, D_MODEL)), _const_spec((D_MODEL, D_FF)),
                  _const_spec((D_MODEL, D_FF)), _const_spec((D_FF, D_MODEL))],
        out_specs=_row_specs(out_rows, tm, D_MODEL),
        out_shape=[jax.ShapeDtypeStruct((r, D_MODEL), jnp.float32) for r in out_rows],
        compiler_params=_params(1), name="ffn",
    )(x, lw["g2"], lw["w_ffn_gate"], lw["w_ffn_up"], lw["w_ffn_down"])


def _tables():
    pos = jnp.arange(SEQ, dtype=jnp.float32)[:, None]
    lane = jnp.arange(LANES)
    inv64 = jnp.power(ROPE_THETA, -jnp.arange(32, dtype=jnp.float32) / 32)
    ang64 = pos * inv64[lane % 32][None, :]
    sign64 = jnp.where((lane % 64) < 32, -1.0, 1.0)[None, :]
    cos64 = jnp.cos(ang64)
    sin64 = jnp.sin(ang64) * sign64
    invc = jnp.power(ROPE_THETA, -jnp.arange(16, dtype=jnp.float32) / 16)
    angc = pos * invc[lane % 16][None, :]
    is_rope = ((lane >= C_NOPE) & (lane < C_NOPE + C_ROPE))[None, :]
    signc = jnp.where(lane < C_NOPE + C_ROPE // 2, -1.0, 1.0)[None, :]
    cosc = jnp.where(is_rope, jnp.cos(angc), 1.0)
    sinc = jnp.where(is_rope, jnp.sin(angc) * signc, 0.0)
    cs = jnp.stack([cos64, sin64, cosc, sinc]).astype(jnp.float32)
    idx = np.arange(MXU_N)
    bd = np.stack([(idx[:, None] // 64) == (idx[None, :] // 64),
                   (idx[:, None] // 128) == (idx[None, :] // 128)]).astype(np.float32)
    return {"cs": cs, "bd": jnp.asarray(bd, dtype=jnp.bfloat16)}


def _layer_weights(l, p):
    bf = jnp.bfloat16
    f32 = jnp.float32
    cols = jnp.split(p["w_in"][l], SPLIT_IDX, axis=-1)
    zeros = lambda n: jnp.zeros((D_MODEL, n), f32)
    dk, dv = cols[16], cols[17]
    w_in = jnp.concatenate(
        list(cols[0:12]) + [cols[12], cols[13], zeros(64), cols[14], zeros(32), cols[15],
                            dk[:, :64], dk[:, :64], dk[:, 64:], dk[:, 64:],
                            dv[:, :64], dv[:, :64], dv[:, 64:], dv[:, 64:]], axis=-1).astype(bf)
    ones = lambda n: jnp.ones((n,), f32)
    qs = A_HD ** -0.5 * LOG2E
    grow = jnp.concatenate(
        [jnp.tile(p["a_qnorm_g"][l], 8) * qs, jnp.tile(p["a_knorm_g"][l], 8), ones(512)]
        + sum([[jnp.tile(p["b_qnorm_g"][l, g], 4) * qs, jnp.tile(p["b_knorm_g"][l, g], 4), ones(256)]
               for g in range(3)], [])
        + [ones(512), jnp.tile(p["d_qnorm_g"][l], 8) * qs, jnp.tile(p["d_knorm_g"][l], 4), ones(256)]
    )[None, :].astype(f32)
    wuq = p["c_w_uq"][l].reshape(C_Q_RANK, C_HEADS, C_NOPE + C_ROPE)
    wuq = jnp.pad(wuq, ((0, 0), (0, 0), (0, 32))).reshape(C_Q_RANK, C_HEADS * LANES).astype(bf)
    wukv = p["c_w_ukv"][l].reshape(C_KV_RANK, C_HEADS, C_NOPE + C_VD)
    wuk = jnp.pad(wukv[:, :, :C_NOPE], ((0, 0), (0, 0), (0, 64))).reshape(C_KV_RANK, C_HEADS * LANES).astype(bf)
    wuv = wukv[:, :, C_NOPE:].reshape(C_KV_RANK, C_HEADS * C_VD).astype(bf)
    cscale = (C_NOPE + C_ROPE) ** -0.5 * LOG2E
    pad32 = lambda g: jnp.tile(jnp.pad(g, (0, 32)), C_HEADS)[None, :].astype(f32)
    slack = (1.0 + 2.0 ** -8) ** 2
    bound_a = (A_HD * qs * slack * jnp.max(jnp.abs(p["a_qnorm_g"][l])) * jnp.max(jnp.abs(p["a_knorm_g"][l])))
    bound_c = ((C_NOPE + C_ROPE) * cscale * slack
               * jnp.max(jnp.abs(p["c_qnorm_g"][l])) * jnp.max(jnp.abs(p["c_knorm_g"][l])))
    return {
        "bound_a": bound_a.reshape(1).astype(f32), "bound_c": bound_c.reshape(1).astype(f32),
        "g1": p["norm1_g"][l][None, :], "w_in": w_in, "grow": grow,
        "qag": p["c_qa_norm_g"][l][None, :], "kvag": p["c_kva_norm_g"][l][None, :],
        "wuq": wuq, "wuk": wuk, "wuv": wuv,
        "gqc": pad32(p["c_qnorm_g"][l]) * cscale, "gkc": pad32(p["c_knorm_g"][l]),
        "a_lambda": p["a_lambda"][l], "subln": p["a_subln_g"][l][None, :],
        "d_sink": p["d_sink"][l],
        "w_gate": p["w_gate"][l].astype(bf),
        "w_br_a": p["w_br_a"][l].astype(bf), "w_br_b": p["w_br_b"][l].astype(bf),
        "w_br_c": p["w_br_c"][l].astype(bf), "w_br_d": p["w_br_d"][l].astype(bf),
        "w_o": p["w_o"][l].astype(bf), "g2": p["norm2_g"][l][None, :],
        "w_ffn_gate": p["w_ffn_gate"][l].astype(bf), "w_ffn_up": p["w_ffn_up"][l].astype(bf),
        "w_ffn_down": p["w_ffn_down"][l].astype(bf),
    }


def _layer(xs, l, lw, tables, out_rows):
    (qa, ka, va, qb0, kb0, vb0, qb1, kb1, vb1, qb2, kb2, vb2, qc, kc, vc, qd, kd, vd) = _proj_call(xs, lw, tables)
    oa = _full_attention(functools.partial(_attn_a_call, qa, ka, va, lw["a_lambda"], lw["subln"],
                                           lambda_init(l), lw["bound_a"]), lw["bound_a"])
    obs, lses = [], []
    for g, (qg, kg, vg) in enumerate(((qb0, kb0, vb0), (qb1, kb1, vb1), (qb2, kb2, vb2))):
        window, dil = B_PATTERNS[g]
        res_shape = qg.shape
        qg, kg, vg = (a.reshape(-1, B_W) for a in (qg, kg, vg))
        o, lse = _band_call(qg, kg, vg, heads=B_HEAD_SPECS, bw=B_W, bq=128, hw=window // (2 * dil),
                            seg_len=SEQ // dil, with_lse=True, out_dtype=jnp.float32, name=f"band_b{g}")
        obs.append(o.reshape(res_shape))
        lses.append(lse.reshape(res_shape))
    oc = _full_attention(functools.partial(_attn_c_call, qc, kc, vc, lw["bound_c"]), lw["bound_c"])
    od = _band_call(qd, kd, vd, heads=D_HEAD_SPECS, bw=LANES, bq=128, hw=D_WIN, seg_len=SEQ,
                    sink=lw["d_sink"], with_lse=False, out_dtype=jnp.bfloat16, name="band_d")[0]
    x = _merge_call(xs, lw, oa, obs, lses, oc, od)
    return _ffn_call(x, lw, out_rows)


def kernel(x_prompt, x_sample, norm1_g, w_in, w_gate, a_qnorm_g, a_knorm_g, a_lambda, a_subln_g, b_qnorm_g, b_knorm_g, c_qa_norm_g, c_kva_norm_g, c_w_uq, c_w_ukv, c_qnorm_g, c_knorm_g, d_qnorm_g, d_knorm_g, d_sink, w_br_a, w_br_b, w_br_c, w_br_d, w_o, norm2_g, w_ffn_gate, w_ffn_up, w_ffn_down):
    p = dict(norm1_g=norm1_g, w_in=w_in, w_gate=w_gate, a_qnorm_g=a_qnorm_g, a_knorm_g=a_knorm_g,
             a_lambda=a_lambda, a_subln_g=a_subln_g, b_qnorm_g=b_qnorm_g, b_knorm_g=b_knorm_g,
             c_qa_norm_g=c_qa_norm_g, c_kva_norm_g=c_kva_norm_g, c_w_uq=c_w_uq, c_w_ukv=c_w_ukv,
             c_qnorm_g=c_qnorm_g, c_knorm_g=c_knorm_g, d_qnorm_g=d_qnorm_g, d_knorm_g=d_knorm_g,
             d_sink=d_sink, w_br_a=w_br_a, w_br_b=w_br_b, w_br_c=w_br_c, w_br_d=w_br_d, w_o=w_o,
             norm2_g=norm2_g, w_ffn_gate=w_ffn_gate, w_ffn_up=w_ffn_up, w_ffn_down=w_ffn_down)
    xs = [x_prompt.reshape(-1, D_MODEL), x_sample.reshape(-1, D_MODEL)]
    rows = [x.shape[0] for x in xs]
    tables = _tables()
    for l in range(DEPTH):
        last = l == DEPTH - 1
        xs = _layer(xs, l, _layer_weights(l, p), tables, rows if last else [sum(rows)])
    return (xs[0].reshape(x_prompt.shape), xs[1].reshape(x_sample.shape))
```

```python
import functools
import math

import jax
import jax.numpy as jnp
import numpy as np
from jax import lax
from jax.experimental import pallas as pl
from jax.experimental.pallas import tpu as pltpu

D_MODEL = 1024
SEQ = 4096
DEPTH = 2
ROPE_THETA = 10000.0
EPS = 1e-6
NEG_INF = -1e30
N_BRANCH = 4
LOG2E = 1.4426950408889634
LN2 = 0.6931471805599453

A_HEADS = 4
A_HD = 64
B_PATTERNS = ((128, 1), (512, 4), (2048, 16))
B_HEADS = 4
B_HD = 64
B_W = B_HEADS * B_HD
C_HEADS = 8
C_Q_RANK = 256
C_KV_RANK = 128
C_NOPE = 64
C_ROPE = 32
C_VD = 64
D_QHEADS = 8
D_KVHEADS = 2
D_HD = 64
D_WIN = 128
D_FF = -(-8 * D_MODEL // (3 * 256)) * 256

IN_SIZES = (512, 512, 512) + (B_W,) * 9 + (C_Q_RANK, C_KV_RANK, C_ROPE, 512, 128, 128)
SPLIT_IDX = tuple(int(i) for i in np.cumsum(IN_SIZES)[:-1])

LANES = 128
MXU_N = 256

OFF_AQ, OFF_AK, OFF_AV = 0, 512, 1024
OFF_B = 1536
OFF_CQ = 3840
OFF_CKV = 4096
OFF_KR = 4224
OFF_DQ = 4352
OFF_DK = 4864
OFF_DV = 5120
N_PROJ = 5376

VMEM_LIMIT = 56 * 1024 * 1024

TM_PROJ = 512
TQ_A = 256
TQ_C = 256
HEADS_PER_STEP_A = 4
HEADS_PER_STEP_C = 4
MAX_SAFE_BOUND = 50.0
TM_MERGE = 512
TM_FFN = 512
FFN_CHUNKS = ((0, 1024), (1024, 1024), (2048, 768))


def lambda_init(layer):
    return 0.8 - 0.6 * math.exp(-0.3 * layer)


def _const_spec(shape):
    nd = len(shape)
    return pl.BlockSpec(shape, lambda *_: (0,) * nd, pipeline_mode=pl.Buffered(1))


def _params(n_grid):
    return pltpu.CompilerParams(dimension_semantics=("arbitrary",) * n_grid,
                                vmem_limit_bytes=VMEM_LIMIT)


def _dot(a, b):
    return jnp.dot(a, b, preferred_element_type=jnp.float32)


def _dot_nt(a, b):
    return lax.dot_general(a, b, (((1,), (1,)), ((), ())), preferred_element_type=jnp.float32)


def _rms(x, g):
    ms = jnp.mean(x * x, axis=-1, keepdims=True)
    return x * lax.rsqrt(ms + EPS) * g


def _row_specs(row_counts, tm, width):
    if len(row_counts) == 1:
        return [pl.BlockSpec((tm, width), lambda i: (i, 0))]
    nb = row_counts[0] // tm
    return [pl.BlockSpec((tm, width), lambda i: (jnp.minimum(i, nb - 1), 0)),
            pl.BlockSpec((tm, width), lambda i: (jnp.maximum(i - nb, 0), 0))]


def _row_tile(refs, first_blocks):
    if len(refs) == 1:
        return refs[0][...]
    return jnp.where(pl.program_id(0) < first_blocks, refs[0][...], refs[1][...])


def _group_sumsq(p, bd):
    outs = []
    for c in range(p.shape[1] // MXU_N):
        pc = p[:, c * MXU_N:(c + 1) * MXU_N]
        outs.append(_dot((pc * pc).astype(jnp.bfloat16), bd))
    return outs[0] if len(outs) == 1 else jnp.concatenate(outs, axis=-1)


def _rope_chunks(y, cos, sin, first_half, shift):
    outs = []
    for c in range(y.shape[1] // LANES):
        yc = y[:, c * LANES:(c + 1) * LANES]
        sw = jnp.where(first_half, pltpu.roll(yc, LANES - shift, 1), pltpu.roll(yc, shift, 1))
        outs.append(yc * cos + sw * sin)
    return outs[0] if len(outs) == 1 else jnp.concatenate(outs, axis=-1)


def _proj_kernel(*refs, n_x, first_blocks):
    x_refs, refs = refs[:n_x], refs[n_x:]
    (g1_ref, w_ref, grow_ref, bd_ref, cs_ref, qag_ref, kvag_ref,
     wuq_ref, wuk_ref, wuv_ref, gqc_ref, gkc_ref,
     qa, ka, va, qb0, kb0, vb0, qb1, kb1, vb1, qb2, kb2, vb2,
     qc, kc, vc, qd, kd, vd) = refs[:30]
    res_scr = list(refs[30:])
    tm = x_refs[0].shape[0]
    h = _rms(_row_tile(x_refs, first_blocks), g1_ref[...]).astype(jnp.bfloat16)
    bd64 = bd_ref[0]
    bd128 = bd_ref[1]
    cos64, sin64, cosc, sinc = cs_ref[0], cs_ref[1], cs_ref[2], cs_ref[3]
    lane = lax.broadcasted_iota(jnp.int32, (tm, LANES), 1)
    first64 = (lane & 63) < 32
    firstc = lane < (C_NOPE + C_ROPE // 2)

    def proj(off, width):
        return _dot(h, w_ref[:, off:off + width])

    def store(out_ref, val):
        if len(out_ref.shape) == 2:
            out_ref[...] = val.astype(out_ref.dtype)
            return
        dil, rows = out_ref.shape[1], out_ref.shape[2]
        scr = res_scr.pop()
        n_c = scr.shape[0]
        for c in range(n_c):
            scr[c] = val[:, c * LANES:(c + 1) * LANES]
        for r in range(dil):
            out_ref[0, r] = jnp.concatenate(
                [scr[c, pl.ds(r, rows, stride=dil), :] for c in range(n_c)], axis=-1).astype(out_ref.dtype)

    def norm_rope64(off, width, out_ref):
        p = proj(off, width)
        ss = _group_sumsq(p, bd64)
        y = p * lax.rsqrt(ss * (1.0 / 64) + EPS) * grow_ref[:, off:off + width]
        store(out_ref, _rope_chunks(y, cos64, sin64, first64, 32))

    def plain(off, width, out_ref):
        store(out_ref, proj(off, width))

    norm_rope64(OFF_AQ, 512, qa)
    norm_rope64(OFF_AK, 512, ka)
    plain(OFF_AV, 512, va)
    for g, (qo, ko, vo) in enumerate(((qb0, kb0, vb0), (qb1, kb1, vb1), (qb2, kb2, vb2))):
        base = OFF_B + g * 768
        norm_rope64(base, 256, qo)
        norm_rope64(base + 256, 256, ko)
        plain(base + 512, 256, vo)
    norm_rope64(OFF_DQ, 512, qd)
    norm_rope64(OFF_DK, 256, kd)
    plain(OFF_DV, 256, vd)

    cqn = _rms(proj(OFF_CQ, C_Q_RANK), qag_ref[...]).astype(jnp.bfloat16)
    qfull = _dot(cqn, wuq_ref[...])
    qn = qfull * lax.rsqrt(_group_sumsq(qfull, bd128) * (1.0 / 96) + EPS) * gqc_ref[...]
    qc[...] = _rope_chunks(qn, cosc, sinc, firstc, C_ROPE // 2).astype(qc.dtype)
    ckvn = _rms(proj(OFF_CKV, C_KV_RANK), kvag_ref[...]).astype(jnp.bfloat16)
    kr = proj(OFF_KR, LANES)
    kfull = _dot(ckvn, wuk_ref[...]) + jnp.concatenate([kr] * C_HEADS, axis=-1)
    kn = kfull * lax.rsqrt(_group_sumsq(kfull, bd128) * (1.0 / 96) + EPS) * gkc_ref[...]
    kc[...] = _rope_chunks(kn, cosc, sinc, firstc, C_ROPE // 2).astype(kc.dtype)
    vc[...] = _dot(ckvn, wuv_ref[...]).astype(vc.dtype)


def _proj_call(xs, lw, tables):
    rows = [x.shape[0] for x in xs]
    t = sum(rows)
    tm = TM_PROJ
    n_pos = SEQ // tm
    widths = (512, 512, 512) + (256,) * 9 + (1024, 1024, 512, 512, 256, 256)
    out_shape = [jax.ShapeDtypeStruct((t, w), jnp.bfloat16) for w in widths]
    out_specs = [pl.BlockSpec((tm, w), lambda i: (i, 0)) for w in widths]
    for g, (_, dil) in enumerate(B_PATTERNS):
        if dil > 1:
            for j in range(3 + 3 * g, 6 + 3 * g):
                out_shape[j] = jax.ShapeDtypeStruct((t // SEQ, dil, SEQ // dil, B_W), jnp.bfloat16)
                out_specs[j] = pl.BlockSpec((1, dil, tm // dil, B_W), lambda i: (i // n_pos, 0, i % n_pos, 0))
    n_res = 3 * sum(1 for _, dil in B_PATTERNS if dil > 1)
    in_specs = _row_specs(rows, tm, D_MODEL) + [
        _const_spec((1, D_MODEL)),
        _const_spec((D_MODEL, N_PROJ)),
        _const_spec((1, N_PROJ)),
        _const_spec((2, MXU_N, MXU_N)),
        pl.BlockSpec((4, tm, LANES), lambda i: (0, i % n_pos, 0)),
        _const_spec((1, C_Q_RANK)),
        _const_spec((1, C_KV_RANK)),
        _const_spec((C_Q_RANK, 1024)),
        _const_spec((C_KV_RANK, 1024)),
        _const_spec((C_KV_RANK, 512)),
        _const_spec((1, 1024)),
        _const_spec((1, 1024)),
    ]
    return pl.pallas_call(
        functools.partial(_proj_kernel, n_x=len(xs), first_blocks=rows[0] // tm),
        grid=(t // tm,), in_specs=in_specs, out_specs=out_specs, out_shape=out_shape,
        scratch_shapes=[pltpu.VMEM((B_W // LANES, tm, LANES), jnp.float32)] * n_res,
        compiler_params=_params(1), name="proj",
    )(*xs, lw["g1"], lw["w_in"], lw["grow"], tables["bd"], tables["cs"], lw["qag"], lw["kvag"],
      lw["wuq"], lw["wuk"], lw["wuv"], lw["gqc"], lw["gkc"])


def _lane_fold(x, op):
    acc = x[:, :LANES]
    for t in range(1, x.shape[1] // LANES):
        acc = op(acc, x[:, t * LANES:(t + 1) * LANES])
    return acc


def _online_step(state, s, v):
    r = jnp.max(_lane_fold(s, jnp.maximum), axis=-1, keepdims=True)
    if state is None:
        e = jnp.exp2(s - r)
        return r, _lane_fold(e, jnp.add), _dot(e.astype(jnp.bfloat16), v)
    m, l, acc = state
    m_new = jnp.maximum(m, r)
    alpha = jnp.exp2(m - m_new)
    e = jnp.exp2(s - m_new)
    return m_new, alpha * l + _lane_fold(e, jnp.add), alpha * acc + _dot(e.astype(jnp.bfloat16), v)


def _online_finish(state):
    _, l, acc = state
    return acc * (1.0 / jnp.sum(l, axis=-1, keepdims=True))


def _attn_a_kernel(bound_ref, lam_ref, q_ref, k_ref, v_ref, g_ref, o_ref, *, lam_init, bounded):
    lp = lam_ref[...]
    lam = (jnp.exp(jnp.sum(lp[0:1] * lp[1:2], axis=-1, keepdims=True))
           - jnp.exp(jnp.sum(lp[2:3] * lp[3:4], axis=-1, keepdims=True)) + lam_init)
    lane = lax.broadcasted_iota(jnp.int32, (q_ref.shape[0], LANES), 1)
    n_heads = q_ref.shape[1] // LANES

    def scores(h):
        hs = slice(h * LANES, (h + 1) * LANES)
        q = q_ref[:, hs]
        k = k_ref[:, hs]
        return [_dot_nt(jnp.where((lane >= c * A_HD) & (lane < (c + 1) * A_HD), q, jnp.zeros_like(q)), k)
                for c in range(2)]

    def finish(h, o):
        o = _rms(o, g_ref[...]) * (1.0 - lam_init)
        o_ref[:, h * LANES:(h + 1) * LANES] = o.astype(o_ref.dtype)

    if bounded:
        bound = bound_ref[0]
        for h in range(n_heads):
            v = v_ref[:, h * LANES:(h + 1) * LANES]
            parts = []
            for s in scores(h):
                e = jnp.exp2(s - bound)
                l = jnp.sum(_lane_fold(e, jnp.add), axis=-1, keepdims=True)
                parts.append((_dot(e.astype(jnp.bfloat16), v), 1.0 / l))
            finish(h, parts[0][0] * parts[0][1] - parts[1][0] * (lam * parts[1][1]))
        return

    s_next = scores(0)
    for h in range(n_heads):
        s_cur = s_next
        if h + 1 < n_heads:
            s_next = scores(h + 1)
        es, inv = [], []
        for s in s_cur:
            e = jnp.exp2(s - jnp.max(s, axis=-1, keepdims=True))
            es.append(e)
            inv.append(1.0 / jnp.sum(e, axis=-1, keepdims=True))
        w = es[0] * inv[0] - es[1] * (lam * inv[1])
        finish(h, _dot(w.astype(jnp.bfloat16), v_ref[:, h * LANES:(h + 1) * LANES]))


def _attn_a_call(q, k, v, a_lambda, subln_row, lam_init, bound, bounded):
    t = q.shape[0]
    nseq = t // SEQ
    tq = TQ_A
    nq = SEQ // tq
    w = HEADS_PER_STEP_A * LANES
    return pl.pallas_call(
        functools.partial(_attn_a_kernel, lam_init=lam_init, bounded=bounded),
        grid=(nseq, A_HEADS // HEADS_PER_STEP_A, nq),
        in_specs=[
            pl.BlockSpec(memory_space=pltpu.SMEM),
            _const_spec((4, A_HD)),
            pl.BlockSpec((tq, w), lambda s, h, i: (s * nq + i, h)),
            pl.BlockSpec((SEQ, w), lambda s, h, i: (s, h)),
            pl.BlockSpec((SEQ, w), lambda s, h, i: (s, h)),
            _const_spec((1, LANES)),
        ],
        out_specs=pl.BlockSpec((tq, w), lambda s, h, i: (s * nq + i, h)),
        out_shape=jax.ShapeDtypeStruct((t, A_HEADS * LANES), jnp.bfloat16),
        compiler_params=_params(3), name="attn_a",
    )(bound, a_lambda, q, k, v, subln_row)


def _attn_c_kernel(bound_ref, q_ref, k_ref, v_ref, o_ref, *, bounded):
    lane = lax.broadcasted_iota(jnp.int32, (o_ref.shape[0], LANES), 1)
    n_heads = q_ref.shape[1] // LANES

    def scores(h):
        hs = slice(h * LANES, (h + 1) * LANES)
        return _dot_nt(q_ref[:, hs], k_ref[:, hs])

    outs = []
    if bounded:
        bound = bound_ref[0]
        for h in range(n_heads):
            e = jnp.exp2(scores(h) - bound)
            l = jnp.sum(_lane_fold(e, jnp.add), axis=-1, keepdims=True)
            outs.append(_dot(e.astype(jnp.bfloat16), v_ref[:, (h // 2) * LANES:(h // 2 + 1) * LANES]) * (1.0 / l))
    else:
        s_next = scores(0)
        for h in range(n_heads):
            s = s_next
            if h + 1 < n_heads:
                s_next = scores(h + 1)
            v = v_ref[:, (h // 2) * LANES:(h // 2 + 1) * LANES]
            half = s.shape[0] // 2
            parts = [_online_step(None, s[r:r + half], v) for r in (0, half)]
            outs.append(_online_finish(tuple(jnp.concatenate([a, b], axis=0) for a, b in zip(*parts))))
    for j in range(n_heads // 2):
        o_ref[:, j * LANES:(j + 1) * LANES] = jnp.where(
            lane < C_VD, outs[2 * j], outs[2 * j + 1]).astype(o_ref.dtype)


def _attn_c_call(q, k, v, bound, bounded):
    t = q.shape[0]
    nseq = t // SEQ
    tq = TQ_C
    nq = SEQ // tq
    hps = HEADS_PER_STEP_C
    return pl.pallas_call(
        functools.partial(_attn_c_kernel, bounded=bounded),
        grid=(nseq, C_HEADS // hps, nq),
        in_specs=[
            pl.BlockSpec(memory_space=pltpu.SMEM),
            pl.BlockSpec((tq, hps * LANES), lambda s, j, i: (s * nq + i, j)),
            pl.BlockSpec((SEQ, hps * LANES), lambda s, j, i: (s, j)),
            pl.BlockSpec((SEQ, hps * C_VD), lambda s, j, i: (s, j)),
        ],
        out_specs=pl.BlockSpec((tq, hps * C_VD), lambda s, j, i: (s * nq + i, j)),
        out_shape=jax.ShapeDtypeStruct((t, C_HEADS * C_VD), jnp.bfloat16),
        compiler_params=_params(3), name="attn_c",
    )(bound, q, k, v)


def _full_attention(call, bound):
    return lax.cond(bound[0] <= MAX_SAFE_BOUND, lambda: call(True), lambda: call(False))


def _band_kernel(bound_ref, *refs, heads, bw, bq, hw, seg_len, with_sink, with_lse, bounded):
    if with_sink:
        sink_ref, q_ref, k_ref, v_ref = refs[:4]
        outs = refs[4:]
    else:
        q_ref, k_ref, v_ref = refs[:3]
        outs = refs[3:]
    o_ref = outs[0]
    lse_ref = outs[1] if with_lse else None
    ch = q_ref.shape[0]
    win = bq + 2 * hw
    shift = int(math.log2(seg_len))
    lane = lax.broadcasted_iota(jnp.int32, (bq, bw), 1)
    row = lax.broadcasted_iota(jnp.int32, (bq, win), 0)
    col = lax.broadcasted_iota(jnp.int32, (bq, win), 1)
    q_offs = sorted({hd[0] for hd in heads})
    k_offs = sorted({hd[2] for hd in heads})

    def body(i, carry):
        q0 = pl.multiple_of(i * bq, bq)
        ws = pl.multiple_of(jnp.clip(q0 - hw, 0, ch - win), hw)
        qi = q0 + row
        kj = ws + col
        d = qi - kj
        valid = (d <= hw) & (d >= -hw) & ((qi >> shift) == (kj >> shift))
        bias = jnp.where(valid, 0.0, NEG_INF)
        qb = q_ref[pl.ds(q0, bq), :]
        kw = k_ref[pl.ds(ws, win), :]
        vw = v_ref[pl.ds(ws, win), :]
        acc = {qo: jnp.zeros((bq, bw), jnp.float32) for qo in q_offs}
        lacc = {qo: jnp.zeros((bq, bw), jnp.float32) for qo in q_offs}
        for ko in k_offs:
            group = [hd for hd in heads if hd[2] == ko]
            hms = [(lane >= mo) & (lane < mo + 64) for _, mo, _, _ in group]
            qs = jnp.concatenate(
                [jnp.where(hm, qb[:, qo:qo + bw], jnp.zeros((bq, bw), qb.dtype))
                 for hm, (qo, _, _, _) in zip(hms, group)], axis=0)
            s_all = _dot_nt(qs, kw[:, ko:ko + bw])
            es, ms, ls = [], [], []
            for n, (_, _, _, hid) in enumerate(group):
                s = s_all[n * bq:(n + 1) * bq] + bias
                m = (jnp.full((bq, 1), bound_ref[0], jnp.float32) if bounded
                     else jnp.max(s, axis=-1, keepdims=True))
                if with_sink:
                    sk = sink_ref[hid] * LOG2E
                    m = jnp.maximum(m, sk)
                e = jnp.exp2(s - m)
                l = jnp.sum(e, axis=-1, keepdims=True)
                if with_sink:
                    l = l + jnp.exp2(sk - m)
                es.append(e.astype(jnp.bfloat16))
                ms.append(m)
                ls.append(l)
            o_all = _dot(jnp.concatenate(es, axis=0), vw[:, ko:ko + bw])
            for n, (hm, (qo, _, _, _)) in enumerate(zip(hms, group)):
                o = o_all[n * bq:(n + 1) * bq] * (1.0 / ls[n])
                acc[qo] = jnp.where(hm, o, acc[qo])
                if with_lse:
                    lacc[qo] = jnp.where(hm, (ms[n] + jnp.log2(ls[n])) * LN2, lacc[qo])
        for qo in q_offs:
            o_ref[pl.ds(q0, bq), qo:qo + bw] = acc[qo].astype(o_ref.dtype)
            if with_lse:
                lse_ref[pl.ds(q0, bq), qo:qo + bw] = lacc[qo]
        return carry

    lax.fori_loop(0, ch // bq, body, 0, unroll=2)


def _band_call(q, k, v, bound, bounded, *, heads, bw, bq, hw, seg_len, sink=None, with_lse, out_dtype, name):
    t, wq = q.shape
    wk = k.shape[1]
    ch = SEQ
    kern = functools.partial(_band_kernel, heads=heads, bw=bw, bq=bq, hw=hw, seg_len=seg_len,
                             with_sink=sink is not None, with_lse=with_lse, bounded=bounded)
    in_specs = [
        pl.BlockSpec((ch, wq), lambda i: (i, 0)),
        pl.BlockSpec((ch, wk), lambda i: (i, 0)),
        pl.BlockSpec((ch, wk), lambda i: (i, 0)),
    ]
    args = [q, k, v]
    if sink is not None:
        in_specs = [pl.BlockSpec(memory_space=pltpu.SMEM)] + in_specs
        args = [sink] + args
    in_specs = [pl.BlockSpec(memory_space=pltpu.SMEM)] + in_specs
    args = [bound] + args
    out_shape = [jax.ShapeDtypeStruct((t, wq), out_dtype)]
    out_specs = [pl.BlockSpec((ch, wq), lambda i: (i, 0))]
    if with_lse:
        out_shape.append(jax.ShapeDtypeStruct((t, wq), jnp.float32))
        out_specs.append(pl.BlockSpec((ch, wq), lambda i: (i, 0)))
    return pl.pallas_call(
        kern, grid=(t // ch,), in_specs=in_specs, out_specs=out_specs, out_shape=out_shape,
        compiler_params=_params(1), name=name,
    )(*args)


B_HEAD_SPECS = tuple((0, h * B_HD, 0, h) for h in range(B_HEADS))
D_HEAD_SPECS = tuple(((h // 2) * LANES, (h % 2) * D_HD, (h // 4) * LANES, h) for h in range(D_QHEADS))


def _sigmoid(z):
    return 1.0 / (1.0 + jnp.exp(-z))


def _merge_kernel(*refs, n_x, first_blocks):
    x_refs, refs = refs[:n_x], refs[n_x:]
    (g1_ref, wg_ref, oa_ref, ob0, ob1, ob2, ls0, ls1, ls2, oc_ref, od_ref,
     wa_ref, wb_ref, wc_ref, wd_ref, wo_ref, out_ref) = refs[:17]
    res_scr = list(refs[17:])

    def load(ref):
        if len(ref.shape) == 2:
            return ref[...]
        dil, rows = ref.shape[1], ref.shape[2]
        scr = res_scr.pop()
        n_c = scr.shape[0]
        for r in range(dil):
            for c in range(n_c):
                scr[c, pl.ds(r, rows, stride=dil), :] = ref[0, r, :, c * LANES:(c + 1) * LANES]
        return jnp.concatenate([scr[c] for c in range(n_c)], axis=-1)

    x = _row_tile(x_refs, first_blocks)
    h = _rms(x, g1_ref[...]).astype(jnp.bfloat16)
    l0, l1, l2 = load(ls0), load(ls1), load(ls2)
    lm = jnp.maximum(jnp.maximum(l0, l1), l2)
    e0, e1, e2 = jnp.exp(l0 - lm), jnp.exp(l1 - lm), jnp.exp(l2 - lm)
    den = e0 + e1 + e2
    ob = ((e0 / den) * load(ob0) + (e1 / den) * load(ob1) + (e2 / den) * load(ob2)).astype(jnp.bfloat16)
    branches = ((oa_ref[...], wa_ref), (ob, wb_ref), (oc_ref[...], wc_ref), (od_ref[...], wd_ref))
    merged = None
    for i, (o, w_ref) in enumerate(branches):
        gate = _sigmoid(_dot(h, wg_ref[:, i * D_MODEL:(i + 1) * D_MODEL]))
        term = gate * _dot(o, w_ref[...])
        merged = term if merged is None else merged + term
    out_ref[...] = x + _dot(merged.astype(jnp.bfloat16), wo_ref[...])


def _merge_call(xs, lw, oa, obs, lses, oc, od):
    rows = [x.shape[0] for x in xs]
    t = sum(rows)
    tm = TM_MERGE

    def tile(w):
        return pl.BlockSpec((tm, w), lambda i: (i, 0))

    n_pos = SEQ // tm

    def band_tile(dil):
        if dil == 1:
            return tile(B_W)
        return pl.BlockSpec((1, dil, tm // dil, B_W), lambda i: (i // n_pos, 0, i % n_pos, 0))

    b_specs = [band_tile(dil) for _, dil in B_PATTERNS]
    in_specs = _row_specs(rows, tm, D_MODEL) + [
                _const_spec((1, D_MODEL)), _const_spec((D_MODEL, N_BRANCH * D_MODEL)),
                tile(512)] + b_specs + b_specs + [tile(512), tile(512),
                _const_spec((512, D_MODEL)), _const_spec((256, D_MODEL)), _const_spec((512, D_MODEL)),
                _const_spec((512, D_MODEL)), _const_spec((D_MODEL, D_MODEL))]
    n_res = 2 * sum(1 for _, dil in B_PATTERNS if dil > 1)
    return pl.pallas_call(
        functools.partial(_merge_kernel, n_x=len(xs), first_blocks=rows[0] // tm),
        grid=(t // tm,), in_specs=in_specs, out_specs=tile(D_MODEL),
        out_shape=jax.ShapeDtypeStruct((t, D_MODEL), jnp.float32),
        scratch_shapes=[pltpu.VMEM((B_W // LANES, tm, LANES), jnp.float32)] * n_res,
        compiler_params=_params(1), name="merge",
    )(*xs, lw["g1"], lw["w_gate"], oa, *obs, *lses, oc, od,
      lw["w_br_a"], lw["w_br_b"], lw["w_br_c"], lw["w_br_d"], lw["w_o"])


def _ffn_kernel(x_ref, g2_ref, wg_ref, wu_ref, wd_ref, *out_refs, first_blocks):
    x = x_ref[...]
    hf = _rms(x, g2_ref[...]).astype(jnp.bfloat16)
    acc = x
    for off, width in FFN_CHUNKS:
        a = _dot(hf, wg_ref[:, off:off + width])
        u = _dot(hf, wu_ref[:, off:off + width])
        act = (a * _sigmoid(a) * u).astype(jnp.bfloat16)
        acc = acc + _dot(act, wd_ref[off:off + width, :])
    if len(out_refs) == 1:
        out_refs[0][...] = acc
    else:
        @pl.when(pl.program_id(0) < first_blocks)
        def _():
            out_refs[0][...] = acc

        @pl.when(pl.program_id(0) >= first_blocks)
        def _():
            out_refs[1][...] = acc


def _ffn_call(x, lw, out_rows):
    t = x.shape[0]
    tm = TM_FFN
    tile = pl.BlockSpec((tm, D_MODEL), lambda i: (i, 0))
    return pl.pallas_call(
        functools.partial(_ffn_kernel, first_blocks=out_rows[0] // tm), grid=(t // tm,),
        in_specs=[tile, _const_spec((1, D_MODEL)), _const_spec((D_MODEL, D_FF)),
                  _const_spec((D_MODEL, D_FF)), _const_spec((D_FF, D_MODEL))],
        out_specs=_row_specs(out_rows, tm, D_MODEL),
        out_shape=[jax.ShapeDtypeStruct((r, D_MODEL), jnp.float32) for r in out_rows],
        compiler_params=_params(1), name="ffn",
    )(x, lw["g2"], lw["w_ffn_gate"], lw["w_ffn_up"], lw["w_ffn_down"])


def _tables():
    pos = jnp.arange(SEQ, dtype=jnp.float32)[:, None]
    lane = jnp.arange(LANES)
    inv64 = jnp.power(ROPE_THETA, -jnp.arange(32, dtype=jnp.float32) / 32)
    ang64 = pos * inv64[lane % 32][None, :]
    sign64 = jnp.where((lane % 64) < 32, -1.0, 1.0)[None, :]
    cos64 = jnp.cos(ang64)
    sin64 = jnp.sin(ang64) * sign64
    invc = jnp.power(ROPE_THETA, -jnp.arange(16, dtype=jnp.float32) / 16)
    angc = pos * invc[lane % 16][None, :]
    is_rope = ((lane >= C_NOPE) & (lane < C_NOPE + C_ROPE))[None, :]
    signc = jnp.where(lane < C_NOPE + C_ROPE // 2, -1.0, 1.0)[None, :]
    cosc = jnp.where(is_rope, jnp.cos(angc), 1.0)
    sinc = jnp.where(is_rope, jnp.sin(angc) * signc, 0.0)
    cs = jnp.stack([cos64, sin64, cosc, sinc]).astype(jnp.float32)
    idx = np.arange(MXU_N)
    bd = np.stack([(idx[:, None] // 64) == (idx[None, :] // 64),
                   (idx[:, None] // 128) == (idx[None, :] // 128)]).astype(np.float32)
    return {"cs": cs, "bd": jnp.asarray(bd, dtype=jnp.bfloat16)}


def _layer_weights(l, p):
    bf = jnp.bfloat16
    f32 = jnp.float32
    cols = jnp.split(p["w_in"][l], SPLIT_IDX, axis=-1)
    zeros = lambda n: jnp.zeros((D_MODEL, n), f32)
    dk, dv = cols[16], cols[17]
    w_in = jnp.concatenate(
        list(cols[0:12]) + [cols[12], cols[13], zeros(64), cols[14], zeros(32), cols[15],
                            dk[:, :64], dk[:, :64], dk[:, 64:], dk[:, 64:],
                            dv[:, :64], dv[:, :64], dv[:, 64:], dv[:, 64:]], axis=-1).astype(bf)
    ones = lambda n: jnp.ones((n,), f32)
    qs = A_HD ** -0.5 * LOG2E
    grow = jnp.concatenate(
        [jnp.tile(p["a_qnorm_g"][l], 8) * qs, jnp.tile(p["a_knorm_g"][l], 8), ones(512)]
        + sum([[jnp.tile(p["b_qnorm_g"][l, g], 4) * qs, jnp.tile(p["b_knorm_g"][l, g], 4), ones(256)]
               for g in range(3)], [])
        + [ones(512), jnp.tile(p["d_qnorm_g"][l], 8) * qs, jnp.tile(p["d_knorm_g"][l], 4), ones(256)]
    )[None, :].astype(f32)
    wuq = p["c_w_uq"][l].reshape(C_Q_RANK, C_HEADS, C_NOPE + C_ROPE)
    wuq = jnp.pad(wuq, ((0, 0), (0, 0), (0, 32))).reshape(C_Q_RANK, C_HEADS * LANES).astype(bf)
    wukv = p["c_w_ukv"][l].reshape(C_KV_RANK, C_HEADS, C_NOPE + C_VD)
    wuk = jnp.pad(wukv[:, :, :C_NOPE], ((0, 0), (0, 0), (0, 64))).reshape(C_KV_RANK, C_HEADS * LANES).astype(bf)
    wuv = wukv[:, :, C_NOPE:].reshape(C_KV_RANK, C_HEADS * C_VD).astype(bf)
    cscale = (C_NOPE + C_ROPE) ** -0.5 * LOG2E
    pad32 = lambda g: jnp.tile(jnp.pad(g, (0, 32)), C_HEADS)[None, :].astype(f32)
    slack = (1.0 + 2.0 ** -8) ** 2
    bound_a = (A_HD * qs * slack * jnp.max(jnp.abs(p["a_qnorm_g"][l])) * jnp.max(jnp.abs(p["a_knorm_g"][l])))
    bound_c = ((C_NOPE + C_ROPE) * cscale * slack
               * jnp.max(jnp.abs(p["c_qnorm_g"][l])) * jnp.max(jnp.abs(p["c_knorm_g"][l])))
    bound_b = [(B_HD * qs * slack * jnp.max(jnp.abs(p["b_qnorm_g"][l, g]))
                * jnp.max(jnp.abs(p["b_knorm_g"][l, g]))).reshape(1).astype(f32) for g in range(len(B_PATTERNS))]
    bound_d = D_HD * qs * slack * jnp.max(jnp.abs(p["d_qnorm_g"][l])) * jnp.max(jnp.abs(p["d_knorm_g"][l]))
    return {
        "bound_a": bound_a.reshape(1).astype(f32), "bound_c": bound_c.reshape(1).astype(f32),
        "bound_b": bound_b, "bound_d": bound_d.reshape(1).astype(f32),
        "g1": p["norm1_g"][l][None, :], "w_in": w_in, "grow": grow,
        "qag": p["c_qa_norm_g"][l][None, :], "kvag": p["c_kva_norm_g"][l][None, :],
        "wuq": wuq, "wuk": wuk, "wuv": wuv,
        "gqc": pad32(p["c_qnorm_g"][l]) * cscale, "gkc": pad32(p["c_knorm_g"][l]),
        "a_lambda": p["a_lambda"][l], "subln": p["a_subln_g"][l][None, :],
        "d_sink": p["d_sink"][l],
        "w_gate": p["w_gate"][l].astype(bf),
        "w_br_a": p["w_br_a"][l].astype(bf), "w_br_b": p["w_br_b"][l].astype(bf),
        "w_br_c": p["w_br_c"][l].astype(bf), "w_br_d": p["w_br_d"][l].astype(bf),
        "w_o": p["w_o"][l].astype(bf), "g2": p["norm2_g"][l][None, :],
        "w_ffn_gate": p["w_ffn_gate"][l].astype(bf), "w_ffn_up": p["w_ffn_up"][l].astype(bf),
        "w_ffn_down": p["w_ffn_down"][l].astype(bf),
    }


def _layer(xs, l, lw, tables, out_rows):
    (qa, ka, va, qb0, kb0, vb0, qb1, kb1, vb1, qb2, kb2, vb2, qc, kc, vc, qd, kd, vd) = _proj_call(xs, lw, tables)
    oa = _full_attention(functools.partial(_attn_a_call, qa, ka, va, lw["a_lambda"], lw["subln"],
                                           lambda_init(l), lw["bound_a"]), lw["bound_a"])
    obs, lses = [], []
    for g, (qg, kg, vg) in enumerate(((qb0, kb0, vb0), (qb1, kb1, vb1), (qb2, kb2, vb2))):
        window, dil = B_PATTERNS[g]
        res_shape = qg.shape
        qg, kg, vg = (a.reshape(-1, B_W) for a in (qg, kg, vg))
        bound = lw["bound_b"][g]
        o, lse = _full_attention(functools.partial(
            _band_call, qg, kg, vg, bound, heads=B_HEAD_SPECS, bw=B_W, bq=128, hw=window // (2 * dil),
            seg_len=SEQ // dil, with_lse=True, out_dtype=jnp.float32, name=f"band_b{g}"), bound)
        obs.append(o.reshape(res_shape))
        lses.append(lse.reshape(res_shape))
    oc = _full_attention(functools.partial(_attn_c_call, qc, kc, vc, lw["bound_c"]), lw["bound_c"])
    od = _full_attention(functools.partial(
        _band_call, qd, kd, vd, lw["bound_d"], heads=D_HEAD_SPECS, bw=LANES, bq=128, hw=D_WIN, seg_len=SEQ,
        sink=lw["d_sink"], with_lse=False, out_dtype=jnp.bfloat16, name="band_d"), lw["bound_d"])[0]
    x = _merge_call(xs, lw, oa, obs, lses, oc, od)
    return _ffn_call(x, lw, out_rows)


def kernel(x_prompt, x_sample, norm1_g, w_in, w_gate, a_qnorm_g, a_knorm_g, a_lambda, a_subln_g, b_qnorm_g, b_knorm_g, c_qa_norm_g, c_kva_norm_g, c_w_uq, c_w_ukv, c_qnorm_g, c_knorm_g, d_qnorm_g, d_knorm_g, d_sink, w_br_a, w_br_b, w_br_c, w_br_d, w_o, norm2_g, w_ffn_gate, w_ffn_up, w_ffn_down):
    p = dict(norm1_g=norm1_g, w_in=w_in, w_gate=w_gate, a_qnorm_g=a_qnorm_g, a_knorm_g=a_knorm_g,
             a_lambda=a_lambda, a_subln_g=a_subln_g, b_qnorm_g=b_qnorm_g, b_knorm_g=b_knorm_g,
             c_qa_norm_g=c_qa_norm_g, c_kva_norm_g=c_kva_norm_g, c_w_uq=c_w_uq, c_w_ukv=c_w_ukv,
             c_qnorm_g=c_qnorm_g, c_knorm_g=c_knorm_g, d_qnorm_g=d_qnorm_g, d_knorm_g=d_knorm_g,
             d_sink=d_sink, w_br_a=w_br_a, w_br_b=w_br_b, w_br_c=w_br_c, w_br_d=w_br_d, w_o=w_o,
             norm2_g=norm2_g, w_ffn_gate=w_ffn_gate, w_ffn_up=w_ffn_up, w_ffn_down=w_ffn_down)
    xs = [x_prompt.reshape(-1, D_MODEL), x_sample.reshape(-1, D_MODEL)]
    rows = [x.shape[0] for x in xs]
    tables = _tables()
    for l in range(DEPTH):
        last = l == DEPTH - 1
        xs = _layer(xs, l, _layer_weights(l, p), tables, rows if last else [sum(rows)])
    return (xs[0].reshape(x_prompt.shape), xs[1].reshape(x_sample.shape))
```

```python
import functools
import math

import jax
import jax.numpy as jnp
import numpy as np
from jax import lax
from jax.experimental import pallas as pl
from jax.experimental.pallas import tpu as pltpu

D_MODEL = 1024
SEQ = 4096
DEPTH = 2
ROPE_THETA = 10000.0
EPS = 1e-6
NEG_INF = -1e30
N_BRANCH = 4
LOG2E = 1.4426950408889634
LN2 = 0.6931471805599453

A_HEADS = 4
A_HD = 64
B_PATTERNS = ((128, 1), (512, 4), (2048, 16))
B_HEADS = 4
B_HD = 64
B_W = B_HEADS * B_HD
C_HEADS = 8
C_Q_RANK = 256
C_KV_RANK = 128
C_NOPE = 64
C_ROPE = 32
C_VD = 64
D_QHEADS = 8
D_KVHEADS = 2
D_HD = 64
D_WIN = 128
D_FF = -(-8 * D_MODEL // (3 * 256)) * 256

IN_SIZES = (512, 512, 512) + (B_W,) * 9 + (C_Q_RANK, C_KV_RANK, C_ROPE, 512, 128, 128)
SPLIT_IDX = tuple(int(i) for i in np.cumsum(IN_SIZES)[:-1])

LANES = 128
MXU_N = 256

OFF_AQ, OFF_AK, OFF_AV = 0, 512, 1024
OFF_B = 1536
OFF_CQ = 3840
OFF_CKV = 4096
OFF_KR = 4224
OFF_DQ = 4352
OFF_DK = 4864
OFF_DV = 5120
N_PROJ = 5376

VMEM_LIMIT = 56 * 1024 * 1024

TM_PROJ = 512
TQ_A = 256
TQ_C = 256
HEADS_PER_STEP_A = 4
HEADS_PER_STEP_C = 4
MAX_SAFE_BOUND = 50.0
TM_MERGE = 512
TM_FFN = 512
FFN_CHUNKS = ((0, 1024), (1024, 1024), (2048, 768))


def lambda_init(layer):
    return 0.8 - 0.6 * math.exp(-0.3 * layer)


def _const_spec(shape):
    nd = len(shape)
    return pl.BlockSpec(shape, lambda *_: (0,) * nd, pipeline_mode=pl.Buffered(1))


def _params(n_grid):
    return pltpu.CompilerParams(dimension_semantics=("arbitrary",) * n_grid,
                                vmem_limit_bytes=VMEM_LIMIT)


def _dot(a, b):
    return jnp.dot(a, b, preferred_element_type=jnp.float32)


def _dot_nt(a, b):
    return lax.dot_general(a, b, (((1,), (1,)), ((), ())), preferred_element_type=jnp.float32)


def _rms(x, g):
    ms = jnp.mean(x * x, axis=-1, keepdims=True)
    return x * lax.rsqrt(ms + EPS) * g


def _row_specs(row_counts, tm, width):
    if len(row_counts) == 1:
        return [pl.BlockSpec((tm, width), lambda i: (i, 0))]
    nb = row_counts[0] // tm
    return [pl.BlockSpec((tm, width), lambda i: (jnp.minimum(i, nb - 1), 0)),
            pl.BlockSpec((tm, width), lambda i: (jnp.maximum(i - nb, 0), 0))]


def _row_tile(refs, first_blocks):
    if len(refs) == 1:
        return refs[0][...]
    return jnp.where(pl.program_id(0) < first_blocks, refs[0][...], refs[1][...])


def _group_sumsq(p, bd):
    outs = []
    for c in range(p.shape[1] // MXU_N):
        pc = p[:, c * MXU_N:(c + 1) * MXU_N]
        outs.append(_dot((pc * pc).astype(jnp.bfloat16), bd))
    return outs[0] if len(outs) == 1 else jnp.concatenate(outs, axis=-1)


def _rope_chunks(y, cos, sin, first_half, shift):
    outs = []
    for c in range(y.shape[1] // LANES):
        yc = y[:, c * LANES:(c + 1) * LANES]
        sw = jnp.where(first_half, pltpu.roll(yc, LANES - shift, 1), pltpu.roll(yc, shift, 1))
        outs.append(yc * cos + sw * sin)
    return outs[0] if len(outs) == 1 else jnp.concatenate(outs, axis=-1)


def _proj_kernel(*refs, n_x, first_blocks):
    x_refs, refs = refs[:n_x], refs[n_x:]
    (g1_ref, w_ref, grow_ref, bd_ref, cs_ref, qag_ref, kvag_ref,
     wuq_ref, wuk_ref, wuv_ref, gqc_ref, gkc_ref,
     qa, ka, va, qb0, kb0, vb0, qb1, kb1, vb1, qb2, kb2, vb2,
     qc, kc, vc, qd, kd, vd) = refs[:30]
    res_scr = list(refs[30:])
    tm = x_refs[0].shape[0]
    h = _rms(_row_tile(x_refs, first_blocks), g1_ref[...]).astype(jnp.bfloat16)
    bd64 = bd_ref[0]
    bd128 = bd_ref[1]
    cos64, sin64, cosc, sinc = cs_ref[0], cs_ref[1], cs_ref[2], cs_ref[3]
    lane = lax.broadcasted_iota(jnp.int32, (tm, LANES), 1)
    first64 = (lane & 63) < 32
    firstc = lane < (C_NOPE + C_ROPE // 2)

    def proj(off, width):
        return _dot(h, w_ref[:, off:off + width])

    def store(out_ref, val):
        if len(out_ref.shape) == 2:
            out_ref[...] = val.astype(out_ref.dtype)
            return
        dil, rows = out_ref.shape[1], out_ref.shape[2]
        scr = res_scr.pop()
        n_c = scr.shape[0]
        for c in range(n_c):
            scr[c] = val[:, c * LANES:(c + 1) * LANES]
        for r in range(dil):
            out_ref[0, r] = jnp.concatenate(
                [scr[c, pl.ds(r, rows, stride=dil), :] for c in range(n_c)], axis=-1).astype(out_ref.dtype)

    def norm_rope64(off, width, out_ref):
        p = proj(off, width)
        ss = _group_sumsq(p, bd64)
        y = p * lax.rsqrt(ss * (1.0 / 64) + EPS) * grow_ref[:, off:off + width]
        store(out_ref, _rope_chunks(y, cos64, sin64, first64, 32))

    def plain(off, width, out_ref):
        store(out_ref, proj(off, width))

    cqn = _rms(proj(OFF_CQ, C_Q_RANK), qag_ref[...]).astype(jnp.bfloat16)
    qfull = _dot(cqn, wuq_ref[...])
    qn = qfull * lax.rsqrt(_group_sumsq(qfull, bd128) * (1.0 / 96) + EPS) * gqc_ref[...]
    qc[...] = _rope_chunks(qn, cosc, sinc, firstc, C_ROPE // 2).astype(qc.dtype)
    ckvn = _rms(proj(OFF_CKV, C_KV_RANK), kvag_ref[...]).astype(jnp.bfloat16)
    kr = proj(OFF_KR, LANES)
    kfull = _dot(ckvn, wuk_ref[...]) + jnp.concatenate([kr] * C_HEADS, axis=-1)
    kn = kfull * lax.rsqrt(_group_sumsq(kfull, bd128) * (1.0 / 96) + EPS) * gkc_ref[...]
    kc[...] = _rope_chunks(kn, cosc, sinc, firstc, C_ROPE // 2).astype(kc.dtype)
    vc[...] = _dot(ckvn, wuv_ref[...]).astype(vc.dtype)

    b_outs = ((qb0, kb0, vb0), (qb1, kb1, vb1), (qb2, kb2, vb2))
    norm_rope64(OFF_AQ, 512, qa)
    norm_rope64(OFF_AK, 512, ka)
    for g, (qo, ko, _) in enumerate(b_outs):
        norm_rope64(OFF_B + g * 768, 256, qo)
        norm_rope64(OFF_B + g * 768 + 256, 256, ko)
    norm_rope64(OFF_DQ, 512, qd)
    norm_rope64(OFF_DK, 256, kd)
    plain(OFF_AV, 512, va)
    for g, (_, _, vo) in enumerate(b_outs):
        plain(OFF_B + g * 768 + 512, 256, vo)
    plain(OFF_DV, 256, vd)


def _proj_call(xs, lw, tables):
    rows = [x.shape[0] for x in xs]
    t = sum(rows)
    tm = TM_PROJ
    n_pos = SEQ // tm
    widths = (512, 512, 512) + (256,) * 9 + (1024, 1024, 512, 512, 256, 256)
    out_shape = [jax.ShapeDtypeStruct((t, w), jnp.bfloat16) for w in widths]
    out_specs = [pl.BlockSpec((tm, w), lambda i: (i, 0)) for w in widths]
    for g, (_, dil) in enumerate(B_PATTERNS):
        if dil > 1:
            for j in range(3 + 3 * g, 6 + 3 * g):
                out_shape[j] = jax.ShapeDtypeStruct((t // SEQ, dil, SEQ // dil, B_W), jnp.bfloat16)
                out_specs[j] = pl.BlockSpec((1, dil, tm // dil, B_W), lambda i: (i // n_pos, 0, i % n_pos, 0))
    n_res = 3 * sum(1 for _, dil in B_PATTERNS if dil > 1)
    in_specs = _row_specs(rows, tm, D_MODEL) + [
        _const_spec((1, D_MODEL)),
        _const_spec((D_MODEL, N_PROJ)),
        _const_spec((1, N_PROJ)),
        _const_spec((2, MXU_N, MXU_N)),
        pl.BlockSpec((4, tm, LANES), lambda i: (0, i % n_pos, 0)),
        _const_spec((1, C_Q_RANK)),
        _const_spec((1, C_KV_RANK)),
        _const_spec((C_Q_RANK, 1024)),
        _const_spec((C_KV_RANK, 1024)),
        _const_spec((C_KV_RANK, 512)),
        _const_spec((1, 1024)),
        _const_spec((1, 1024)),
    ]
    return pl.pallas_call(
        functools.partial(_proj_kernel, n_x=len(xs), first_blocks=rows[0] // tm),
        grid=(t // tm,), in_specs=in_specs, out_specs=out_specs, out_shape=out_shape,
        scratch_shapes=[pltpu.VMEM((B_W // LANES, tm, LANES), jnp.float32)] * n_res,
        compiler_params=_params(1), name="proj",
    )(*xs, lw["g1"], lw["w_in"], lw["grow"], tables["bd"], tables["cs"], lw["qag"], lw["kvag"],
      lw["wuq"], lw["wuk"], lw["wuv"], lw["gqc"], lw["gkc"])


def _lane_fold(x, op):
    acc = x[:, :LANES]
    for t in range(1, x.shape[1] // LANES):
        acc = op(acc, x[:, t * LANES:(t + 1) * LANES])
    return acc


def _online_step(state, s, v):
    r = jnp.max(_lane_fold(s, jnp.maximum), axis=-1, keepdims=True)
    if state is None:
        e = jnp.exp2(s - r)
        return r, _lane_fold(e, jnp.add), _dot(e.astype(jnp.bfloat16), v)
    m, l, acc = state
    m_new = jnp.maximum(m, r)
    alpha = jnp.exp2(m - m_new)
    e = jnp.exp2(s - m_new)
    return m_new, alpha * l + _lane_fold(e, jnp.add), alpha * acc + _dot(e.astype(jnp.bfloat16), v)


def _online_finish(state):
    _, l, acc = state
    return acc * (1.0 / jnp.sum(l, axis=-1, keepdims=True))


def _attn_a_kernel(bound_ref, lam_ref, q_ref, k_ref, v_ref, g_ref, o_ref, *, lam_init, bounded):
    lp = lam_ref[...]
    lam = (jnp.exp(jnp.sum(lp[0:1] * lp[1:2], axis=-1, keepdims=True))
           - jnp.exp(jnp.sum(lp[2:3] * lp[3:4], axis=-1, keepdims=True)) + lam_init)
    lane = lax.broadcasted_iota(jnp.int32, (q_ref.shape[0], LANES), 1)
    n_heads = q_ref.shape[1] // LANES

    def scores(h):
        hs = slice(h * LANES, (h + 1) * LANES)
        q = q_ref[:, hs]
        k = k_ref[:, hs]
        return [_dot_nt(jnp.where((lane >= c * A_HD) & (lane < (c + 1) * A_HD), q, jnp.zeros_like(q)), k)
                for c in range(2)]

    def finish(h, o):
        o = _rms(o, g_ref[...]) * (1.0 - lam_init)
        o_ref[:, h * LANES:(h + 1) * LANES] = o.astype(o_ref.dtype)

    if bounded:
        bound = bound_ref[0]
        def exps(h):
            es, ls = [], []
            for s in scores(h):
                e = jnp.exp2(s - bound)
                ls.append(jnp.sum(_lane_fold(e, jnp.add), axis=-1, keepdims=True))
                es.append(e)
            return es, ls

        nxt = exps(0)
        for h in range(n_heads):
            (e0, e1), (l0, l1) = nxt
            if h + 1 < n_heads:
                nxt = exps(h + 1)
            w = e0 - (lam * l0 / l1) * e1
            finish(h, _dot(w.astype(jnp.bfloat16), v_ref[:, h * LANES:(h + 1) * LANES]) * (1.0 / l0))
        return

    s_next = scores(0)
    for h in range(n_heads):
        s_cur = s_next
        if h + 1 < n_heads:
            s_next = scores(h + 1)
        es, inv = [], []
        for s in s_cur:
            e = jnp.exp2(s - jnp.max(s, axis=-1, keepdims=True))
            es.append(e)
            inv.append(1.0 / jnp.sum(e, axis=-1, keepdims=True))
        w = es[0] * inv[0] - es[1] * (lam * inv[1])
        finish(h, _dot(w.astype(jnp.bfloat16), v_ref[:, h * LANES:(h + 1) * LANES]))


def _attn_a_call(q, k, v, a_lambda, subln_row, lam_init, bound, bounded):
    t = q.shape[0]
    nseq = t // SEQ
    tq = TQ_A
    nq = SEQ // tq
    w = HEADS_PER_STEP_A * LANES
    return pl.pallas_call(
        functools.partial(_attn_a_kernel, lam_init=lam_init, bounded=bounded),
        grid=(nseq, A_HEADS // HEADS_PER_STEP_A, nq),
        in_specs=[
            pl.BlockSpec(memory_space=pltpu.SMEM),
            _const_spec((4, A_HD)),
            pl.BlockSpec((tq, w), lambda s, h, i: (s * nq + i, h)),
            pl.BlockSpec((SEQ, w), lambda s, h, i: (s, h)),
            pl.BlockSpec((SEQ, w), lambda s, h, i: (s, h)),
            _const_spec((1, LANES)),
        ],
        out_specs=pl.BlockSpec((tq, w), lambda s, h, i: (s * nq + i, h)),
        out_shape=jax.ShapeDtypeStruct((t, A_HEADS * LANES), jnp.bfloat16),
        compiler_params=_params(3), name="attn_a",
    )(bound, a_lambda, q, k, v, subln_row)


def _attn_c_kernel(bound_ref, q_ref, k_ref, v_ref, o_ref, *, bounded):
    lane = lax.broadcasted_iota(jnp.int32, (o_ref.shape[0], LANES), 1)
    n_heads = q_ref.shape[1] // LANES

    def scores(h):
        hs = slice(h * LANES, (h + 1) * LANES)
        return _dot_nt(q_ref[:, hs], k_ref[:, hs])

    outs = []
    if bounded:
        bound = bound_ref[0]
        for h in range(n_heads):
            e = jnp.exp2(scores(h) - bound)
            l = jnp.sum(_lane_fold(e, jnp.add), axis=-1, keepdims=True)
            outs.append(_dot(e.astype(jnp.bfloat16), v_ref[:, (h // 2) * LANES:(h // 2 + 1) * LANES]) * (1.0 / l))
    else:
        s_next = scores(0)
        for h in range(n_heads):
            s = s_next
            if h + 1 < n_heads:
                s_next = scores(h + 1)
            v = v_ref[:, (h // 2) * LANES:(h // 2 + 1) * LANES]
            half = s.shape[0] // 2
            parts = [_online_step(None, s[r:r + half], v) for r in (0, half)]
            outs.append(_online_finish(tuple(jnp.concatenate([a, b], axis=0) for a, b in zip(*parts))))
    for j in range(n_heads // 2):
        o_ref[:, j * LANES:(j + 1) * LANES] = jnp.where(
            lane < C_VD, outs[2 * j], outs[2 * j + 1]).astype(o_ref.dtype)


def _attn_c_call(q, k, v, bound, bounded):
    t = q.shape[0]
    nseq = t // SEQ
    tq = TQ_C
    nq = SEQ // tq
    hps = HEADS_PER_STEP_C
    return pl.pallas_call(
        functools.partial(_attn_c_kernel, bounded=bounded),
        grid=(nseq, C_HEADS // hps, nq),
        in_specs=[
            pl.BlockSpec(memory_space=pltpu.SMEM),
            pl.BlockSpec((tq, hps * LANES), lambda s, j, i: (s * nq + i, j)),
            pl.BlockSpec((SEQ, hps * LANES), lambda s, j, i: (s, j)),
            pl.BlockSpec((SEQ, hps * C_VD), lambda s, j, i: (s, j)),
        ],
        out_specs=pl.BlockSpec((tq, hps * C_VD), lambda s, j, i: (s * nq + i, j)),
        out_shape=jax.ShapeDtypeStruct((t, C_HEADS * C_VD), jnp.bfloat16),
        compiler_params=_params(3), name="attn_c",
    )(bound, q, k, v)


def _full_attention(call, bound):
    return lax.cond(bound[0] <= MAX_SAFE_BOUND, lambda: call(True), lambda: call(False))


def _band_kernel(bound_ref, *refs, heads, bw, bq, hw, seg_len, with_sink, with_lse, bounded):
    if with_sink:
        sink_ref, q_ref, k_ref, v_ref = refs[:4]
        outs = refs[4:]
    else:
        q_ref, k_ref, v_ref = refs[:3]
        outs = refs[3:]
    o_ref = outs[0]
    lse_ref = outs[1] if with_lse else None
    ch = q_ref.shape[0]
    win = bq + 2 * hw
    shift = int(math.log2(seg_len))
    lane = lax.broadcasted_iota(jnp.int32, (bq, bw), 1)
    row = lax.broadcasted_iota(jnp.int32, (bq, win), 0)
    col = lax.broadcasted_iota(jnp.int32, (bq, win), 1)
    q_offs = sorted({hd[0] for hd in heads})
    k_offs = sorted({hd[2] for hd in heads})

    def body(i, carry):
        q0 = pl.multiple_of(i * bq, bq)
        ws = pl.multiple_of(jnp.clip(q0 - hw, 0, ch - win), hw)
        qi = q0 + row
        kj = ws + col
        d = qi - kj
        valid = (d <= hw) & (d >= -hw) & ((qi >> shift) == (kj >> shift))
        bias = jnp.where(valid, 0.0, NEG_INF)
        qb = q_ref[pl.ds(q0, bq), :]
        kw = k_ref[pl.ds(ws, win), :]
        vw = v_ref[pl.ds(ws, win), :]
        acc = {qo: jnp.zeros((bq, bw), jnp.float32) for qo in q_offs}
        lacc = {qo: jnp.zeros((bq, bw), jnp.float32) for qo in q_offs}
        for ko in k_offs:
            group = [hd for hd in heads if hd[2] == ko]
            hms = [(lane >= mo) & (lane < mo + 64) for _, mo, _, _ in group]
            qs = jnp.concatenate(
                [jnp.where(hm, qb[:, qo:qo + bw], jnp.zeros((bq, bw), qb.dtype))
                 for hm, (qo, _, _, _) in zip(hms, group)], axis=0)
            s_all = _dot_nt(qs, kw[:, ko:ko + bw])
            es, ms, ls = [], [], []
            for n, (_, _, _, hid) in enumerate(group):
                s = s_all[n * bq:(n + 1) * bq] + bias
                m = (jnp.full((bq, 1), bound_ref[0], jnp.float32) if bounded
                     else jnp.max(s, axis=-1, keepdims=True))
                if with_sink:
                    sk = sink_ref[hid] * LOG2E
                    m = jnp.maximum(m, sk)
                e = jnp.exp2(s - m)
                l = jnp.sum(e, axis=-1, keepdims=True)
                if with_sink:
                    l = l + jnp.exp2(sk - m)
                es.append(e.astype(jnp.bfloat16))
                ms.append(m)
                ls.append(l)
            o_all = _dot(jnp.concatenate(es, axis=0), vw[:, ko:ko + bw])
            for n, (hm, (qo, _, _, _)) in enumerate(zip(hms, group)):
                o = o_all[n * bq:(n + 1) * bq] * (1.0 / ls[n])
                acc[qo] = jnp.where(hm, o, acc[qo])
                if with_lse:
                    lacc[qo] = jnp.where(hm, (ms[n] + jnp.log2(ls[n])) * LN2, lacc[qo])
        for qo in q_offs:
            o_ref[pl.ds(q0, bq), qo:qo + bw] = acc[qo].astype(o_ref.dtype)
            if with_lse:
                lse_ref[pl.ds(q0, bq), qo:qo + bw] = lacc[qo]
        return carry

    lax.fori_loop(0, ch // bq, body, 0, unroll=2)


def _band_call(q, k, v, bound, bounded, *, heads, bw, bq, hw, seg_len, sink=None, with_lse, out_dtype, name):
    t, wq = q.shape
    wk = k.shape[1]
    ch = SEQ
    kern = functools.partial(_band_kernel, heads=heads, bw=bw, bq=bq, hw=hw, seg_len=seg_len,
                             with_sink=sink is not None, with_lse=with_lse, bounded=bounded)
    in_specs = [
        pl.BlockSpec((ch, wq), lambda i: (i, 0)),
        pl.BlockSpec((ch, wk), lambda i: (i, 0)),
        pl.BlockSpec((ch, wk), lambda i: (i, 0)),
    ]
    args = [q, k, v]
    if sink is not None:
        in_specs = [pl.BlockSpec(memory_space=pltpu.SMEM)] + in_specs
        args = [sink] + args
    in_specs = [pl.BlockSpec(memory_space=pltpu.SMEM)] + in_specs
    args = [bound] + args
    out_shape = [jax.ShapeDtypeStruct((t, wq), out_dtype)]
    out_specs = [pl.BlockSpec((ch, wq), lambda i: (i, 0))]
    if with_lse:
        out_shape.append(jax.ShapeDtypeStruct((t, wq), jnp.float32))
        out_specs.append(pl.BlockSpec((ch, wq), lambda i: (i, 0)))
    return pl.pallas_call(
        kern, grid=(t // ch,), in_specs=in_specs, out_specs=out_specs, out_shape=out_shape,
        compiler_params=_params(1), name=name,
    )(*args)


B_HEAD_SPECS = tuple((0, h * B_HD, 0, h) for h in range(B_HEADS))
D_HEAD_SPECS = tuple(((h // 2) * LANES, (h % 2) * D_HD, (h // 4) * LANES, h) for h in range(D_QHEADS))


def _sigmoid(z):
    return 1.0 / (1.0 + jnp.exp(-z))


def _merge_kernel(*refs, n_x, first_blocks):
    x_refs, refs = refs[:n_x], refs[n_x:]
    (g1_ref, wg_ref, oa_ref, ob0, ob1, ob2, ls0, ls1, ls2, oc_ref, od_ref,
     wa_ref, wb_ref, wc_ref, wd_ref, wo_ref, out_ref) = refs[:17]
    res_scr = list(refs[17:])

    def load(ref):
        if len(ref.shape) == 2:
            return ref[...]
        dil, rows = ref.shape[1], ref.shape[2]
        scr = res_scr.pop()
        n_c = scr.shape[0]
        for r in range(dil):
            for c in range(n_c):
                scr[c, pl.ds(r, rows, stride=dil), :] = ref[0, r, :, c * LANES:(c + 1) * LANES]
        return jnp.concatenate([scr[c] for c in range(n_c)], axis=-1)

    x = _row_tile(x_refs, first_blocks)
    h = _rms(x, g1_ref[...]).astype(jnp.bfloat16)
    l0, l1, l2 = load(ls0), load(ls1), load(ls2)
    lm = jnp.maximum(jnp.maximum(l0, l1), l2)
    e0, e1, e2 = jnp.exp(l0 - lm), jnp.exp(l1 - lm), jnp.exp(l2 - lm)
    den = e0 + e1 + e2
    ob = ((e0 / den) * load(ob0) + (e1 / den) * load(ob1) + (e2 / den) * load(ob2)).astype(jnp.bfloat16)
    branches = ((oa_ref[...], wa_ref), (ob, wb_ref), (oc_ref[...], wc_ref), (od_ref[...], wd_ref))
    merged = None
    for i, (o, w_ref) in enumerate(branches):
        gate = _sigmoid(_dot(h, wg_ref[:, i * D_MODEL:(i + 1) * D_MODEL]))
        term = gate * _dot(o, w_ref[...])
        merged = term if merged is None else merged + term
    out_ref[...] = x + _dot(merged.astype(jnp.bfloat16), wo_ref[...])


def _merge_call(xs, lw, oa, obs, lses, oc, od):
    rows = [x.shape[0] for x in xs]
    t = sum(rows)
    tm = TM_MERGE

    def tile(w):
        return pl.BlockSpec((tm, w), lambda i: (i, 0))

    n_pos = SEQ // tm

    def band_tile(dil):
        if dil == 1:
            return tile(B_W)
        return pl.BlockSpec((1, dil, tm // dil, B_W), lambda i: (i // n_pos, 0, i % n_pos, 0))

    b_specs = [band_tile(dil) for _, dil in B_PATTERNS]
    in_specs = _row_specs(rows, tm, D_MODEL) + [
                _const_spec((1, D_MODEL)), _const_spec((D_MODEL, N_BRANCH * D_MODEL)),
                tile(512)] + b_specs + b_specs + [tile(512), tile(512),
                _const_spec((512, D_MODEL)), _const_spec((256, D_MODEL)), _const_spec((512, D_MODEL)),
                _const_spec((512, D_MODEL)), _const_spec((D_MODEL, D_MODEL))]
    n_res = 2 * sum(1 for _, dil in B_PATTERNS if dil > 1)
    return pl.pallas_call(
        functools.partial(_merge_kernel, n_x=len(xs), first_blocks=rows[0] // tm),
        grid=(t // tm,), in_specs=in_specs, out_specs=tile(D_MODEL),
        out_shape=jax.ShapeDtypeStruct((t, D_MODEL), jnp.float32),
        scratch_shapes=[pltpu.VMEM((B_W // LANES, tm, LANES), jnp.float32)] * n_res,
        compiler_params=_params(1), name="merge",
    )(*xs, lw["g1"], lw["w_gate"], oa, *obs, *lses, oc, od,
      lw["w_br_a"], lw["w_br_b"], lw["w_br_c"], lw["w_br_d"], lw["w_o"])


def _ffn_kernel(x_ref, g2_ref, wg_ref, wu_ref, wd_ref, *out_refs, first_blocks):
    x = x_ref[...]
    hf = _rms(x, g2_ref[...]).astype(jnp.bfloat16)
    acc = x
    for off, width in FFN_CHUNKS:
        a = _dot(hf, wg_ref[:, off:off + width])
        u = _dot(hf, wu_ref[:, off:off + width])
        act = (a * _sigmoid(a) * u).astype(jnp.bfloat16)
        acc = acc + _dot(act, wd_ref[off:off + width, :])
    if len(out_refs) == 1:
        out_refs[0][...] = acc
    else:
        @pl.when(pl.program_id(0) < first_blocks)
        def _():
            out_refs[0][...] = acc

        @pl.when(pl.program_id(0) >= first_blocks)
        def _():
            out_refs[1][...] = acc


def _ffn_call(x, lw, out_rows):
    t = x.shape[0]
    tm = TM_FFN
    tile = pl.BlockSpec((tm, D_MODEL), lambda i: (i, 0))
    return pl.pallas_call(
        functools.partial(_ffn_kernel, first_blocks=out_rows[0] // tm), grid=(t // tm,),
        in_specs=[tile, _const_spec((1, D_MODEL)), _const_spec((D_MODEL, D_FF)),
                  _const_spec((D_MODEL, D_FF)), _const_spec((D_FF, D_MODEL))],
        out_specs=_row_specs(out_rows, tm, D_MODEL),
        out_shape=[jax.ShapeDtypeStruct((r, D_MODEL), jnp.float32) for r in out_rows],
        compiler_params=_params(1), name="ffn",
    )(x, lw["g2"], lw["w_ffn_gate"], lw["w_ffn_up"], lw["w_ffn_down"])


def _tables():
    pos = jnp.arange(SEQ, dtype=jnp.float32)[:, None]
    lane = jnp.arange(LANES)
    inv64 = jnp.power(ROPE_THETA, -jnp.arange(32, dtype=jnp.float32) / 32)
    ang64 = pos * inv64[lane % 32][None, :]
    sign64 = jnp.where((lane % 64) < 32, -1.0, 1.0)[None, :]
    cos64 = jnp.cos(ang64)
    sin64 = jnp.sin(ang64) * sign64
    invc = jnp.power(ROPE_THETA, -jnp.arange(16, dtype=jnp.float32) / 16)
    angc = pos * invc[lane % 16][None, :]
    is_rope = ((lane >= C_NOPE) & (lane < C_NOPE + C_ROPE))[None, :]
    signc = jnp.where(lane < C_NOPE + C_ROPE // 2, -1.0, 1.0)[None, :]
    cosc = jnp.where(is_rope, jnp.cos(angc), 1.0)
    sinc = jnp.where(is_rope, jnp.sin(angc) * signc, 0.0)
    cs = jnp.stack([cos64, sin64, cosc, sinc]).astype(jnp.float32)
    idx = np.arange(MXU_N)
    bd = np.stack([(idx[:, None] // 64) == (idx[None, :] // 64),
                   (idx[:, None] // 128) == (idx[None, :] // 128)]).astype(np.float32)
    return {"cs": cs, "bd": jnp.asarray(bd, dtype=jnp.bfloat16)}


def _layer_weights(l, p):
    bf = jnp.bfloat16
    f32 = jnp.float32
    cols = jnp.split(p["w_in"][l], SPLIT_IDX, axis=-1)
    zeros = lambda n: jnp.zeros((D_MODEL, n), f32)
    dk, dv = cols[16], cols[17]
    w_in = jnp.concatenate(
        list(cols[0:12]) + [cols[12], cols[13], zeros(64), cols[14], zeros(32), cols[15],
                            dk[:, :64], dk[:, :64], dk[:, 64:], dk[:, 64:],
                            dv[:, :64], dv[:, :64], dv[:, 64:], dv[:, 64:]], axis=-1).astype(bf)
    ones = lambda n: jnp.ones((n,), f32)
    qs = A_HD ** -0.5 * LOG2E
    grow = jnp.concatenate(
        [jnp.tile(p["a_qnorm_g"][l], 8) * qs, jnp.tile(p["a_knorm_g"][l], 8), ones(512)]
        + sum([[jnp.tile(p["b_qnorm_g"][l, g], 4) * qs, jnp.tile(p["b_knorm_g"][l, g], 4), ones(256)]
               for g in range(3)], [])
        + [ones(512), jnp.tile(p["d_qnorm_g"][l], 8) * qs, jnp.tile(p["d_knorm_g"][l], 4), ones(256)]
    )[None, :].astype(f32)
    wuq = p["c_w_uq"][l].reshape(C_Q_RANK, C_HEADS, C_NOPE + C_ROPE)
    wuq = jnp.pad(wuq, ((0, 0), (0, 0), (0, 32))).reshape(C_Q_RANK, C_HEADS * LANES).astype(bf)
    wukv = p["c_w_ukv"][l].reshape(C_KV_RANK, C_HEADS, C_NOPE + C_VD)
    wuk = jnp.pad(wukv[:, :, :C_NOPE], ((0, 0), (0, 0), (0, 64))).reshape(C_KV_RANK, C_HEADS * LANES).astype(bf)
    wuv = wukv[:, :, C_NOPE:].reshape(C_KV_RANK, C_HEADS * C_VD).astype(bf)
    cscale = (C_NOPE + C_ROPE) ** -0.5 * LOG2E
    pad32 = lambda g: jnp.tile(jnp.pad(g, (0, 32)), C_HEADS)[None, :].astype(f32)
    slack = (1.0 + 2.0 ** -8) ** 2
    bound_a = (A_HD * qs * slack * jnp.max(jnp.abs(p["a_qnorm_g"][l])) * jnp.max(jnp.abs(p["a_knorm_g"][l])))
    bound_c = ((C_NOPE + C_ROPE) * cscale * slack
               * jnp.max(jnp.abs(p["c_qnorm_g"][l])) * jnp.max(jnp.abs(p["c_knorm_g"][l])))
    bound_b = [(B_HD * qs * slack * jnp.max(jnp.abs(p["b_qnorm_g"][l, g]))
                * jnp.max(jnp.abs(p["b_knorm_g"][l, g]))).reshape(1).astype(f32) for g in range(len(B_PATTERNS))]
    bound_d = D_HD * qs * slack * jnp.max(jnp.abs(p["d_qnorm_g"][l])) * jnp.max(jnp.abs(p["d_knorm_g"][l]))
    return {
        "bound_a": bound_a.reshape(1).astype(f32), "bound_c": bound_c.reshape(1).astype(f32),
        "bound_b": bound_b, "bound_d": bound_d.reshape(1).astype(f32),
        "g1": p["norm1_g"][l][None, :], "w_in": w_in, "grow": grow,
        "qag": p["c_qa_norm_g"][l][None, :], "kvag": p["c_kva_norm_g"][l][None, :],
        "wuq": wuq, "wuk": wuk, "wuv": wuv,
        "gqc": pad32(p["c_qnorm_g"][l]) * cscale, "gkc": pad32(p["c_knorm_g"][l]),
        "a_lambda": p["a_lambda"][l], "subln": p["a_subln_g"][l][None, :],
        "d_sink": p["d_sink"][l],
        "w_gate": p["w_gate"][l].astype(bf),
        "w_br_a": p["w_br_a"][l].astype(bf), "w_br_b": p["w_br_b"][l].astype(bf),
        "w_br_c": p["w_br_c"][l].astype(bf), "w_br_d": p["w_br_d"][l].astype(bf),
        "w_o": p["w_o"][l].astype(bf), "g2": p["norm2_g"][l][None, :],
        "w_ffn_gate": p["w_ffn_gate"][l].astype(bf), "w_ffn_up": p["w_ffn_up"][l].astype(bf),
        "w_ffn_down": p["w_ffn_down"][l].astype(bf),
    }


def _layer(xs, l, lw, tables, out_rows):
    (qa, ka, va, qb0, kb0, vb0, qb1, kb1, vb1, qb2, kb2, vb2, qc, kc, vc, qd, kd, vd) = _proj_call(xs, lw, tables)
    oa = _full_attention(functools.partial(_attn_a_call, qa, ka, va, lw["a_lambda"], lw["subln"],
                                           lambda_init(l), lw["bound_a"]), lw["bound_a"])
    obs, lses = [], []
    for g, (qg, kg, vg) in enumerate(((qb0, kb0, vb0), (qb1, kb1, vb1), (qb2, kb2, vb2))):
        window, dil = B_PATTERNS[g]
        res_shape = qg.shape
        qg, kg, vg = (a.reshape(-1, B_W) for a in (qg, kg, vg))
        bound = lw["bound_b"][g]
        o, lse = _full_attention(functools.partial(
            _band_call, qg, kg, vg, bound, heads=B_HEAD_SPECS, bw=B_W, bq=128, hw=window // (2 * dil),
            seg_len=SEQ // dil, with_lse=True, out_dtype=jnp.float32, name=f"band_b{g}"), bound)
        obs.append(o.reshape(res_shape))
        lses.append(lse.reshape(res_shape))
    oc = _full_attention(functools.partial(_attn_c_call, qc, kc, vc, lw["bound_c"]), lw["bound_c"])
    od = _full_attention(functools.partial(
        _band_call, qd, kd, vd, lw["bound_d"], heads=D_HEAD_SPECS, bw=LANES, bq=128, hw=D_WIN, seg_len=SEQ,
        sink=lw["d_sink"], with_lse=False, out_dtype=jnp.bfloat16, name="band_d"), lw["bound_d"])[0]
    x = _merge_call(xs, lw, oa, obs, lses, oc, od)
    return _ffn_call(x, lw, out_rows)


def kernel(x_prompt, x_sample, norm1_g, w_in, w_gate, a_qnorm_g, a_knorm_g, a_lambda, a_subln_g, b_qnorm_g, b_knorm_g, c_qa_norm_g, c_kva_norm_g, c_w_uq, c_w_ukv, c_qnorm_g, c_knorm_g, d_qnorm_g, d_knorm_g, d_sink, w_br_a, w_br_b, w_br_c, w_br_d, w_o, norm2_g, w_ffn_gate, w_ffn_up, w_ffn_down):
    p = dict(norm1_g=norm1_g, w_in=w_in, w_gate=w_gate, a_qnorm_g=a_qnorm_g, a_knorm_g=a_knorm_g,
             a_lambda=a_lambda, a_subln_g=a_subln_g, b_qnorm_g=b_qnorm_g, b_knorm_g=b_knorm_g,
             c_qa_norm_g=c_qa_norm_g, c_kva_norm_g=c_kva_norm_g, c_w_uq=c_w_uq, c_w_ukv=c_w_ukv,
             c_qnorm_g=c_qnorm_g, c_knorm_g=c_knorm_g, d_qnorm_g=d_qnorm_g, d_knorm_g=d_knorm_g,
             d_sink=d_sink, w_br_a=w_br_a, w_br_b=w_br_b, w_br_c=w_br_c, w_br_d=w_br_d, w_o=w_o,
             norm2_g=norm2_g, w_ffn_gate=w_ffn_gate, w_ffn_up=w_ffn_up, w_ffn_down=w_ffn_down)
    xs = [x_prompt.reshape(-1, D_MODEL), x_sample.reshape(-1, D_MODEL)]
    rows = [x.shape[0] for x in xs]
    tables = _tables()
    for l in range(DEPTH):
        last = l == DEPTH - 1
        xs = _layer(xs, l, _layer_weights(l, p), tables, rows if last else [sum(rows)])
    return (xs[0].reshape(x_prompt.shape), xs[1].reshape(x_sample.shape))
```

```python
import functools
import math

import jax
import jax.numpy as jnp
import numpy as np
from jax import lax
from jax.experimental import pallas as pl
from jax.experimental.pallas import tpu as pltpu

D_MODEL = 1024
SEQ = 4096
DEPTH = 2
ROPE_THETA = 10000.0
EPS = 1e-6
NEG_INF = -1e30
N_BRANCH = 4
LOG2E = 1.4426950408889634
LN2 = 0.6931471805599453

A_HEADS = 4
A_HD = 64
B_PATTERNS = ((128, 1), (512, 4), (2048, 16))
B_HEADS = 4
B_HD = 64
B_W = B_HEADS * B_HD
C_HEADS = 8
C_Q_RANK = 256
C_KV_RANK = 128
C_NOPE = 64
C_ROPE = 32
C_VD = 64
D_QHEADS = 8
D_KVHEADS = 2
D_HD = 64
D_WIN = 128
D_FF = -(-8 * D_MODEL // (3 * 256)) * 256

IN_SIZES = (512, 512, 512) + (B_W,) * 9 + (C_Q_RANK, C_KV_RANK, C_ROPE, 512, 128, 128)
SPLIT_IDX = tuple(int(i) for i in np.cumsum(IN_SIZES)[:-1])

LANES = 128
MXU_N = 256

OFF_AQ, OFF_AK, OFF_AV = 0, 512, 1024
OFF_B = 1536
OFF_CQ = 3840
OFF_CKV = 4096
OFF_KR = 4224
OFF_DQ = 4352
OFF_DK = 4864
OFF_DV = 5120
N_PROJ = 5376

VMEM_LIMIT = 56 * 1024 * 1024

TM_PROJ = 512
TQ_A = 256
TQ_C = 256
HEADS_PER_STEP_A = 4
HEADS_PER_STEP_C = 4
BAND_UNROLL = 8
MAX_SAFE_BOUND = 50.0
TM_MERGE = 512
TM_FFN = 512
FFN_CHUNKS = ((0, 1024), (1024, 1024), (2048, 768))


def lambda_init(layer):
    return 0.8 - 0.6 * math.exp(-0.3 * layer)


def _const_spec(shape):
    nd = len(shape)
    return pl.BlockSpec(shape, lambda *_: (0,) * nd, pipeline_mode=pl.Buffered(1))


def _params(n_grid):
    return pltpu.CompilerParams(dimension_semantics=("arbitrary",) * n_grid,
                                vmem_limit_bytes=VMEM_LIMIT)


def _dot(a, b):
    return jnp.dot(a, b, preferred_element_type=jnp.float32)


def _dot_nt(a, b):
    return lax.dot_general(a, b, (((1,), (1,)), ((), ())), preferred_element_type=jnp.float32)


def _rms(x, g):
    ms = jnp.mean(x * x, axis=-1, keepdims=True)
    return x * lax.rsqrt(ms + EPS) * g


def _row_specs(row_counts, tm, width):
    if len(row_counts) == 1:
        return [pl.BlockSpec((tm, width), lambda i: (i, 0))]
    nb = row_counts[0] // tm
    return [pl.BlockSpec((tm, width), lambda i: (jnp.minimum(i, nb - 1), 0)),
            pl.BlockSpec((tm, width), lambda i: (jnp.maximum(i - nb, 0), 0))]


def _row_tile(refs, first_blocks):
    if len(refs) == 1:
        return refs[0][...]
    return jnp.where(pl.program_id(0) < first_blocks, refs[0][...], refs[1][...])


def _group_sumsq(p, bd):
    outs = []
    for c in range(p.shape[1] // MXU_N):
        pc = p[:, c * MXU_N:(c + 1) * MXU_N]
        outs.append(_dot((pc * pc).astype(jnp.bfloat16), bd))
    return outs[0] if len(outs) == 1 else jnp.concatenate(outs, axis=-1)


def _rope_chunks(y, cos, sin, first_half, shift):
    outs = []
    for c in range(y.shape[1] // LANES):
        yc = y[:, c * LANES:(c + 1) * LANES]
        sw = jnp.where(first_half, pltpu.roll(yc, LANES - shift, 1), pltpu.roll(yc, shift, 1))
        outs.append(yc * cos + sw * sin)
    return outs[0] if len(outs) == 1 else jnp.concatenate(outs, axis=-1)


def _proj_kernel(*refs, n_x, first_blocks):
    x_refs, refs = refs[:n_x], refs[n_x:]
    (g1_ref, w_ref, grow_ref, bd_ref, cs_ref, qag_ref, kvag_ref,
     wuq_ref, wuk_ref, wuv_ref, gqc_ref, gkc_ref,
     qa, ka, va, qb0, kb0, vb0, qb1, kb1, vb1, qb2, kb2, vb2,
     qc, kc, vc, qd, kd, vd) = refs[:30]
    res_scr = list(refs[30:])
    tm = x_refs[0].shape[0]
    h = _rms(_row_tile(x_refs, first_blocks), g1_ref[...]).astype(jnp.bfloat16)
    bd64 = bd_ref[0]
    bd128 = bd_ref[1]
    cos64, sin64, cosc, sinc = cs_ref[0], cs_ref[1], cs_ref[2], cs_ref[3]
    lane = lax.broadcasted_iota(jnp.int32, (tm, LANES), 1)
    first64 = (lane & 63) < 32
    firstc = lane < (C_NOPE + C_ROPE // 2)

    def proj(off, width):
        return _dot(h, w_ref[:, off:off + width])

    def store(out_ref, val):
        if len(out_ref.shape) == 2:
            out_ref[...] = val.astype(out_ref.dtype)
            return
        dil, rows = out_ref.shape[1], out_ref.shape[2]
        scr = res_scr.pop()
        n_c = scr.shape[0]
        for c in range(n_c):
            scr[c] = val[:, c * LANES:(c + 1) * LANES]
        for r in range(dil):
            out_ref[0, r] = jnp.concatenate(
                [scr[c, pl.ds(r, rows, stride=dil), :] for c in range(n_c)], axis=-1).astype(out_ref.dtype)

    def norm_rope64(off, width, out_ref):
        p = proj(off, width)
        ss = _group_sumsq(p, bd64)
        y = p * lax.rsqrt(ss * (1.0 / 64) + EPS) * grow_ref[:, off:off + width]
        store(out_ref, _rope_chunks(y, cos64, sin64, first64, 32))

    def plain(off, width, out_ref):
        store(out_ref, proj(off, width))

    cqn = _rms(proj(OFF_CQ, C_Q_RANK), qag_ref[...]).astype(jnp.bfloat16)
    qfull = _dot(cqn, wuq_ref[...])
    qn = qfull * lax.rsqrt(_group_sumsq(qfull, bd128) * (1.0 / 96) + EPS) * gqc_ref[...]
    qc[...] = _rope_chunks(qn, cosc, sinc, firstc, C_ROPE // 2).astype(qc.dtype)
    ckvn = _rms(proj(OFF_CKV, C_KV_RANK), kvag_ref[...]).astype(jnp.bfloat16)
    kr = proj(OFF_KR, LANES)
    kfull = _dot(ckvn, wuk_ref[...]) + jnp.concatenate([kr] * C_HEADS, axis=-1)
    kn = kfull * lax.rsqrt(_group_sumsq(kfull, bd128) * (1.0 / 96) + EPS) * gkc_ref[...]
    kc[...] = _rope_chunks(kn, cosc, sinc, firstc, C_ROPE // 2).astype(kc.dtype)
    vc[...] = _dot(ckvn, wuv_ref[...]).astype(vc.dtype)

    b_outs = ((qb0, kb0, vb0), (qb1, kb1, vb1), (qb2, kb2, vb2))
    norm_rope64(OFF_AQ, 512, qa)
    norm_rope64(OFF_AK, 512, ka)
    for g, (qo, ko, _) in enumerate(b_outs):
        norm_rope64(OFF_B + g * 768, 256, qo)
        norm_rope64(OFF_B + g * 768 + 256, 256, ko)
    norm_rope64(OFF_DQ, 512, qd)
    norm_rope64(OFF_DK, 256, kd)
    plain(OFF_AV, 512, va)
    for g, (_, _, vo) in enumerate(b_outs):
        plain(OFF_B + g * 768 + 512, 256, vo)
    plain(OFF_DV, 256, vd)


def _proj_call(xs, lw, tables):
    rows = [x.shape[0] for x in xs]
    t = sum(rows)
    tm = TM_PROJ
    n_pos = SEQ // tm
    widths = (512, 512, 512) + (256,) * 9 + (1024, 1024, 512, 512, 256, 256)
    out_shape = [jax.ShapeDtypeStruct((t, w), jnp.bfloat16) for w in widths]
    out_specs = [pl.BlockSpec((tm, w), lambda i: (i, 0)) for w in widths]
    for g, (_, dil) in enumerate(B_PATTERNS):
        if dil > 1:
            for j in range(3 + 3 * g, 6 + 3 * g):
                out_shape[j] = jax.ShapeDtypeStruct((t // SEQ, dil, SEQ // dil, B_W), jnp.bfloat16)
                out_specs[j] = pl.BlockSpec((1, dil, tm // dil, B_W), lambda i: (i // n_pos, 0, i % n_pos, 0))
    n_res = 3 * sum(1 for _, dil in B_PATTERNS if dil > 1)
    in_specs = _row_specs(rows, tm, D_MODEL) + [
        _const_spec((1, D_MODEL)),
        _const_spec((D_MODEL, N_PROJ)),
        _const_spec((1, N_PROJ)),
        _const_spec((2, MXU_N, MXU_N)),
        pl.BlockSpec((4, tm, LANES), lambda i: (0, i % n_pos, 0)),
        _const_spec((1, C_Q_RANK)),
        _const_spec((1, C_KV_RANK)),
        _const_spec((C_Q_RANK, 1024)),
        _const_spec((C_KV_RANK, 1024)),
        _const_spec((C_KV_RANK, 512)),
        _const_spec((1, 1024)),
        _const_spec((1, 1024)),
    ]
    return pl.pallas_call(
        functools.partial(_proj_kernel, n_x=len(xs), first_blocks=rows[0] // tm),
        grid=(t // tm,), in_specs=in_specs, out_specs=out_specs, out_shape=out_shape,
        scratch_shapes=[pltpu.VMEM((B_W // LANES, tm, LANES), jnp.float32)] * n_res,
        compiler_params=_params(1), name="proj",
    )(*xs, lw["g1"], lw["w_in"], lw["grow"], tables["bd"], tables["cs"], lw["qag"], lw["kvag"],
      lw["wuq"], lw["wuk"], lw["wuv"], lw["gqc"], lw["gkc"])


def _lane_fold(x, op):
    acc = x[:, :LANES]
    for t in range(1, x.shape[1] // LANES):
        acc = op(acc, x[:, t * LANES:(t + 1) * LANES])
    return acc


def _online_step(state, s, v):
    r = jnp.max(_lane_fold(s, jnp.maximum), axis=-1, keepdims=True)
    if state is None:
        e = jnp.exp2(s - r)
        return r, _lane_fold(e, jnp.add), _dot(e.astype(jnp.bfloat16), v)
    m, l, acc = state
    m_new = jnp.maximum(m, r)
    alpha = jnp.exp2(m - m_new)
    e = jnp.exp2(s - m_new)
    return m_new, alpha * l + _lane_fold(e, jnp.add), alpha * acc + _dot(e.astype(jnp.bfloat16), v)


def _online_finish(state):
    _, l, acc = state
    return acc * (1.0 / jnp.sum(l, axis=-1, keepdims=True))


def _attn_a_kernel(bound_ref, lam_ref, q_ref, k_ref, v_ref, g_ref, o_ref, *, lam_init, bounded):
    lp = lam_ref[...]
    lam = (jnp.exp(jnp.sum(lp[0:1] * lp[1:2], axis=-1, keepdims=True))
           - jnp.exp(jnp.sum(lp[2:3] * lp[3:4], axis=-1, keepdims=True)) + lam_init)
    lane = lax.broadcasted_iota(jnp.int32, (q_ref.shape[0], LANES), 1)
    n_heads = q_ref.shape[1] // LANES

    def scores(h):
        hs = slice(h * LANES, (h + 1) * LANES)
        q = q_ref[:, hs]
        k = k_ref[:, hs]
        return [_dot_nt(jnp.where((lane >= c * A_HD) & (lane < (c + 1) * A_HD), q, jnp.zeros_like(q)), k)
                for c in range(2)]

    def finish(h, o):
        o = _rms(o, g_ref[...]) * (1.0 - lam_init)
        o_ref[:, h * LANES:(h + 1) * LANES] = o.astype(o_ref.dtype)

    if bounded:
        bound = bound_ref[0]
        def exps(h):
            es, ls = [], []
            for s in scores(h):
                e = jnp.exp2(s - bound)
                ls.append(jnp.sum(_lane_fold(e, jnp.add), axis=-1, keepdims=True))
                es.append(e)
            return es, ls

        nxt = exps(0)
        for h in range(n_heads):
            (e0, e1), (l0, l1) = nxt
            if h + 1 < n_heads:
                nxt = exps(h + 1)
            w = e0 - (lam * l0 / l1) * e1
            finish(h, _dot(w.astype(jnp.bfloat16), v_ref[:, h * LANES:(h + 1) * LANES]) * (1.0 / l0))
        return

    s_next = scores(0)
    for h in range(n_heads):
        s_cur = s_next
        if h + 1 < n_heads:
            s_next = scores(h + 1)
        es, inv = [], []
        for s in s_cur:
            e = jnp.exp2(s - jnp.max(s, axis=-1, keepdims=True))
            es.append(e)
            inv.append(1.0 / jnp.sum(e, axis=-1, keepdims=True))
        w = es[0] * inv[0] - es[1] * (lam * inv[1])
        finish(h, _dot(w.astype(jnp.bfloat16), v_ref[:, h * LANES:(h + 1) * LANES]))


def _attn_a_call(q, k, v, a_lambda, subln_row, lam_init, bound, bounded):
    t = q.shape[0]
    nseq = t // SEQ
    tq = TQ_A
    nq = SEQ // tq
    w = HEADS_PER_STEP_A * LANES
    return pl.pallas_call(
        functools.partial(_attn_a_kernel, lam_init=lam_init, bounded=bounded),
        grid=(nseq, A_HEADS // HEADS_PER_STEP_A, nq),
        in_specs=[
            pl.BlockSpec(memory_space=pltpu.SMEM),
            _const_spec((4, A_HD)),
            pl.BlockSpec((tq, w), lambda s, h, i: (s * nq + i, h)),
            pl.BlockSpec((SEQ, w), lambda s, h, i: (s, h)),
            pl.BlockSpec((SEQ, w), lambda s, h, i: (s, h)),
            _const_spec((1, LANES)),
        ],
        out_specs=pl.BlockSpec((tq, w), lambda s, h, i: (s * nq + i, h)),
        out_shape=jax.ShapeDtypeStruct((t, A_HEADS * LANES), jnp.bfloat16),
        compiler_params=_params(3), name="attn_a",
    )(bound, a_lambda, q, k, v, subln_row)


def _attn_c_kernel(bound_ref, q_ref, k_ref, v_ref, o_ref, *, bounded):
    lane = lax.broadcasted_iota(jnp.int32, (o_ref.shape[0], LANES), 1)
    n_heads = q_ref.shape[1] // LANES

    def scores(h):
        hs = slice(h * LANES, (h + 1) * LANES)
        return _dot_nt(q_ref[:, hs], k_ref[:, hs])

    outs = []
    if bounded:
        bound = bound_ref[0]
        for h in range(n_heads):
            e = jnp.exp2(scores(h) - bound)
            l = jnp.sum(_lane_fold(e, jnp.add), axis=-1, keepdims=True)
            outs.append(_dot(e.astype(jnp.bfloat16), v_ref[:, (h // 2) * LANES:(h // 2 + 1) * LANES]) * (1.0 / l))
    else:
        s_next = scores(0)
        for h in range(n_heads):
            s = s_next
            if h + 1 < n_heads:
                s_next = scores(h + 1)
            v = v_ref[:, (h // 2) * LANES:(h // 2 + 1) * LANES]
            half = s.shape[0] // 2
            parts = [_online_step(None, s[r:r + half], v) for r in (0, half)]
            outs.append(_online_finish(tuple(jnp.concatenate([a, b], axis=0) for a, b in zip(*parts))))
    for j in range(n_heads // 2):
        o_ref[:, j * LANES:(j + 1) * LANES] = jnp.where(
            lane < C_VD, outs[2 * j], outs[2 * j + 1]).astype(o_ref.dtype)


def _attn_c_call(q, k, v, bound, bounded):
    t = q.shape[0]
    nseq = t // SEQ
    tq = TQ_C
    nq = SEQ // tq
    hps = HEADS_PER_STEP_C
    return pl.pallas_call(
        functools.partial(_attn_c_kernel, bounded=bounded),
        grid=(nseq, C_HEADS // hps, nq),
        in_specs=[
            pl.BlockSpec(memory_space=pltpu.SMEM),
            pl.BlockSpec((tq, hps * LANES), lambda s, j, i: (s * nq + i, j)),
            pl.BlockSpec((SEQ, hps * LANES), lambda s, j, i: (s, j)),
            pl.BlockSpec((SEQ, hps * C_VD), lambda s, j, i: (s, j)),
        ],
        out_specs=pl.BlockSpec((tq, hps * C_VD), lambda s, j, i: (s * nq + i, j)),
        out_shape=jax.ShapeDtypeStruct((t, C_HEADS * C_VD), jnp.bfloat16),
        compiler_params=_params(3), name="attn_c",
    )(bound, q, k, v)


def _full_attention(call, bound):
    return lax.cond(bound[0] <= MAX_SAFE_BOUND, lambda: call(True), lambda: call(False))


def _band_kernel(bound_ref, *refs, heads, bw, bq, hw, seg_len, with_sink, with_lse, bounded):
    if with_sink:
        sink_ref, q_ref, k_ref, v_ref = refs[:4]
        outs = refs[4:]
    else:
        q_ref, k_ref, v_ref = refs[:3]
        outs = refs[3:]
    o_ref = outs[0]
    lse_ref = outs[1] if with_lse else None
    ch = q_ref.shape[0]
    win = bq + 2 * hw
    shift = int(math.log2(seg_len))
    lane = lax.broadcasted_iota(jnp.int32, (bq, bw), 1)
    lane128 = lax.broadcasted_iota(jnp.int32, (bq, LANES), 1)
    assert seg_len % bq == 0 and seg_len >= win
    row_minus_col = (lax.broadcasted_iota(jnp.int32, (bq, win), 0)
                     - lax.broadcasted_iota(jnp.int32, (bq, win), 1))
    k_offs = sorted({hd[2] for hd in heads})

    def body(i, carry):
        q0 = pl.multiple_of(i * bq, bq)
        seg_lo = (q0 >> shift) << shift
        ws = pl.multiple_of(jnp.clip(q0 - hw, seg_lo, seg_lo + seg_len - win), hw)
        d = row_minus_col + (q0 - ws)
        bias = jnp.where((d <= hw) & (d >= -hw), 0.0, NEG_INF)
        qb = q_ref[pl.ds(q0, bq), :]
        kw = k_ref[pl.ds(ws, win), :]
        vw = v_ref[pl.ds(ws, win), :]
        acc, lacc = {}, {}
        for ko in k_offs:
            group = [hd for hd in heads if hd[2] == ko]
            qs = jnp.concatenate(
                [jnp.where((lane >= mo) & (lane < mo + 64), qb[:, qo:qo + bw], jnp.zeros((bq, bw), qb.dtype))
                 for qo, mo, _, _ in group], axis=0)
            s_all = _dot_nt(qs, kw[:, ko:ko + bw])
            es, ms, ls = [], [], []
            for n, (_, _, _, hid) in enumerate(group):
                s = s_all[n * bq:(n + 1) * bq] + bias
                m = (jnp.full((bq, 1), bound_ref[0], jnp.float32) if bounded
                     else jnp.max(s, axis=-1, keepdims=True))
                if with_sink:
                    sk = sink_ref[hid] * LOG2E
                    m = jnp.maximum(m, sk)
                e = jnp.exp2(s - m)
                l = jnp.sum(e, axis=-1, keepdims=True)
                if with_sink:
                    l = l + jnp.exp2(sk - m)
                es.append(e.astype(jnp.bfloat16))
                ms.append(m)
                ls.append(l)
            for half in range(bw // LANES):
                sub = [n for n, (_, mo, _, _) in enumerate(group) if mo // LANES == half]
                o_sub = _dot(jnp.concatenate([es[n] for n in sub], axis=0),
                             vw[:, ko + half * LANES:ko + (half + 1) * LANES])
                for j, n in enumerate(sub):
                    qo, mo = group[n][0], group[n][1] % LANES
                    hm = (lane128 >= mo) & (lane128 < mo + 64)
                    key = (qo, half)
                    o = o_sub[j * bq:(j + 1) * bq] * (1.0 / ls[n])
                    acc[key] = jnp.where(hm, o, acc[key]) if key in acc else o
                    if with_lse:
                        lse = jnp.broadcast_to((ms[n] + jnp.log2(ls[n])) * LN2, (bq, LANES))
                        lacc[key] = jnp.where(hm, lse, lacc[key]) if key in lacc else lse
        for (qo, half), val in acc.items():
            lanes = slice(qo + half * LANES, qo + (half + 1) * LANES)
            o_ref[pl.ds(q0, bq), lanes] = val.astype(o_ref.dtype)
            if with_lse:
                lse_ref[pl.ds(q0, bq), lanes] = lacc[(qo, half)]
        return carry

    lax.fori_loop(0, ch // bq, body, 0, unroll=BAND_UNROLL)


def _band_call(q, k, v, bound, bounded, *, heads, bw, bq, hw, seg_len, sink=None, with_lse, out_dtype, name):
    t, wq = q.shape
    wk = k.shape[1]
    ch = SEQ
    kern = functools.partial(_band_kernel, heads=heads, bw=bw, bq=bq, hw=hw, seg_len=seg_len,
                             with_sink=sink is not None, with_lse=with_lse, bounded=bounded)
    in_specs = [
        pl.BlockSpec((ch, wq), lambda i: (i, 0)),
        pl.BlockSpec((ch, wk), lambda i: (i, 0)),
        pl.BlockSpec((ch, wk), lambda i: (i, 0)),
    ]
    args = [q, k, v]
    if sink is not None:
        in_specs = [pl.BlockSpec(memory_space=pltpu.SMEM)] + in_specs
        args = [sink] + args
    in_specs = [pl.BlockSpec(memory_space=pltpu.SMEM)] + in_specs
    args = [bound] + args
    out_shape = [jax.ShapeDtypeStruct((t, wq), out_dtype)]
    out_specs = [pl.BlockSpec((ch, wq), lambda i: (i, 0))]
    if with_lse:
        out_shape.append(jax.ShapeDtypeStruct((t, wq), jnp.float32))
        out_specs.append(pl.BlockSpec((ch, wq), lambda i: (i, 0)))
    return pl.pallas_call(
        kern, grid=(t // ch,), in_specs=in_specs, out_specs=out_specs, out_shape=out_shape,
        compiler_params=_params(1), name=name,
    )(*args)


B_HEAD_SPECS = tuple((0, h * B_HD, 0, h) for h in range(B_HEADS))
D_HEAD_SPECS = tuple(((h // 2) * LANES, (h % 2) * D_HD, (h // 4) * LANES, h) for h in range(D_QHEADS))


def _sigmoid(z):
    return 1.0 / (1.0 + jnp.exp(-z))


def _merge_kernel(*refs, n_x, first_blocks):
    x_refs, refs = refs[:n_x], refs[n_x:]
    (g1_ref, wg_ref, oa_ref, ob0, ob1, ob2, ls0, ls1, ls2, oc_ref, od_ref,
     wa_ref, wb_ref, wc_ref, wd_ref, wo_ref, out_ref) = refs[:17]
    res_scr = list(refs[17:])

    def load(ref):
        if len(ref.shape) == 2:
            return ref[...]
        dil, rows = ref.shape[1], ref.shape[2]
        scr = res_scr.pop()
        n_c = scr.shape[0]
        for r in range(dil):
            for c in range(n_c):
                scr[c, pl.ds(r, rows, stride=dil), :] = ref[0, r, :, c * LANES:(c + 1) * LANES]
        return jnp.concatenate([scr[c] for c in range(n_c)], axis=-1)

    x = _row_tile(x_refs, first_blocks)
    h = _rms(x, g1_ref[...]).astype(jnp.bfloat16)
    l0, l1, l2 = load(ls0), load(ls1), load(ls2)
    lm = jnp.maximum(jnp.maximum(l0, l1), l2)
    e0, e1, e2 = jnp.exp(l0 - lm), jnp.exp(l1 - lm), jnp.exp(l2 - lm)
    den = e0 + e1 + e2
    ob = ((e0 / den) * load(ob0) + (e1 / den) * load(ob1) + (e2 / den) * load(ob2)).astype(jnp.bfloat16)
    branches = ((oa_ref[...], wa_ref), (ob, wb_ref), (oc_ref[...], wc_ref), (od_ref[...], wd_ref))
    merged = None
    for i, (o, w_ref) in enumerate(branches):
        gate = _sigmoid(_dot(h, wg_ref[:, i * D_MODEL:(i + 1) * D_MODEL]))
        term = gate * _dot(o, w_ref[...])
        merged = term if merged is None else merged + term
    out_ref[...] = x + _dot(merged.astype(jnp.bfloat16), wo_ref[...])


def _merge_call(xs, lw, oa, obs, lses, oc, od):
    rows = [x.shape[0] for x in xs]
    t = sum(rows)
    tm = TM_MERGE

    def tile(w):
        return pl.BlockSpec((tm, w), lambda i: (i, 0))

    n_pos = SEQ // tm

    def band_tile(dil):
        if dil == 1:
            return tile(B_W)
        return pl.BlockSpec((1, dil, tm // dil, B_W), lambda i: (i // n_pos, 0, i % n_pos, 0))

    b_specs = [band_tile(dil) for _, dil in B_PATTERNS]
    in_specs = _row_specs(rows, tm, D_MODEL) + [
                _const_spec((1, D_MODEL)), _const_spec((D_MODEL, N_BRANCH * D_MODEL)),
                tile(512)] + b_specs + b_specs + [tile(512), tile(512),
                _const_spec((512, D_MODEL)), _const_spec((256, D_MODEL)), _const_spec((512, D_MODEL)),
                _const_spec((512, D_MODEL)), _const_spec((D_MODEL, D_MODEL))]
    n_res = 2 * sum(1 for _, dil in B_PATTERNS if dil > 1)
    return pl.pallas_call(
        functools.partial(_merge_kernel, n_x=len(xs), first_blocks=rows[0] // tm),
        grid=(t // tm,), in_specs=in_specs, out_specs=tile(D_MODEL),
        out_shape=jax.ShapeDtypeStruct((t, D_MODEL), jnp.float32),
        scratch_shapes=[pltpu.VMEM((B_W // LANES, tm, LANES), jnp.float32)] * n_res,
        compiler_params=_params(1), name="merge",
    )(*xs, lw["g1"], lw["w_gate"], oa, *obs, *lses, oc, od,
      lw["w_br_a"], lw["w_br_b"], lw["w_br_c"], lw["w_br_d"], lw["w_o"])


def _ffn_kernel(x_ref, g2_ref, wg_ref, wu_ref, wd_ref, *out_refs, first_blocks):
    x = x_ref[...]
    hf = _rms(x, g2_ref[...]).astype(jnp.bfloat16)
    acc = x
    for off, width in FFN_CHUNKS:
        a = _dot(hf, wg_ref[:, off:off + width])
        u = _dot(hf, wu_ref[:, off:off + width])
        act = (a * _sigmoid(a) * u).astype(jnp.bfloat16)
        acc = acc + _dot(act, wd_ref[off:off + width, :])
    if len(out_refs) == 1:
        out_refs[0][...] = acc
    else:
        @pl.when(pl.program_id(0) < first_blocks)
        def _():
            out_refs[0][...] = acc

        @pl.when(pl.program_id(0) >= first_blocks)
        def _():
            out_refs[1][...] = acc


def _ffn_call(x, lw, out_rows):
    t = x.shape[0]
    tm = TM_FFN
    tile = pl.BlockSpec((tm, D_MODEL), lambda i: (i, 0))
    return pl.pallas_call(
        functools.partial(_ffn_kernel, first_blocks=out_rows[0] // tm), grid=(t // tm,),
        in_specs=[tile, _const_spec((1, D_MODEL)), _const_spec((D_MODEL, D_FF)),
                  _const_spec((D_MODEL, D_FF)), _const_spec((D_FF, D_MODEL))],
        out_specs=_row_specs(out_rows, tm, D_MODEL),
        out_shape=[jax.ShapeDtypeStruct((r, D_MODEL), jnp.float32) for r in out_rows],
        compiler_params=_params(1), name="ffn",
    )(x, lw["g2"], lw["w_ffn_gate"], lw["w_ffn_up"], lw["w_ffn_down"])


def _tables():
    pos = jnp.arange(SEQ, dtype=jnp.float32)[:, None]
    lane = jnp.arange(LANES)
    inv64 = jnp.power(ROPE_THETA, -jnp.arange(32, dtype=jnp.float32) / 32)
    ang64 = pos * inv64[lane % 32][None, :]
    sign64 = jnp.where((lane % 64) < 32, -1.0, 1.0)[None, :]
    cos64 = jnp.cos(ang64)
    sin64 = jnp.sin(ang64) * sign64
    invc = jnp.power(ROPE_THETA, -jnp.arange(16, dtype=jnp.float32) / 16)
    angc = pos * invc[lane % 16][None, :]
    is_rope = ((lane >= C_NOPE) & (lane < C_NOPE + C_ROPE))[None, :]
    signc = jnp.where(lane < C_NOPE + C_ROPE // 2, -1.0, 1.0)[None, :]
    cosc = jnp.where(is_rope, jnp.cos(angc), 1.0)
    sinc = jnp.where(is_rope, jnp.sin(angc) * signc, 0.0)
    cs = jnp.stack([cos64, sin64, cosc, sinc]).astype(jnp.float32)
    idx = np.arange(MXU_N)
    bd = np.stack([(idx[:, None] // 64) == (idx[None, :] // 64),
                   (idx[:, None] // 128) == (idx[None, :] // 128)]).astype(np.float32)
    return {"cs": cs, "bd": jnp.asarray(bd, dtype=jnp.bfloat16)}


def _layer_weights(l, p):
    bf = jnp.bfloat16
    f32 = jnp.float32
    cols = jnp.split(p["w_in"][l], SPLIT_IDX, axis=-1)
    zeros = lambda n: jnp.zeros((D_MODEL, n), f32)
    dk, dv = cols[16], cols[17]
    w_in = jnp.concatenate(
        list(cols[0:12]) + [cols[12], cols[13], zeros(64), cols[14], zeros(32), cols[15],
                            dk[:, :64], dk[:, :64], dk[:, 64:], dk[:, 64:],
                            dv[:, :64], dv[:, :64], dv[:, 64:], dv[:, 64:]], axis=-1).astype(bf)
    ones = lambda n: jnp.ones((n,), f32)
    qs = A_HD ** -0.5 * LOG2E
    grow = jnp.concatenate(
        [jnp.tile(p["a_qnorm_g"][l], 8) * qs, jnp.tile(p["a_knorm_g"][l], 8), ones(512)]
        + sum([[jnp.tile(p["b_qnorm_g"][l, g], 4) * qs, jnp.tile(p["b_knorm_g"][l, g], 4), ones(256)]
               for g in range(3)], [])
        + [ones(512), jnp.tile(p["d_qnorm_g"][l], 8) * qs, jnp.tile(p["d_knorm_g"][l], 4), ones(256)]
    )[None, :].astype(f32)
    wuq = p["c_w_uq"][l].reshape(C_Q_RANK, C_HEADS, C_NOPE + C_ROPE)
    wuq = jnp.pad(wuq, ((0, 0), (0, 0), (0, 32))).reshape(C_Q_RANK, C_HEADS * LANES).astype(bf)
    wukv = p["c_w_ukv"][l].reshape(C_KV_RANK, C_HEADS, C_NOPE + C_VD)
    wuk = jnp.pad(wukv[:, :, :C_NOPE], ((0, 0), (0, 0), (0, 64))).reshape(C_KV_RANK, C_HEADS * LANES).astype(bf)
    wuv = wukv[:, :, C_NOPE:].reshape(C_KV_RANK, C_HEADS * C_VD).astype(bf)
    cscale = (C_NOPE + C_ROPE) ** -0.5 * LOG2E
    pad32 = lambda g: jnp.tile(jnp.pad(g, (0, 32)), C_HEADS)[None, :].astype(f32)
    slack = (1.0 + 2.0 ** -8) ** 2
    bound_a = (A_HD * qs * slack * jnp.max(jnp.abs(p["a_qnorm_g"][l])) * jnp.max(jnp.abs(p["a_knorm_g"][l])))
    bound_c = ((C_NOPE + C_ROPE) * cscale * slack
               * jnp.max(jnp.abs(p["c_qnorm_g"][l])) * jnp.max(jnp.abs(p["c_knorm_g"][l])))
    bound_b = [(B_HD * qs * slack * jnp.max(jnp.abs(p["b_qnorm_g"][l, g]))
                * jnp.max(jnp.abs(p["b_knorm_g"][l, g]))).reshape(1).astype(f32) for g in range(len(B_PATTERNS))]
    bound_d = D_HD * qs * slack * jnp.max(jnp.abs(p["d_qnorm_g"][l])) * jnp.max(jnp.abs(p["d_knorm_g"][l]))
    return {
        "bound_a": bound_a.reshape(1).astype(f32), "bound_c": bound_c.reshape(1).astype(f32),
        "bound_b": bound_b, "bound_d": bound_d.reshape(1).astype(f32),
        "g1": p["norm1_g"][l][None, :], "w_in": w_in, "grow": grow,
        "qag": p["c_qa_norm_g"][l][None, :], "kvag": p["c_kva_norm_g"][l][None, :],
        "wuq": wuq, "wuk": wuk, "wuv": wuv,
        "gqc": pad32(p["c_qnorm_g"][l]) * cscale, "gkc": pad32(p["c_knorm_g"][l]),
        "a_lambda": p["a_lambda"][l], "subln": p["a_subln_g"][l][None, :],
        "d_sink": p["d_sink"][l],
        "w_gate": p["w_gate"][l].astype(bf),
        "w_br_a": p["w_br_a"][l].astype(bf), "w_br_b": p["w_br_b"][l].astype(bf),
        "w_br_c": p["w_br_c"][l].astype(bf), "w_br_d": p["w_br_d"][l].astype(bf),
        "w_o": p["w_o"][l].astype(bf), "g2": p["norm2_g"][l][None, :],
        "w_ffn_gate": p["w_ffn_gate"][l].astype(bf), "w_ffn_up": p["w_ffn_up"][l].astype(bf),
        "w_ffn_down": p["w_ffn_down"][l].astype(bf),
    }


def _layer(xs, l, lw, tables, out_rows):
    (qa, ka, va, qb0, kb0, vb0, qb1, kb1, vb1, qb2, kb2, vb2, qc, kc, vc, qd, kd, vd) = _proj_call(xs, lw, tables)
    oa = _full_attention(functools.partial(_attn_a_call, qa, ka, va, lw["a_lambda"], lw["subln"],
                                           lambda_init(l), lw["bound_a"]), lw["bound_a"])
    obs, lses = [], []
    for g, (qg, kg, vg) in enumerate(((qb0, kb0, vb0), (qb1, kb1, vb1), (qb2, kb2, vb2))):
        window, dil = B_PATTERNS[g]
        res_shape = qg.shape
        qg, kg, vg = (a.reshape(-1, B_W) for a in (qg, kg, vg))
        bound = lw["bound_b"][g]
        o, lse = _full_attention(functools.partial(
            _band_call, qg, kg, vg, bound, heads=B_HEAD_SPECS, bw=B_W, bq=128, hw=window // (2 * dil),
            seg_len=SEQ // dil, with_lse=True, out_dtype=jnp.float32, name=f"band_b{g}"), bound)
        obs.append(o.reshape(res_shape))
        lses.append(lse.reshape(res_shape))
    oc = _full_attention(functools.partial(_attn_c_call, qc, kc, vc, lw["bound_c"]), lw["bound_c"])
    od = _full_attention(functools.partial(
        _band_call, qd, kd, vd, lw["bound_d"], heads=D_HEAD_SPECS, bw=LANES, bq=128, hw=D_WIN, seg_len=SEQ,
        sink=lw["d_sink"], with_lse=False, out_dtype=jnp.bfloat16, name="band_d"), lw["bound_d"])[0]
    x = _merge_call(xs, lw, oa, obs, lses, oc, od)
    return _ffn_call(x, lw, out_rows)


def kernel(x_prompt, x_sample, norm1_g, w_in, w_gate, a_qnorm_g, a_knorm_g, a_lambda, a_subln_g, b_qnorm_g, b_knorm_g, c_qa_norm_g, c_kva_norm_g, c_w_uq, c_w_ukv, c_qnorm_g, c_knorm_g, d_qnorm_g, d_knorm_g, d_sink, w_br_a, w_br_b, w_br_c, w_br_d, w_o, norm2_g, w_ffn_gate, w_ffn_up, w_ffn_down):
    p = dict(norm1_g=norm1_g, w_in=w_in, w_gate=w_gate, a_qnorm_g=a_qnorm_g, a_knorm_g=a_knorm_g,
             a_lambda=a_lambda, a_subln_g=a_subln_g, b_qnorm_g=b_qnorm_g, b_knorm_g=b_knorm_g,
             c_qa_norm_g=c_qa_norm_g, c_kva_norm_g=c_kva_norm_g, c_w_uq=c_w_uq, c_w_ukv=c_w_ukv,
             c_qnorm_g=c_qnorm_g, c_knorm_g=c_knorm_g, d_qnorm_g=d_qnorm_g, d_knorm_g=d_knorm_g,
             d_sink=d_sink, w_br_a=w_br_a, w_br_b=w_br_b, w_br_c=w_br_c, w_br_d=w_br_d, w_o=w_o,
             norm2_g=norm2_g, w_ffn_gate=w_ffn_gate, w_ffn_up=w_ffn_up, w_ffn_down=w_ffn_down)
    xs = [x_prompt.reshape(-1, D_MODEL), x_sample.reshape(-1, D_MODEL)]
    rows = [x.shape[0] for x in xs]
    tables = _tables()
    for l in range(DEPTH):
        last = l == DEPTH - 1
        xs = _layer(xs, l, _layer_weights(l, p), tables, rows if last else [sum(rows)])
    return (xs[0].reshape(x_prompt.shape), xs[1].reshape(x_sample.shape))
```

```python
import functools
import math

import jax
import jax.numpy as jnp
import numpy as np
from jax import lax
from jax.experimental import pallas as pl
from jax.experimental.pallas import tpu as pltpu

D_MODEL = 1024
SEQ = 4096
DEPTH = 2
ROPE_THETA = 10000.0
EPS = 1e-6
NEG_INF = -1e30
N_BRANCH = 4
LOG2E = 1.4426950408889634
LN2 = 0.6931471805599453

A_HEADS = 4
A_HD = 64
B_PATTERNS = ((128, 1), (512, 4), (2048, 16))
B_HEADS = 4
B_HD = 64
B_W = B_HEADS * B_HD
C_HEADS = 8
C_Q_RANK = 256
C_KV_RANK = 128
C_NOPE = 64
C_ROPE = 32
C_VD = 64
D_QHEADS = 8
D_KVHEADS = 2
D_HD = 64
D_WIN = 128
D_FF = -(-8 * D_MODEL // (3 * 256)) * 256

IN_SIZES = (512, 512, 512) + (B_W,) * 9 + (C_Q_RANK, C_KV_RANK, C_ROPE, 512, 128, 128)
SPLIT_IDX = tuple(int(i) for i in np.cumsum(IN_SIZES)[:-1])

LANES = 128
MXU_N = 256

OFF_AQ, OFF_AK, OFF_AV = 0, 512, 1024
OFF_B = 1536
OFF_CQ = 3840
OFF_CKV = 4096
OFF_KR = 4224
OFF_DQ = 4352
OFF_DK = 4864
OFF_DV = 5120
N_PROJ = 5376

VMEM_LIMIT = 56 * 1024 * 1024

TM_PROJ = 512
TQ_A = 256
TQ_C = 256
HEADS_PER_STEP_A = 4
HEADS_PER_STEP_C = 4
BAND_UNROLL = 8
MAX_SAFE_BOUND = 50.0
TM_MERGE = 512
TM_FFN = 512
FFN_CHUNKS = ((0, 1024), (1024, 1024), (2048, 768))


def lambda_init(layer):
    return 0.8 - 0.6 * math.exp(-0.3 * layer)


def _const_spec(shape):
    nd = len(shape)
    return pl.BlockSpec(shape, lambda *_: (0,) * nd, pipeline_mode=pl.Buffered(1))


def _params(n_grid):
    return pltpu.CompilerParams(dimension_semantics=("arbitrary",) * n_grid,
                                vmem_limit_bytes=VMEM_LIMIT)


def _dot(a, b):
    return jnp.dot(a, b, preferred_element_type=jnp.float32)


def _dot_nt(a, b):
    return lax.dot_general(a, b, (((1,), (1,)), ((), ())), preferred_element_type=jnp.float32)


def _rms(x, g):
    ms = jnp.mean(x * x, axis=-1, keepdims=True)
    return x * lax.rsqrt(ms + EPS) * g


def _row_specs(row_counts, tm, width):
    if len(row_counts) == 1:
        return [pl.BlockSpec((tm, width), lambda i: (i, 0))]
    nb = row_counts[0] // tm
    return [pl.BlockSpec((tm, width), lambda i: (jnp.minimum(i, nb - 1), 0)),
            pl.BlockSpec((tm, width), lambda i: (jnp.maximum(i - nb, 0), 0))]


def _row_tile(refs, first_blocks):
    if len(refs) == 1:
        return refs[0][...]
    return jnp.where(pl.program_id(0) < first_blocks, refs[0][...], refs[1][...])


def _group_sumsq(p, bd):
    outs = []
    for c in range(p.shape[1] // MXU_N):
        pc = p[:, c * MXU_N:(c + 1) * MXU_N]
        outs.append(_dot((pc * pc).astype(jnp.bfloat16), bd))
    return outs[0] if len(outs) == 1 else jnp.concatenate(outs, axis=-1)


def _rope_chunks(y, cos, sin, first_half, shift):
    outs = []
    for c in range(y.shape[1] // LANES):
        yc = y[:, c * LANES:(c + 1) * LANES]
        sw = jnp.where(first_half, pltpu.roll(yc, LANES - shift, 1), pltpu.roll(yc, shift, 1))
        outs.append(yc * cos + sw * sin)
    return outs[0] if len(outs) == 1 else jnp.concatenate(outs, axis=-1)


def _proj_kernel(*refs, n_x, first_blocks):
    x_refs, refs = refs[:n_x], refs[n_x:]
    (g1_ref, w_ref, grow_ref, bd_ref, cs_ref, qag_ref, kvag_ref,
     wuq_ref, wuk_ref, wuv_ref, gqc_ref, gkc_ref,
     qa, ka, va, qb0, kb0, vb0, qb1, kb1, vb1, qb2, kb2, vb2,
     qc, kc, vc, qd, kd, vd) = refs[:30]
    res_scr = list(refs[30:])
    tm = x_refs[0].shape[0]
    h = _rms(_row_tile(x_refs, first_blocks), g1_ref[...]).astype(jnp.bfloat16)
    bd64 = bd_ref[0]
    bd128 = bd_ref[1]
    cos64, sin64, cosc, sinc = cs_ref[0], cs_ref[1], cs_ref[2], cs_ref[3]
    lane = lax.broadcasted_iota(jnp.int32, (tm, LANES), 1)
    first64 = (lane & 63) < 32
    firstc = lane < (C_NOPE + C_ROPE // 2)

    def proj(off, width):
        return _dot(h, w_ref[:, off:off + width])

    def store(out_ref, val):
        if len(out_ref.shape) == 2:
            out_ref[...] = val.astype(out_ref.dtype)
            return
        dil, rows = out_ref.shape[1], out_ref.shape[2]
        scr = res_scr.pop()
        n_c = scr.shape[0]
        for c in range(n_c):
            scr[c] = val[:, c * LANES:(c + 1) * LANES]
        for r in range(dil):
            out_ref[0, r] = jnp.concatenate(
                [scr[c, pl.ds(r, rows, stride=dil), :] for c in range(n_c)], axis=-1).astype(out_ref.dtype)

    def norm_rope64(p, off, out_ref):
        ss = _group_sumsq(p, bd64)
        y = p * lax.rsqrt(ss * (1.0 / 64) + EPS) * grow_ref[:, off:off + p.shape[1]]
        store(out_ref, _rope_chunks(y, cos64, sin64, first64, 32))

    b_outs = ((qb0, kb0, vb0), (qb1, kb1, vb1), (qb2, kb2, vb2))
    sections = [(OFF_AQ, 512, qa, True), (OFF_AK, 512, ka, True)]
    for g, (qo, ko, _) in enumerate(b_outs):
        sections += [(OFF_B + g * 768, 256, qo, True), (OFF_B + g * 768 + 256, 256, ko, True)]
    sections += [(OFF_DQ, 512, qd, True), (OFF_DK, 256, kd, True), (OFF_AV, 512, va, False)]
    sections += [(OFF_B + g * 768 + 512, 256, vo, False) for g, (_, _, vo) in enumerate(b_outs)]
    sections += [(OFF_DV, 256, vd, False)]

    cq_raw = proj(OFF_CQ, C_Q_RANK)
    ckv_raw = proj(OFF_CKV, C_KV_RANK)
    kr = proj(OFF_KR, LANES)
    cqn = _rms(cq_raw, qag_ref[...]).astype(jnp.bfloat16)
    ckvn = _rms(ckv_raw, kvag_ref[...]).astype(jnp.bfloat16)
    qfull = _dot(cqn, wuq_ref[...])
    kfull = _dot(ckvn, wuk_ref[...]) + jnp.concatenate([kr] * C_HEADS, axis=-1)
    vc[...] = _dot(ckvn, wuv_ref[...]).astype(vc.dtype)
    p_next = proj(sections[0][0], sections[0][1])
    qn = qfull * lax.rsqrt(_group_sumsq(qfull, bd128) * (1.0 / 96) + EPS) * gqc_ref[...]
    kn = kfull * lax.rsqrt(_group_sumsq(kfull, bd128) * (1.0 / 96) + EPS) * gkc_ref[...]
    qc[...] = _rope_chunks(qn, cosc, sinc, firstc, C_ROPE // 2).astype(qc.dtype)
    kc[...] = _rope_chunks(kn, cosc, sinc, firstc, C_ROPE // 2).astype(kc.dtype)

    for n, (off, _, out_ref, roped) in enumerate(sections):
        p = p_next
        if n + 1 < len(sections):
            p_next = proj(sections[n + 1][0], sections[n + 1][1])
        if roped:
            norm_rope64(p, off, out_ref)
        else:
            store(out_ref, p)


def _proj_call(xs, lw, tables):
    rows = [x.shape[0] for x in xs]
    t = sum(rows)
    tm = TM_PROJ
    n_pos = SEQ // tm
    widths = (512, 512, 512) + (256,) * 9 + (1024, 1024, 512, 512, 256, 256)
    out_shape = [jax.ShapeDtypeStruct((t, w), jnp.bfloat16) for w in widths]
    out_specs = [pl.BlockSpec((tm, w), lambda i: (i, 0)) for w in widths]
    for g, (_, dil) in enumerate(B_PATTERNS):
        if dil > 1:
            for j in range(3 + 3 * g, 6 + 3 * g):
                out_shape[j] = jax.ShapeDtypeStruct((t // SEQ, dil, SEQ // dil, B_W), jnp.bfloat16)
                out_specs[j] = pl.BlockSpec((1, dil, tm // dil, B_W), lambda i: (i // n_pos, 0, i % n_pos, 0))
    n_res = 3 * sum(1 for _, dil in B_PATTERNS if dil > 1)
    in_specs = _row_specs(rows, tm, D_MODEL) + [
        _const_spec((1, D_MODEL)),
        _const_spec((D_MODEL, N_PROJ)),
        _const_spec((1, N_PROJ)),
        _const_spec((2, MXU_N, MXU_N)),
        pl.BlockSpec((4, tm, LANES), lambda i: (0, i % n_pos, 0)),
        _const_spec((1, C_Q_RANK)),
        _const_spec((1, C_KV_RANK)),
        _const_spec((C_Q_RANK, 1024)),
        _const_spec((C_KV_RANK, 1024)),
        _const_spec((C_KV_RANK, 512)),
        _const_spec((1, 1024)),
        _const_spec((1, 1024)),
    ]
    return pl.pallas_call(
        functools.partial(_proj_kernel, n_x=len(xs), first_blocks=rows[0] // tm),
        grid=(t // tm,), in_specs=in_specs, out_specs=out_specs, out_shape=out_shape,
        scratch_shapes=[pltpu.VMEM((B_W // LANES, tm, LANES), jnp.float32)] * n_res,
        compiler_params=_params(1), name="proj",
    )(*xs, lw["g1"], lw["w_in"], lw["grow"], tables["bd"], tables["cs"], lw["qag"], lw["kvag"],
      lw["wuq"], lw["wuk"], lw["wuv"], lw["gqc"], lw["gkc"])


def _lane_fold(x, op):
    acc = x[:, :LANES]
    for t in range(1, x.shape[1] // LANES):
        acc = op(acc, x[:, t * LANES:(t + 1) * LANES])
    return acc


def _online_step(state, s, v):
    r = jnp.max(_lane_fold(s, jnp.maximum), axis=-1, keepdims=True)
    if state is None:
        e = jnp.exp2(s - r)
        return r, _lane_fold(e, jnp.add), _dot(e.astype(jnp.bfloat16), v)
    m, l, acc = state
    m_new = jnp.maximum(m, r)
    alpha = jnp.exp2(m - m_new)
    e = jnp.exp2(s - m_new)
    return m_new, alpha * l + _lane_fold(e, jnp.add), alpha * acc + _dot(e.astype(jnp.bfloat16), v)


def _online_finish(state):
    _, l, acc = state
    return acc * (1.0 / jnp.sum(l, axis=-1, keepdims=True))


def _attn_a_kernel(bound_ref, lam_ref, q_ref, k_ref, v_ref, g_ref, o_ref, *, lam_init, bounded):
    lp = lam_ref[...]
    lam = (jnp.exp(jnp.sum(lp[0:1] * lp[1:2], axis=-1, keepdims=True))
           - jnp.exp(jnp.sum(lp[2:3] * lp[3:4], axis=-1, keepdims=True)) + lam_init)
    lane = lax.broadcasted_iota(jnp.int32, (q_ref.shape[0], LANES), 1)
    n_heads = q_ref.shape[1] // LANES

    def scores(h):
        hs = slice(h * LANES, (h + 1) * LANES)
        q = q_ref[:, hs]
        k = k_ref[:, hs]
        return [_dot_nt(jnp.where((lane >= c * A_HD) & (lane < (c + 1) * A_HD), q, jnp.zeros_like(q)), k)
                for c in range(2)]

    def finish(h, o):
        o = _rms(o, g_ref[...]) * (1.0 - lam_init)
        o_ref[:, h * LANES:(h + 1) * LANES] = o.astype(o_ref.dtype)

    if bounded:
        bound = bound_ref[0]
        def exps(h):
            es, ls = [], []
            for s in scores(h):
                e = jnp.exp2(s - bound)
                ls.append(jnp.sum(_lane_fold(e, jnp.add), axis=-1, keepdims=True))
                es.append(e)
            return es, ls

        nxt = exps(0)
        for h in range(n_heads):
            (e0, e1), (l0, l1) = nxt
            if h + 1 < n_heads:
                nxt = exps(h + 1)
            w = e0 - (lam * l0 / l1) * e1
            finish(h, _dot(w.astype(jnp.bfloat16), v_ref[:, h * LANES:(h + 1) * LANES]) * (1.0 / l0))
        return

    s_next = scores(0)
    for h in range(n_heads):
        s_cur = s_next
        if h + 1 < n_heads:
            s_next = scores(h + 1)
        es, inv = [], []
        for s in s_cur:
            e = jnp.exp2(s - jnp.max(s, axis=-1, keepdims=True))
            es.append(e)
            inv.append(1.0 / jnp.sum(e, axis=-1, keepdims=True))
        w = es[0] * inv[0] - es[1] * (lam * inv[1])
        finish(h, _dot(w.astype(jnp.bfloat16), v_ref[:, h * LANES:(h + 1) * LANES]))


def _attn_a_call(q, k, v, a_lambda, subln_row, lam_init, bound, bounded):
    t = q.shape[0]
    nseq = t // SEQ
    tq = TQ_A
    nq = SEQ // tq
    w = HEADS_PER_STEP_A * LANES
    return pl.pallas_call(
        functools.partial(_attn_a_kernel, lam_init=lam_init, bounded=bounded),
        grid=(nseq, A_HEADS // HEADS_PER_STEP_A, nq),
        in_specs=[
            pl.BlockSpec(memory_space=pltpu.SMEM),
            _const_spec((4, A_HD)),
            pl.BlockSpec((tq, w), lambda s, h, i: (s * nq + i, h)),
            pl.BlockSpec((SEQ, w), lambda s, h, i: (s, h)),
            pl.BlockSpec((SEQ, w), lambda s, h, i: (s, h)),
            _const_spec((1, LANES)),
        ],
        out_specs=pl.BlockSpec((tq, w), lambda s, h, i: (s * nq + i, h)),
        out_shape=jax.ShapeDtypeStruct((t, A_HEADS * LANES), jnp.bfloat16),
        compiler_params=_params(3), name="attn_a",
    )(bound, a_lambda, q, k, v, subln_row)


def _attn_c_kernel(bound_ref, q_ref, k_ref, v_ref, o_ref, *, bounded):
    lane = lax.broadcasted_iota(jnp.int32, (o_ref.shape[0], LANES), 1)
    n_heads = q_ref.shape[1] // LANES

    def scores(h):
        hs = slice(h * LANES, (h + 1) * LANES)
        return _dot_nt(q_ref[:, hs], k_ref[:, hs])

    outs = []
    if bounded:
        bound = bound_ref[0]
        for h in range(n_heads):
            e = jnp.exp2(scores(h) - bound)
            l = jnp.sum(_lane_fold(e, jnp.add), axis=-1, keepdims=True)
            outs.append(_dot(e.astype(jnp.bfloat16), v_ref[:, (h // 2) * LANES:(h // 2 + 1) * LANES]) * (1.0 / l))
    else:
        s_next = scores(0)
        for h in range(n_heads):
            s = s_next
            if h + 1 < n_heads:
                s_next = scores(h + 1)
            v = v_ref[:, (h // 2) * LANES:(h // 2 + 1) * LANES]
            half = s.shape[0] // 2
            parts = [_online_step(None, s[r:r + half], v) for r in (0, half)]
            outs.append(_online_finish(tuple(jnp.concatenate([a, b], axis=0) for a, b in zip(*parts))))
    for j in range(n_heads // 2):
        o_ref[:, j * LANES:(j + 1) * LANES] = jnp.where(
            lane < C_VD, outs[2 * j], outs[2 * j + 1]).astype(o_ref.dtype)


def _attn_c_call(q, k, v, bound, bounded):
    t = q.shape[0]
    nseq = t // SEQ
    tq = TQ_C
    nq = SEQ // tq
    hps = HEADS_PER_STEP_C
    return pl.pallas_call(
        functools.partial(_attn_c_kernel, bounded=bounded),
        grid=(nseq, C_HEADS // hps, nq),
        in_specs=[
            pl.BlockSpec(memory_space=pltpu.SMEM),
            pl.BlockSpec((tq, hps * LANES), lambda s, j, i: (s * nq + i, j)),
            pl.BlockSpec((SEQ, hps * LANES), lambda s, j, i: (s, j)),
            pl.BlockSpec((SEQ, hps * C_VD), lambda s, j, i: (s, j)),
        ],
        out_specs=pl.BlockSpec((tq, hps * C_VD), lambda s, j, i: (s * nq + i, j)),
        out_shape=jax.ShapeDtypeStruct((t, C_HEADS * C_VD), jnp.bfloat16),
        compiler_params=_params(3), name="attn_c",
    )(bound, q, k, v)


def _full_attention(call, bound):
    return lax.cond(bound[0] <= MAX_SAFE_BOUND, lambda: call(True), lambda: call(False))


def _band_kernel(bound_ref, *refs, heads, bw, bq, hw, seg_len, with_sink, with_lse, bounded):
    if with_sink:
        sink_ref, q_ref, k_ref, v_ref = refs[:4]
        outs = refs[4:]
    else:
        q_ref, k_ref, v_ref = refs[:3]
        outs = refs[3:]
    o_ref = outs[0]
    lse_ref = outs[1] if with_lse else None
    ch = q_ref.shape[0]
    win = bq + 2 * hw
    shift = int(math.log2(seg_len))
    lane = lax.broadcasted_iota(jnp.int32, (bq, bw), 1)
    lane128 = lax.broadcasted_iota(jnp.int32, (bq, LANES), 1)
    assert seg_len % bq == 0 and seg_len >= win
    row_minus_col = (lax.broadcasted_iota(jnp.int32, (bq, win), 0)
                     - lax.broadcasted_iota(jnp.int32, (bq, win), 1))
    k_offs = sorted({hd[2] for hd in heads})

    def body(i, carry):
        q0 = pl.multiple_of(i * bq, bq)
        seg_lo = (q0 >> shift) << shift
        ws = pl.multiple_of(jnp.clip(q0 - hw, seg_lo, seg_lo + seg_len - win), hw)
        d = row_minus_col + (q0 - ws)
        bias = jnp.where((d <= hw) & (d >= -hw), 0.0, NEG_INF)
        qb = q_ref[pl.ds(q0, bq), :]
        kw = k_ref[pl.ds(ws, win), :]
        vw = v_ref[pl.ds(ws, win), :]
        acc, lacc = {}, {}
        for ko in k_offs:
            group = [hd for hd in heads if hd[2] == ko]
            qs = jnp.concatenate(
                [jnp.where((lane >= mo) & (lane < mo + 64), qb[:, qo:qo + bw], jnp.zeros((bq, bw), qb.dtype))
                 for qo, mo, _, _ in group], axis=0)
            s_all = _dot_nt(qs, kw[:, ko:ko + bw])
            es, ms, ls = [], [], []
            for n, (_, _, _, hid) in enumerate(group):
                s = s_all[n * bq:(n + 1) * bq] + bias
                m = (jnp.full((bq, 1), bound_ref[0], jnp.float32) if bounded
                     else jnp.max(s, axis=-1, keepdims=True))
                if with_sink:
                    sk = sink_ref[hid] * LOG2E
                    m = jnp.maximum(m, sk)
                e = jnp.exp2(s - m)
                l = jnp.sum(e, axis=-1, keepdims=True)
                if with_sink:
                    l = l + jnp.exp2(sk - m)
                es.append(e.astype(jnp.bfloat16))
                ms.append(m)
                ls.append(l)
            for half in range(bw // LANES):
                sub = [n for n, (_, mo, _, _) in enumerate(group) if mo // LANES == half]
                o_sub = _dot(jnp.concatenate([es[n] for n in sub], axis=0),
                             vw[:, ko + half * LANES:ko + (half + 1) * LANES])
                for j, n in enumerate(sub):
                    qo, mo = group[n][0], group[n][1] % LANES
                    hm = (lane128 >= mo) & (lane128 < mo + 64)
                    key = (qo, half)
                    o = o_sub[j * bq:(j + 1) * bq] * (1.0 / ls[n])
                    acc[key] = jnp.where(hm, o, acc[key]) if key in acc else o
                    if with_lse:
                        lse = jnp.broadcast_to((ms[n] + jnp.log2(ls[n])) * LN2, (bq, LANES))
                        lacc[key] = jnp.where(hm, lse, lacc[key]) if key in lacc else lse
        for (qo, half), val in acc.items():
            lanes = slice(qo + half * LANES, qo + (half + 1) * LANES)
            o_ref[pl.ds(q0, bq), lanes] = val.astype(o_ref.dtype)
            if with_lse:
                lse_ref[pl.ds(q0, bq), lanes] = lacc[(qo, half)]
        return carry

    lax.fori_loop(0, ch // bq, body, 0, unroll=BAND_UNROLL)


def _band_call(q, k, v, bound, bounded, *, heads, bw, bq, hw, seg_len, sink=None, with_lse, out_dtype, name):
    t, wq = q.shape
    wk = k.shape[1]
    ch = SEQ
    kern = functools.partial(_band_kernel, heads=heads, bw=bw, bq=bq, hw=hw, seg_len=seg_len,
                             with_sink=sink is not None, with_lse=with_lse, bounded=bounded)
    in_specs = [
        pl.BlockSpec((ch, wq), lambda i: (i, 0)),
        pl.BlockSpec((ch, wk), lambda i: (i, 0)),
        pl.BlockSpec((ch, wk), lambda i: (i, 0)),
    ]
    args = [q, k, v]
    if sink is not None:
        in_specs = [pl.BlockSpec(memory_space=pltpu.SMEM)] + in_specs
        args = [sink] + args
    in_specs = [pl.BlockSpec(memory_space=pltpu.SMEM)] + in_specs
    args = [bound] + args
    out_shape = [jax.ShapeDtypeStruct((t, wq), out_dtype)]
    out_specs = [pl.BlockSpec((ch, wq), lambda i: (i, 0))]
    if with_lse:
        out_shape.append(jax.ShapeDtypeStruct((t, wq), jnp.float32))
        out_specs.append(pl.BlockSpec((ch, wq), lambda i: (i, 0)))
    return pl.pallas_call(
        kern, grid=(t // ch,), in_specs=in_specs, out_specs=out_specs, out_shape=out_shape,
        compiler_params=_params(1), name=name,
    )(*args)


B_HEAD_SPECS = tuple((0, h * B_HD, 0, h) for h in range(B_HEADS))
D_HEAD_SPECS = tuple(((h // 2) * LANES, (h % 2) * D_HD, (h // 4) * LANES, h) for h in range(D_QHEADS))


def _sigmoid(z):
    return 1.0 / (1.0 + jnp.exp(-z))


def _merge_kernel(*refs, n_x, first_blocks):
    x_refs, refs = refs[:n_x], refs[n_x:]
    (g1_ref, wg_ref, oa_ref, ob0, ob1, ob2, ls0, ls1, ls2, oc_ref, od_ref,
     wa_ref, wb_ref, wc_ref, wd_ref, wo_ref, out_ref) = refs[:17]
    res_scr = list(refs[17:])

    def load(ref):
        if len(ref.shape) == 2:
            return ref[...]
        dil, rows = ref.shape[1], ref.shape[2]
        scr = res_scr.pop()
        n_c = scr.shape[0]
        for r in range(dil):
            for c in range(n_c):
                scr[c, pl.ds(r, rows, stride=dil), :] = ref[0, r, :, c * LANES:(c + 1) * LANES]
        return jnp.concatenate([scr[c] for c in range(n_c)], axis=-1)

    x = _row_tile(x_refs, first_blocks)
    h = _rms(x, g1_ref[...]).astype(jnp.bfloat16)
    l0, l1, l2 = load(ls0), load(ls1), load(ls2)
    lm = jnp.maximum(jnp.maximum(l0, l1), l2)
    e0, e1, e2 = jnp.exp(l0 - lm), jnp.exp(l1 - lm), jnp.exp(l2 - lm)
    den = e0 + e1 + e2
    ob = ((e0 / den) * load(ob0) + (e1 / den) * load(ob1) + (e2 / den) * load(ob2)).astype(jnp.bfloat16)
    branches = ((oa_ref[...], wa_ref), (ob, wb_ref), (oc_ref[...], wc_ref), (od_ref[...], wd_ref))
    merged = None
    for i, (o, w_ref) in enumerate(branches):
        gate = _sigmoid(_dot(h, wg_ref[:, i * D_MODEL:(i + 1) * D_MODEL]))
        term = gate * _dot(o, w_ref[...])
        merged = term if merged is None else merged + term
    out_ref[...] = x + _dot(merged.astype(jnp.bfloat16), wo_ref[...])


def _merge_call(xs, lw, oa, obs, lses, oc, od):
    rows = [x.shape[0] for x in xs]
    t = sum(rows)
    tm = TM_MERGE

    def tile(w):
        return pl.BlockSpec((tm, w), lambda i: (i, 0))

    n_pos = SEQ // tm

    def band_tile(dil):
        if dil == 1:
            return tile(B_W)
        return pl.BlockSpec((1, dil, tm // dil, B_W), lambda i: (i // n_pos, 0, i % n_pos, 0))

    b_specs = [band_tile(dil) for _, dil in B_PATTERNS]
    in_specs = _row_specs(rows, tm, D_MODEL) + [
                _const_spec((1, D_MODEL)), _const_spec((D_MODEL, N_BRANCH * D_MODEL)),
                tile(512)] + b_specs + b_specs + [tile(512), tile(512),
                _const_spec((512, D_MODEL)), _const_spec((256, D_MODEL)), _const_spec((512, D_MODEL)),
                _const_spec((512, D_MODEL)), _const_spec((D_MODEL, D_MODEL))]
    n_res = 2 * sum(1 for _, dil in B_PATTERNS if dil > 1)
    return pl.pallas_call(
        functools.partial(_merge_kernel, n_x=len(xs), first_blocks=rows[0] // tm),
        grid=(t // tm,), in_specs=in_specs, out_specs=tile(D_MODEL),
        out_shape=jax.ShapeDtypeStruct((t, D_MODEL), jnp.float32),
        scratch_shapes=[pltpu.VMEM((B_W // LANES, tm, LANES), jnp.float32)] * n_res,
        compiler_params=_params(1), name="merge",
    )(*xs, lw["g1"], lw["w_gate"], oa, *obs, *lses, oc, od,
      lw["w_br_a"], lw["w_br_b"], lw["w_br_c"], lw["w_br_d"], lw["w_o"])


def _ffn_kernel(x_ref, g2_ref, wg_ref, wu_ref, wd_ref, *out_refs, first_blocks):
    x = x_ref[...]
    hf = _rms(x, g2_ref[...]).astype(jnp.bfloat16)
    acc = x
    for off, width in FFN_CHUNKS:
        a = _dot(hf, wg_ref[:, off:off + width])
        u = _dot(hf, wu_ref[:, off:off + width])
        act = (a * _sigmoid(a) * u).astype(jnp.bfloat16)
        acc = acc + _dot(act, wd_ref[off:off + width, :])
    if len(out_refs) == 1:
        out_refs[0][...] = acc
    else:
        @pl.when(pl.program_id(0) < first_blocks)
        def _():
            out_refs[0][...] = acc

        @pl.when(pl.program_id(0) >= first_blocks)
        def _():
            out_refs[1][...] = acc


def _ffn_call(x, lw, out_rows):
    t = x.shape[0]
    tm = TM_FFN
    tile = pl.BlockSpec((tm, D_MODEL), lambda i: (i, 0))
    return pl.pallas_call(
        functools.partial(_ffn_kernel, first_blocks=out_rows[0] // tm), grid=(t // tm,),
        in_specs=[tile, _const_spec((1, D_MODEL)), _const_spec((D_MODEL, D_FF)),
                  _const_spec((D_MODEL, D_FF)), _const_spec((D_FF, D_MODEL))],
        out_specs=_row_specs(out_rows, tm, D_MODEL),
        out_shape=[jax.ShapeDtypeStruct((r, D_MODEL), jnp.float32) for r in out_rows],
        compiler_params=_params(1), name="ffn",
    )(x, lw["g2"], lw["w_ffn_gate"], lw["w_ffn_up"], lw["w_ffn_down"])


def _tables():
    pos = jnp.arange(SEQ, dtype=jnp.float32)[:, None]
    lane = jnp.arange(LANES)
    inv64 = jnp.power(ROPE_THETA, -jnp.arange(32, dtype=jnp.float32) / 32)
    ang64 = pos * inv64[lane % 32][None, :]
    sign64 = jnp.where((lane % 64) < 32, -1.0, 1.0)[None, :]
    cos64 = jnp.cos(ang64)
    sin64 = jnp.sin(ang64) * sign64
    invc = jnp.power(ROPE_THETA, -jnp.arange(16, dtype=jnp.float32) / 16)
    angc = pos * invc[lane % 16][None, :]
    is_rope = ((lane >= C_NOPE) & (lane < C_NOPE + C_ROPE))[None, :]
    signc = jnp.where(lane < C_NOPE + C_ROPE // 2, -1.0, 1.0)[None, :]
    cosc = jnp.where(is_rope, jnp.cos(angc), 1.0)
    sinc = jnp.where(is_rope, jnp.sin(angc) * signc, 0.0)
    cs = jnp.stack([cos64, sin64, cosc, sinc]).astype(jnp.float32)
    idx = np.arange(MXU_N)
    bd = np.stack([(idx[:, None] // 64) == (idx[None, :] // 64),
                   (idx[:, None] // 128) == (idx[None, :] // 128)]).astype(np.float32)
    return {"cs": cs, "bd": jnp.asarray(bd, dtype=jnp.bfloat16)}


def _layer_weights(l, p):
    bf = jnp.bfloat16
    f32 = jnp.float32
    cols = jnp.split(p["w_in"][l], SPLIT_IDX, axis=-1)
    zeros = lambda n: jnp.zeros((D_MODEL, n), f32)
    dk, dv = cols[16], cols[17]
    w_in = jnp.concatenate(
        list(cols[0:12]) + [cols[12], cols[13], zeros(64), cols[14], zeros(32), cols[15],
                            dk[:, :64], dk[:, :64], dk[:, 64:], dk[:, 64:],
                            dv[:, :64], dv[:, :64], dv[:, 64:], dv[:, 64:]], axis=-1).astype(bf)
    ones = lambda n: jnp.ones((n,), f32)
    qs = A_HD ** -0.5 * LOG2E
    grow = jnp.concatenate(
        [jnp.tile(p["a_qnorm_g"][l], 8) * qs, jnp.tile(p["a_knorm_g"][l], 8), ones(512)]
        + sum([[jnp.tile(p["b_qnorm_g"][l, g], 4) * qs, jnp.tile(p["b_knorm_g"][l, g], 4), ones(256)]
               for g in range(3)], [])
        + [ones(512), jnp.tile(p["d_qnorm_g"][l], 8) * qs, jnp.tile(p["d_knorm_g"][l], 4), ones(256)]
    )[None, :].astype(f32)
    wuq = p["c_w_uq"][l].reshape(C_Q_RANK, C_HEADS, C_NOPE + C_ROPE)
    wuq = jnp.pad(wuq, ((0, 0), (0, 0), (0, 32))).reshape(C_Q_RANK, C_HEADS * LANES).astype(bf)
    wukv = p["c_w_ukv"][l].reshape(C_KV_RANK, C_HEADS, C_NOPE + C_VD)
    wuk = jnp.pad(wukv[:, :, :C_NOPE], ((0, 0), (0, 0), (0, 64))).reshape(C_KV_RANK, C_HEADS * LANES).astype(bf)
    wuv = wukv[:, :, C_NOPE:].reshape(C_KV_RANK, C_HEADS * C_VD).astype(bf)
    cscale = (C_NOPE + C_ROPE) ** -0.5 * LOG2E
    pad32 = lambda g: jnp.tile(jnp.pad(g, (0, 32)), C_HEADS)[None, :].astype(f32)
    slack = (1.0 + 2.0 ** -8) ** 2
    bound_a = (A_HD * qs * slack * jnp.max(jnp.abs(p["a_qnorm_g"][l])) * jnp.max(jnp.abs(p["a_knorm_g"][l])))
    bound_c = ((C_NOPE + C_ROPE) * cscale * slack
               * jnp.max(jnp.abs(p["c_qnorm_g"][l])) * jnp.max(jnp.abs(p["c_knorm_g"][l])))
    bound_b = [(B_HD * qs * slack * jnp.max(jnp.abs(p["b_qnorm_g"][l, g]))
                * jnp.max(jnp.abs(p["b_knorm_g"][l, g]))).reshape(1).astype(f32) for g in range(len(B_PATTERNS))]
    bound_d = D_HD * qs * slack * jnp.max(jnp.abs(p["d_qnorm_g"][l])) * jnp.max(jnp.abs(p["d_knorm_g"][l]))
    return {
        "bound_a": bound_a.reshape(1).astype(f32), "bound_c": bound_c.reshape(1).astype(f32),
        "bound_b": bound_b, "bound_d": bound_d.reshape(1).astype(f32),
        "g1": p["norm1_g"][l][None, :], "w_in": w_in, "grow": grow,
        "qag": p["c_qa_norm_g"][l][None, :], "kvag": p["c_kva_norm_g"][l][None, :],
        "wuq": wuq, "wuk": wuk, "wuv": wuv,
        "gqc": pad32(p["c_qnorm_g"][l]) * cscale, "gkc": pad32(p["c_knorm_g"][l]),
        "a_lambda": p["a_lambda"][l], "subln": p["a_subln_g"][l][None, :],
        "d_sink": p["d_sink"][l],
        "w_gate": p["w_gate"][l].astype(bf),
        "w_br_a": p["w_br_a"][l].astype(bf), "w_br_b": p["w_br_b"][l].astype(bf),
        "w_br_c": p["w_br_c"][l].astype(bf), "w_br_d": p["w_br_d"][l].astype(bf),
        "w_o": p["w_o"][l].astype(bf), "g2": p["norm2_g"][l][None, :],
        "w_ffn_gate": p["w_ffn_gate"][l].astype(bf), "w_ffn_up": p["w_ffn_up"][l].astype(bf),
        "w_ffn_down": p["w_ffn_down"][l].astype(bf),
    }


def _layer(xs, l, lw, tables, out_rows):
    (qa, ka, va, qb0, kb0, vb0, qb1, kb1, vb1, qb2, kb2, vb2, qc, kc, vc, qd, kd, vd) = _proj_call(xs, lw, tables)
    oa = _full_attention(functools.partial(_attn_a_call, qa, ka, va, lw["a_lambda"], lw["subln"],
                                           lambda_init(l), lw["bound_a"]), lw["bound_a"])
    obs, lses = [], []
    for g, (qg, kg, vg) in enumerate(((qb0, kb0, vb0), (qb1, kb1, vb1), (qb2, kb2, vb2))):
        window, dil = B_PATTERNS[g]
        res_shape = qg.shape
        qg, kg, vg = (a.reshape(-1, B_W) for a in (qg, kg, vg))
        bound = lw["bound_b"][g]
        o, lse = _full_attention(functools.partial(
            _band_call, qg, kg, vg, bound, heads=B_HEAD_SPECS, bw=B_W, bq=128, hw=window // (2 * dil),
            seg_len=SEQ // dil, with_lse=True, out_dtype=jnp.float32, name=f"band_b{g}"), bound)
        obs.append(o.reshape(res_shape))
        lses.append(lse.reshape(res_shape))
    oc = _full_attention(functools.partial(_attn_c_call, qc, kc, vc, lw["bound_c"]), lw["bound_c"])
    od = _full_attention(functools.partial(
        _band_call, qd, kd, vd, lw["bound_d"], heads=D_HEAD_SPECS, bw=LANES, bq=128, hw=D_WIN, seg_len=SEQ,
        sink=lw["d_sink"], with_lse=False, out_dtype=jnp.bfloat16, name="band_d"), lw["bound_d"])[0]
    x = _merge_call(xs, lw, oa, obs, lses, oc, od)
    return _ffn_call(x, lw, out_rows)


def kernel(x_prompt, x_sample, norm1_g, w_in, w_gate, a_qnorm_g, a_knorm_g, a_lambda, a_subln_g, b_qnorm_g, b_knorm_g, c_qa_norm_g, c_kva_norm_g, c_w_uq, c_w_ukv, c_qnorm_g, c_knorm_g, d_qnorm_g, d_knorm_g, d_sink, w_br_a, w_br_b, w_br_c, w_br_d, w_o, norm2_g, w_ffn_gate, w_ffn_up, w_ffn_down):
    p = dict(norm1_g=norm1_g, w_in=w_in, w_gate=w_gate, a_qnorm_g=a_qnorm_g, a_knorm_g=a_knorm_g,
             a_lambda=a_lambda, a_subln_g=a_subln_g, b_qnorm_g=b_qnorm_g, b_knorm_g=b_knorm_g,
             c_qa_norm_g=c_qa_norm_g, c_kva_norm_g=c_kva_norm_g, c_w_uq=c_w_uq, c_w_ukv=c_w_ukv,
             c_qnorm_g=c_qnorm_g, c_knorm_g=c_knorm_g, d_qnorm_g=d_qnorm_g, d_knorm_g=d_knorm_g,
             d_sink=d_sink, w_br_a=w_br_a, w_br_b=w_br_b, w_br_c=w_br_c, w_br_d=w_br_d, w_o=w_o,
             norm2_g=norm2_g, w_ffn_gate=w_ffn_gate, w_ffn_up=w_ffn_up, w_ffn_down=w_ffn_down)
    xs = [x_prompt.reshape(-1, D_MODEL), x_sample.reshape(-1, D_MODEL)]
    rows = [x.shape[0] for x in xs]
    tables = _tables()
    for l in range(DEPTH):
        last = l == DEPTH - 1
        xs = _layer(xs, l, _layer_weights(l, p), tables, rows if last else [sum(rows)])
    return (xs[0].reshape(x_prompt.shape), xs[1].reshape(x_sample.shape))
```

```python
import functools
import math

import jax
import jax.numpy as jnp
import numpy as np
from jax import lax
from jax.experimental import pallas as pl
from jax.experimental.pallas import tpu as pltpu

D_MODEL = 1024
SEQ = 4096
DEPTH = 2
ROPE_THETA = 10000.0
EPS = 1e-6
NEG_INF = -1e30
N_BRANCH = 4
LOG2E = 1.4426950408889634
LN2 = 0.6931471805599453

A_HEADS = 4
A_HD = 64
B_PATTERNS = ((128, 1), (512, 4), (2048, 16))
B_HEADS = 4
B_HD = 64
B_W = B_HEADS * B_HD
C_HEADS = 8
C_Q_RANK = 256
C_KV_RANK = 128
C_NOPE = 64
C_ROPE = 32
C_VD = 64
D_QHEADS = 8
D_KVHEADS = 2
D_HD = 64
D_WIN = 128
D_FF = -(-8 * D_MODEL // (3 * 256)) * 256

IN_SIZES = (512, 512, 512) + (B_W,) * 9 + (C_Q_RANK, C_KV_RANK, C_ROPE, 512, 128, 128)
SPLIT_IDX = tuple(int(i) for i in np.cumsum(IN_SIZES)[:-1])

LANES = 128
MXU_N = 256

OFF_AQ, OFF_AK, OFF_AV = 0, 512, 1024
OFF_B = 1536
OFF_CQ = 3840
OFF_CKV = 4096
OFF_KR = 4224
OFF_DQ = 4352
OFF_DK = 4864
OFF_DV = 5120
N_PROJ = 5376

VMEM_LIMIT = 56 * 1024 * 1024

TM_PROJ = 512
TQ_A = 256
TQ_C = 256
HEADS_PER_STEP_A = 4
HEADS_PER_STEP_C = 4
BAND_UNROLL = 8
MAX_SAFE_BOUND = 50.0
TM_MERGE = 512
TM_FFN = 512
FFN_CHUNKS = ((0, 1024), (1024, 1024), (2048, 768))


def lambda_init(layer):
    return 0.8 - 0.6 * math.exp(-0.3 * layer)


def _const_spec(shape):
    nd = len(shape)
    return pl.BlockSpec(shape, lambda *_: (0,) * nd, pipeline_mode=pl.Buffered(1))


def _params(n_grid):
    return pltpu.CompilerParams(dimension_semantics=("arbitrary",) * n_grid,
                                vmem_limit_bytes=VMEM_LIMIT)


def _dot(a, b):
    return jnp.dot(a, b, preferred_element_type=jnp.float32)


def _dot_nt(a, b):
    return lax.dot_general(a, b, (((1,), (1,)), ((), ())), preferred_element_type=jnp.float32)


def _rms(x, g):
    ms = jnp.mean(x * x, axis=-1, keepdims=True)
    return x * lax.rsqrt(ms + EPS) * g


def _row_specs(row_counts, tm, width):
    if len(row_counts) == 1:
        return [pl.BlockSpec((tm, width), lambda i: (i, 0))]
    nb = row_counts[0] // tm
    return [pl.BlockSpec((tm, width), lambda i: (jnp.minimum(i, nb - 1), 0)),
            pl.BlockSpec((tm, width), lambda i: (jnp.maximum(i - nb, 0), 0))]


def _row_tile(refs, first_blocks):
    if len(refs) == 1:
        return refs[0][...]
    return jnp.where(pl.program_id(0) < first_blocks, refs[0][...], refs[1][...])


def _group_sumsq(p, bd):
    outs = []
    for c in range(p.shape[1] // MXU_N):
        pc = p[:, c * MXU_N:(c + 1) * MXU_N]
        outs.append(_dot((pc * pc).astype(jnp.bfloat16), bd))
    return outs[0] if len(outs) == 1 else jnp.concatenate(outs, axis=-1)


def _rope_chunks(y, cos, sin, first_half, shift):
    outs = []
    for c in range(y.shape[1] // LANES):
        yc = y[:, c * LANES:(c + 1) * LANES]
        sw = jnp.where(first_half, pltpu.roll(yc, LANES - shift, 1), pltpu.roll(yc, shift, 1))
        outs.append(yc * cos + sw * sin)
    return outs[0] if len(outs) == 1 else jnp.concatenate(outs, axis=-1)


def _proj_kernel(*refs, n_x, first_blocks):
    x_refs, refs = refs[:n_x], refs[n_x:]
    (g1_ref, w_ref, grow_ref, bd_ref, cs_ref, qag_ref, kvag_ref,
     wuq_ref, wuk_ref, wuv_ref, gqc_ref, gkc_ref,
     qa, ka, va, qb0, kb0, vb0, qb1, kb1, vb1, qb2, kb2, vb2,
     qc, kc, vc, qd, kd, vd) = refs[:30]
    res_scr = list(refs[30:])
    tm = x_refs[0].shape[0]
    h = _rms(_row_tile(x_refs, first_blocks), g1_ref[...]).astype(jnp.bfloat16)
    bd64 = bd_ref[0]
    bd128 = bd_ref[1]
    cos64, sin64, cosc, sinc = cs_ref[0], cs_ref[1], cs_ref[2], cs_ref[3]
    lane = lax.broadcasted_iota(jnp.int32, (tm, LANES), 1)
    first64 = (lane & 63) < 32
    firstc = lane < (C_NOPE + C_ROPE // 2)

    def proj(off, width):
        return _dot(h, w_ref[:, off:off + width])

    def store(out_ref, val):
        if len(out_ref.shape) == 2:
            out_ref[...] = val.astype(out_ref.dtype)
            return
        dil, rows = out_ref.shape[1], out_ref.shape[2]
        scr = res_scr.pop()
        n_c = scr.shape[0]
        for c in range(n_c):
            scr[c] = val[:, c * LANES:(c + 1) * LANES]
        for r in range(dil):
            out_ref[0, r] = jnp.concatenate(
                [scr[c, pl.ds(r, rows, stride=dil), :] for c in range(n_c)], axis=-1).astype(out_ref.dtype)

    def norm_rope64(p, off, out_ref):
        ss = _group_sumsq(p, bd64)
        y = p * lax.rsqrt(ss * (1.0 / 64) + EPS) * grow_ref[:, off:off + p.shape[1]]
        store(out_ref, _rope_chunks(y, cos64, sin64, first64, 32))

    b_outs = ((qb0, kb0, vb0), (qb1, kb1, vb1), (qb2, kb2, vb2))
    sections = [(OFF_AQ, 512, qa, True), (OFF_AK, 512, ka, True)]
    for g, (qo, ko, _) in enumerate(b_outs):
        sections += [(OFF_B + g * 768, 256, qo, True), (OFF_B + g * 768 + 256, 256, ko, True)]
    sections += [(OFF_DQ, 512, qd, True), (OFF_DK, 256, kd, True), (OFF_AV, 512, va, False)]
    sections += [(OFF_B + g * 768 + 512, 256, vo, False) for g, (_, _, vo) in enumerate(b_outs)]
    sections += [(OFF_DV, 256, vd, False)]

    cq_raw = proj(OFF_CQ, C_Q_RANK)
    ckv_raw = proj(OFF_CKV, C_KV_RANK)
    kr = proj(OFF_KR, LANES)
    cqn = _rms(cq_raw, qag_ref[...]).astype(jnp.bfloat16)
    ckvn = _rms(ckv_raw, kvag_ref[...]).astype(jnp.bfloat16)
    qfull = _dot(cqn, wuq_ref[...])
    kfull = _dot(ckvn, wuk_ref[...]) + jnp.concatenate([kr] * C_HEADS, axis=-1)
    vc[...] = _dot(ckvn, wuv_ref[...]).astype(vc.dtype)
    p_next = proj(sections[0][0], sections[0][1])
    qn = qfull * lax.rsqrt(_group_sumsq(qfull, bd128) * (1.0 / 96) + EPS) * gqc_ref[...]
    kn = kfull * lax.rsqrt(_group_sumsq(kfull, bd128) * (1.0 / 96) + EPS) * gkc_ref[...]
    qc[...] = _rope_chunks(qn, cosc, sinc, firstc, C_ROPE // 2).astype(qc.dtype)
    kc[...] = _rope_chunks(kn, cosc, sinc, firstc, C_ROPE // 2).astype(kc.dtype)

    for n, (off, _, out_ref, roped) in enumerate(sections):
        p = p_next
        if n + 1 < len(sections):
            p_next = proj(sections[n + 1][0], sections[n + 1][1])
        if roped:
            norm_rope64(p, off, out_ref)
        else:
            store(out_ref, p)


def _proj_call(xs, lw, tables):
    rows = [x.shape[0] for x in xs]
    t = sum(rows)
    tm = TM_PROJ
    n_pos = SEQ // tm
    widths = (512, 512, 512) + (256,) * 9 + (1024, 1024, 512, 512, 256, 256)
    out_shape = [jax.ShapeDtypeStruct((t, w), jnp.bfloat16) for w in widths]
    out_specs = [pl.BlockSpec((tm, w), lambda i: (i, 0)) for w in widths]
    for g, (_, dil) in enumerate(B_PATTERNS):
        if dil > 1:
            for j in range(3 + 3 * g, 6 + 3 * g):
                out_shape[j] = jax.ShapeDtypeStruct((t // SEQ, dil, SEQ // dil, B_W), jnp.bfloat16)
                out_specs[j] = pl.BlockSpec((1, dil, tm // dil, B_W), lambda i: (i // n_pos, 0, i % n_pos, 0))
    n_res = 3 * sum(1 for _, dil in B_PATTERNS if dil > 1)
    in_specs = _row_specs(rows, tm, D_MODEL) + [
        _const_spec((1, D_MODEL)),
        _const_spec((D_MODEL, N_PROJ)),
        _const_spec((1, N_PROJ)),
        _const_spec((2, MXU_N, MXU_N)),
        pl.BlockSpec((4, tm, LANES), lambda i: (0, i % n_pos, 0)),
        _const_spec((1, C_Q_RANK)),
        _const_spec((1, C_KV_RANK)),
        _const_spec((C_Q_RANK, 1024)),
        _const_spec((C_KV_RANK, 1024)),
        _const_spec((C_KV_RANK, 512)),
        _const_spec((1, 1024)),
        _const_spec((1, 1024)),
    ]
    return pl.pallas_call(
        functools.partial(_proj_kernel, n_x=len(xs), first_blocks=rows[0] // tm),
        grid=(t // tm,), in_specs=in_specs, out_specs=out_specs, out_shape=out_shape,
        scratch_shapes=[pltpu.VMEM((B_W // LANES, tm, LANES), jnp.float32)] * n_res,
        compiler_params=_params(1), name="proj",
    )(*xs, lw["g1"], lw["w_in"], lw["grow"], tables["bd"], tables["cs"], lw["qag"], lw["kvag"],
      lw["wuq"], lw["wuk"], lw["wuv"], lw["gqc"], lw["gkc"])


def _lane_fold(x, op):
    acc = x[:, :LANES]
    for t in range(1, x.shape[1] // LANES):
        acc = op(acc, x[:, t * LANES:(t + 1) * LANES])
    return acc


def _online_step(state, s, v):
    r = jnp.max(_lane_fold(s, jnp.maximum), axis=-1, keepdims=True)
    if state is None:
        e = jnp.exp2(s - r)
        return r, _lane_fold(e, jnp.add), _dot(e.astype(jnp.bfloat16), v)
    m, l, acc = state
    m_new = jnp.maximum(m, r)
    alpha = jnp.exp2(m - m_new)
    e = jnp.exp2(s - m_new)
    return m_new, alpha * l + _lane_fold(e, jnp.add), alpha * acc + _dot(e.astype(jnp.bfloat16), v)


def _online_finish(state):
    _, l, acc = state
    return acc * (1.0 / jnp.sum(l, axis=-1, keepdims=True))


def _attn_a_kernel(lam_ref, q_ref, k_ref, v_ref, g_ref, o_ref, *, lam_init, bounded):
    lp = lam_ref[...]
    lam = (jnp.exp(jnp.sum(lp[0:1] * lp[1:2], axis=-1, keepdims=True))
           - jnp.exp(jnp.sum(lp[2:3] * lp[3:4], axis=-1, keepdims=True)) + lam_init)
    lane = lax.broadcasted_iota(jnp.int32, (q_ref.shape[0], LANES), 1)
    n_heads = q_ref.shape[1] // LANES

    def scores(h):
        hs = slice(h * LANES, (h + 1) * LANES)
        q = q_ref[:, hs]
        k = k_ref[:, hs]
        return [_dot_nt(jnp.where((lane >= c * A_HD) & (lane < (c + 1) * A_HD), q, jnp.zeros_like(q)), k)
                for c in range(2)]

    def finish(h, o):
        o = _rms(o, g_ref[...]) * (1.0 - lam_init)
        o_ref[:, h * LANES:(h + 1) * LANES] = o.astype(o_ref.dtype)

    if bounded:
        def exps(h):
            es, ls = [], []
            for s in scores(h):
                e = jnp.exp2(s)
                ls.append(jnp.sum(_lane_fold(e, jnp.add), axis=-1, keepdims=True))
                es.append(e)
            return es, ls

        nxt = exps(0)
        for h in range(n_heads):
            (e0, e1), (l0, l1) = nxt
            if h + 1 < n_heads:
                nxt = exps(h + 1)
            w = e0 - (lam * l0 / l1) * e1
            finish(h, _dot(w.astype(jnp.bfloat16), v_ref[:, h * LANES:(h + 1) * LANES]) * (1.0 / l0))
        return

    s_next = scores(0)
    for h in range(n_heads):
        s_cur = s_next
        if h + 1 < n_heads:
            s_next = scores(h + 1)
        es, inv = [], []
        for s in s_cur:
            e = jnp.exp2(s - jnp.max(s, axis=-1, keepdims=True))
            es.append(e)
            inv.append(1.0 / jnp.sum(e, axis=-1, keepdims=True))
        w = es[0] * inv[0] - es[1] * (lam * inv[1])
        finish(h, _dot(w.astype(jnp.bfloat16), v_ref[:, h * LANES:(h + 1) * LANES]))


def _attn_a_call(q, k, v, a_lambda, subln_row, lam_init, bounded):
    t = q.shape[0]
    nseq = t // SEQ
    tq = TQ_A
    nq = SEQ // tq
    w = HEADS_PER_STEP_A * LANES
    return pl.pallas_call(
        functools.partial(_attn_a_kernel, lam_init=lam_init, bounded=bounded),
        grid=(nseq, A_HEADS // HEADS_PER_STEP_A, nq),
        in_specs=[
            _const_spec((4, A_HD)),
            pl.BlockSpec((tq, w), lambda s, h, i: (s * nq + i, h)),
            pl.BlockSpec((SEQ, w), lambda s, h, i: (s, h)),
            pl.BlockSpec((SEQ, w), lambda s, h, i: (s, h)),
            _const_spec((1, LANES)),
        ],
        out_specs=pl.BlockSpec((tq, w), lambda s, h, i: (s * nq + i, h)),
        out_shape=jax.ShapeDtypeStruct((t, A_HEADS * LANES), jnp.bfloat16),
        compiler_params=_params(3), name="attn_a",
    )(a_lambda, q, k, v, subln_row)


def _attn_c_kernel(q_ref, k_ref, v_ref, o_ref, *, bounded):
    lane = lax.broadcasted_iota(jnp.int32, (o_ref.shape[0], LANES), 1)
    n_heads = q_ref.shape[1] // LANES

    def scores(h):
        hs = slice(h * LANES, (h + 1) * LANES)
        return _dot_nt(q_ref[:, hs], k_ref[:, hs])

    outs = []
    if bounded:
        for h in range(n_heads):
            e = jnp.exp2(scores(h))
            l = jnp.sum(_lane_fold(e, jnp.add), axis=-1, keepdims=True)
            outs.append(_dot(e.astype(jnp.bfloat16), v_ref[:, (h // 2) * LANES:(h // 2 + 1) * LANES]) * (1.0 / l))
    else:
        s_next = scores(0)
        for h in range(n_heads):
            s = s_next
            if h + 1 < n_heads:
                s_next = scores(h + 1)
            v = v_ref[:, (h // 2) * LANES:(h // 2 + 1) * LANES]
            half = s.shape[0] // 2
            parts = [_online_step(None, s[r:r + half], v) for r in (0, half)]
            outs.append(_online_finish(tuple(jnp.concatenate([a, b], axis=0) for a, b in zip(*parts))))
    for j in range(n_heads // 2):
        o_ref[:, j * LANES:(j + 1) * LANES] = jnp.where(
            lane < C_VD, outs[2 * j], outs[2 * j + 1]).astype(o_ref.dtype)


def _attn_c_call(q, k, v, bounded):
    t = q.shape[0]
    nseq = t // SEQ
    tq = TQ_C
    nq = SEQ // tq
    hps = HEADS_PER_STEP_C
    return pl.pallas_call(
        functools.partial(_attn_c_kernel, bounded=bounded),
        grid=(nseq, C_HEADS // hps, nq),
        in_specs=[
            pl.BlockSpec((tq, hps * LANES), lambda s, j, i: (s * nq + i, j)),
            pl.BlockSpec((SEQ, hps * LANES), lambda s, j, i: (s, j)),
            pl.BlockSpec((SEQ, hps * C_VD), lambda s, j, i: (s, j)),
        ],
        out_specs=pl.BlockSpec((tq, hps * C_VD), lambda s, j, i: (s * nq + i, j)),
        out_shape=jax.ShapeDtypeStruct((t, C_HEADS * C_VD), jnp.bfloat16),
        compiler_params=_params(3), name="attn_c",
    )(q, k, v)


def _full_attention(call, bound):
    return lax.cond(bound[0] <= MAX_SAFE_BOUND, lambda: call(True), lambda: call(False))


def _band_kernel(*refs, heads, bw, bq, hw, seg_len, with_sink, with_lse, bounded):
    if with_sink:
        sink_ref, q_ref, k_ref, v_ref = refs[:4]
        outs = refs[4:]
    else:
        q_ref, k_ref, v_ref = refs[:3]
        outs = refs[3:]
    o_ref = outs[0]
    lse_ref = outs[1] if with_lse else None
    ch = q_ref.shape[0]
    win = bq + 2 * hw
    shift = int(math.log2(seg_len))
    lane = lax.broadcasted_iota(jnp.int32, (bq, bw), 1)
    lane128 = lax.broadcasted_iota(jnp.int32, (bq, LANES), 1)
    assert seg_len % bq == 0 and seg_len >= win and B_HD == D_HD
    row_minus_col = (lax.broadcasted_iota(jnp.int32, (bq, win), 0)
                     - lax.broadcasted_iota(jnp.int32, (bq, win), 1))
    k_offs = sorted({hd[2] for hd in heads})

    def body(i, carry):
        q0 = pl.multiple_of(i * bq, bq)
        seg_lo = (q0 >> shift) << shift
        ws = pl.multiple_of(jnp.clip(q0 - hw, seg_lo, seg_lo + seg_len - win), hw)
        d = row_minus_col + (q0 - ws)
        bias = jnp.where((d <= hw) & (d >= -hw), 0.0, NEG_INF)
        qb = q_ref[pl.ds(q0, bq), :]
        kw = k_ref[pl.ds(ws, win), :]
        vw = v_ref[pl.ds(ws, win), :]
        acc, lacc = {}, {}
        for ko in k_offs:
            group = [hd for hd in heads if hd[2] == ko]
            qs = jnp.concatenate(
                [jnp.where((lane >= mo) & (lane < mo + B_HD), qb[:, qo:qo + bw], jnp.zeros((bq, bw), qb.dtype))
                 for qo, mo, _, _ in group], axis=0)
            s_all = _dot_nt(qs, kw[:, ko:ko + bw])
            es, ms, ls = [], [], []
            for n, (_, _, _, hid) in enumerate(group):
                s = s_all[n * bq:(n + 1) * bq] + bias
                if with_sink:
                    sk = sink_ref[hid] * LOG2E
                if bounded:
                    m = 0.0
                    e = jnp.exp2(s)
                    l = jnp.sum(e, axis=-1, keepdims=True)
                    if with_sink:
                        l = l + jnp.exp2(jnp.full((bq, 1), sk, jnp.float32))
                else:
                    m = jnp.max(s, axis=-1, keepdims=True)
                    if with_sink:
                        m = jnp.maximum(m, sk)
                    e = jnp.exp2(s - m)
                    l = jnp.sum(e, axis=-1, keepdims=True)
                    if with_sink:
                        l = l + jnp.exp2(sk - m)
                es.append(e.astype(jnp.bfloat16))
                ms.append(m)
                ls.append(l)
            for half in range(bw // LANES):
                sub = [n for n, (_, mo, _, _) in enumerate(group) if mo // LANES == half]
                o_sub = _dot(jnp.concatenate([es[n] for n in sub], axis=0),
                             vw[:, ko + half * LANES:ko + (half + 1) * LANES])
                for j, n in enumerate(sub):
                    qo, mo = group[n][0], group[n][1] % LANES
                    hm = (lane128 >= mo) & (lane128 < mo + B_HD)
                    key = (qo, half)
                    o = o_sub[j * bq:(j + 1) * bq] * (1.0 / ls[n])
                    acc[key] = jnp.where(hm, o, acc[key]) if key in acc else o
                    if with_lse:
                        lse = jnp.broadcast_to((ms[n] + jnp.log2(ls[n])) * LN2, (bq, LANES))
                        lacc[key] = jnp.where(hm, lse, lacc[key]) if key in lacc else lse
        for (qo, half), val in acc.items():
            lanes = slice(qo + half * LANES, qo + (half + 1) * LANES)
            o_ref[pl.ds(q0, bq), lanes] = val.astype(o_ref.dtype)
            if with_lse:
                lse_ref[pl.ds(q0, bq), lanes] = lacc[(qo, half)]
        return carry

    lax.fori_loop(0, ch // bq, body, 0, unroll=BAND_UNROLL)


def _band_call(q, k, v, bounded, *, heads, bw, bq, hw, seg_len, sink=None, with_lse, out_dtype, name):
    t, wq = q.shape
    wk = k.shape[1]
    ch = SEQ
    kern = functools.partial(_band_kernel, heads=heads, bw=bw, bq=bq, hw=hw, seg_len=seg_len,
                             with_sink=sink is not None, with_lse=with_lse, bounded=bounded)
    in_specs = [
        pl.BlockSpec((ch, wq), lambda i: (i, 0)),
        pl.BlockSpec((ch, wk), lambda i: (i, 0)),
        pl.BlockSpec((ch, wk), lambda i: (i, 0)),
    ]
    args = [q, k, v]
    if sink is not None:
        in_specs = [pl.BlockSpec(memory_space=pltpu.SMEM)] + in_specs
        args = [sink] + args
    out_shape = [jax.ShapeDtypeStruct((t, wq), out_dtype)]
    out_specs = [pl.BlockSpec((ch, wq), lambda i: (i, 0))]
    if with_lse:
        out_shape.append(jax.ShapeDtypeStruct((t, wq), jnp.float32))
        out_specs.append(pl.BlockSpec((ch, wq), lambda i: (i, 0)))
    return pl.pallas_call(
        kern, grid=(t // ch,), in_specs=in_specs, out_specs=out_specs, out_shape=out_shape,
        compiler_params=_params(1), name=name,
    )(*args)


B_HEAD_SPECS = tuple((0, h * B_HD, 0, h) for h in range(B_HEADS))
D_HEAD_SPECS = tuple(((h // 2) * LANES, (h % 2) * D_HD, (h // 4) * LANES, h) for h in range(D_QHEADS))


def _sigmoid(z):
    return 1.0 / (1.0 + jnp.exp(-z))


def _merge_kernel(*refs, n_x, first_blocks):
    x_refs, refs = refs[:n_x], refs[n_x:]
    (g1_ref, wg_ref, oa_ref, ob0, ob1, ob2, ls0, ls1, ls2, oc_ref, od_ref,
     wa_ref, wb_ref, wc_ref, wd_ref, wo_ref, out_ref) = refs[:17]
    res_scr = list(refs[17:])

    def load(ref):
        if len(ref.shape) == 2:
            return ref[...]
        dil, rows = ref.shape[1], ref.shape[2]
        scr = res_scr.pop()
        n_c = scr.shape[0]
        for r in range(dil):
            for c in range(n_c):
                scr[c, pl.ds(r, rows, stride=dil), :] = ref[0, r, :, c * LANES:(c + 1) * LANES]
        return jnp.concatenate([scr[c] for c in range(n_c)], axis=-1)

    x = _row_tile(x_refs, first_blocks)
    h = _rms(x, g1_ref[...]).astype(jnp.bfloat16)
    l0, l1, l2 = load(ls0), load(ls1), load(ls2)
    lm = jnp.maximum(jnp.maximum(l0, l1), l2)
    e0, e1, e2 = jnp.exp(l0 - lm), jnp.exp(l1 - lm), jnp.exp(l2 - lm)
    den = e0 + e1 + e2
    ob = ((e0 / den) * load(ob0) + (e1 / den) * load(ob1) + (e2 / den) * load(ob2)).astype(jnp.bfloat16)
    branches = ((oa_ref[...], wa_ref), (ob, wb_ref), (oc_ref[...], wc_ref), (od_ref[...], wd_ref))
    merged = None
    for i, (o, w_ref) in enumerate(branches):
        gate = _sigmoid(_dot(h, wg_ref[:, i * D_MODEL:(i + 1) * D_MODEL]))
        term = gate * _dot(o, w_ref[...])
        merged = term if merged is None else merged + term
    out_ref[...] = x + _dot(merged.astype(jnp.bfloat16), wo_ref[...])


def _merge_call(xs, lw, oa, obs, lses, oc, od):
    rows = [x.shape[0] for x in xs]
    t = sum(rows)
    tm = TM_MERGE

    def tile(w):
        return pl.BlockSpec((tm, w), lambda i: (i, 0))

    n_pos = SEQ // tm

    def band_tile(dil):
        if dil == 1:
            return tile(B_W)
        return pl.BlockSpec((1, dil, tm // dil, B_W), lambda i: (i // n_pos, 0, i % n_pos, 0))

    b_specs = [band_tile(dil) for _, dil in B_PATTERNS]
    in_specs = _row_specs(rows, tm, D_MODEL) + [
                _const_spec((1, D_MODEL)), _const_spec((D_MODEL, N_BRANCH * D_MODEL)),
                tile(512)] + b_specs + b_specs + [tile(512), tile(512),
                _const_spec((512, D_MODEL)), _const_spec((256, D_MODEL)), _const_spec((512, D_MODEL)),
                _const_spec((512, D_MODEL)), _const_spec((D_MODEL, D_MODEL))]
    n_res = 2 * sum(1 for _, dil in B_PATTERNS if dil > 1)
    return pl.pallas_call(
        functools.partial(_merge_kernel, n_x=len(xs), first_blocks=rows[0] // tm),
        grid=(t // tm,), in_specs=in_specs, out_specs=tile(D_MODEL),
        out_shape=jax.ShapeDtypeStruct((t, D_MODEL), jnp.float32),
        scratch_shapes=[pltpu.VMEM((B_W // LANES, tm, LANES), jnp.float32)] * n_res,
        compiler_params=_params(1), name="merge",
    )(*xs, lw["g1"], lw["w_gate"], oa, *obs, *lses, oc, od,
      lw["w_br_a"], lw["w_br_b"], lw["w_br_c"], lw["w_br_d"], lw["w_o"])


def _ffn_kernel(x_ref, g2_ref, wg_ref, wu_ref, wd_ref, *out_refs, first_blocks):
    x = x_ref[...]
    hf = _rms(x, g2_ref[...]).astype(jnp.bfloat16)
    acc = x
    for off, width in FFN_CHUNKS:
        a = _dot(hf, wg_ref[:, off:off + width])
        u = _dot(hf, wu_ref[:, off:off + width])
        act = (a * _sigmoid(a) * u).astype(jnp.bfloat16)
        acc = acc + _dot(act, wd_ref[off:off + width, :])
    if len(out_refs) == 1:
        out_refs[0][...] = acc
    else:
        @pl.when(pl.program_id(0) < first_blocks)
        def _():
            out_refs[0][...] = acc

        @pl.when(pl.program_id(0) >= first_blocks)
        def _():
            out_refs[1][...] = acc


def _ffn_call(x, lw, out_rows):
    t = x.shape[0]
    tm = TM_FFN
    tile = pl.BlockSpec((tm, D_MODEL), lambda i: (i, 0))
    return pl.pallas_call(
        functools.partial(_ffn_kernel, first_blocks=out_rows[0] // tm), grid=(t // tm,),
        in_specs=[tile, _const_spec((1, D_MODEL)), _const_spec((D_MODEL, D_FF)),
                  _const_spec((D_MODEL, D_FF)), _const_spec((D_FF, D_MODEL))],
        out_specs=_row_specs(out_rows, tm, D_MODEL),
        out_shape=[jax.ShapeDtypeStruct((r, D_MODEL), jnp.float32) for r in out_rows],
        compiler_params=_params(1), name="ffn",
    )(x, lw["g2"], lw["w_ffn_gate"], lw["w_ffn_up"], lw["w_ffn_down"])


def _tables():
    pos = jnp.arange(SEQ, dtype=jnp.float32)[:, None]
    lane = jnp.arange(LANES)
    inv64 = jnp.power(ROPE_THETA, -jnp.arange(32, dtype=jnp.float32) / 32)
    ang64 = pos * inv64[lane % 32][None, :]
    sign64 = jnp.where((lane % 64) < 32, -1.0, 1.0)[None, :]
    cos64 = jnp.cos(ang64)
    sin64 = jnp.sin(ang64) * sign64
    invc = jnp.power(ROPE_THETA, -jnp.arange(16, dtype=jnp.float32) / 16)
    angc = pos * invc[lane % 16][None, :]
    is_rope = ((lane >= C_NOPE) & (lane < C_NOPE + C_ROPE))[None, :]
    signc = jnp.where(lane < C_NOPE + C_ROPE // 2, -1.0, 1.0)[None, :]
    cosc = jnp.where(is_rope, jnp.cos(angc), 1.0)
    sinc = jnp.where(is_rope, jnp.sin(angc) * signc, 0.0)
    cs = jnp.stack([cos64, sin64, cosc, sinc]).astype(jnp.float32)
    idx = np.arange(MXU_N)
    bd = np.stack([(idx[:, None] // 64) == (idx[None, :] // 64),
                   (idx[:, None] // 128) == (idx[None, :] // 128)]).astype(np.float32)
    return {"cs": cs, "bd": jnp.asarray(bd, dtype=jnp.bfloat16)}


def _layer_weights(l, p):
    bf = jnp.bfloat16
    f32 = jnp.float32
    cols = jnp.split(p["w_in"][l], SPLIT_IDX, axis=-1)
    zeros = lambda n: jnp.zeros((D_MODEL, n), f32)
    dk, dv = cols[16], cols[17]
    w_in = jnp.concatenate(
        list(cols[0:12]) + [cols[12], cols[13], zeros(64), cols[14], zeros(32), cols[15],
                            dk[:, :64], dk[:, :64], dk[:, 64:], dk[:, 64:],
                            dv[:, :64], dv[:, :64], dv[:, 64:], dv[:, 64:]], axis=-1).astype(bf)
    ones = lambda n: jnp.ones((n,), f32)
    qs = A_HD ** -0.5 * LOG2E
    grow = jnp.concatenate(
        [jnp.tile(p["a_qnorm_g"][l], 8) * qs, jnp.tile(p["a_knorm_g"][l], 8), ones(512)]
        + sum([[jnp.tile(p["b_qnorm_g"][l, g], 4) * qs, jnp.tile(p["b_knorm_g"][l, g], 4), ones(256)]
               for g in range(3)], [])
        + [ones(512), jnp.tile(p["d_qnorm_g"][l], 8) * qs, jnp.tile(p["d_knorm_g"][l], 4), ones(256)]
    )[None, :].astype(f32)
    wuq = p["c_w_uq"][l].reshape(C_Q_RANK, C_HEADS, C_NOPE + C_ROPE)
    wuq = jnp.pad(wuq, ((0, 0), (0, 0), (0, 32))).reshape(C_Q_RANK, C_HEADS * LANES).astype(bf)
    wukv = p["c_w_ukv"][l].reshape(C_KV_RANK, C_HEADS, C_NOPE + C_VD)
    wuk = jnp.pad(wukv[:, :, :C_NOPE], ((0, 0), (0, 0), (0, 64))).reshape(C_KV_RANK, C_HEADS * LANES).astype(bf)
    wuv = wukv[:, :, C_NOPE:].reshape(C_KV_RANK, C_HEADS * C_VD).astype(bf)
    cscale = (C_NOPE + C_ROPE) ** -0.5 * LOG2E
    pad32 = lambda g: jnp.tile(jnp.pad(g, (0, 32)), C_HEADS)[None, :].astype(f32)
    slack = (1.0 + 2.0 ** -8) ** 2
    bound_a = (A_HD * qs * slack * jnp.max(jnp.abs(p["a_qnorm_g"][l])) * jnp.max(jnp.abs(p["a_knorm_g"][l])))
    bound_c = ((C_NOPE + C_ROPE) * cscale * slack
               * jnp.max(jnp.abs(p["c_qnorm_g"][l])) * jnp.max(jnp.abs(p["c_knorm_g"][l])))
    bound_b = [(B_HD * qs * slack * jnp.max(jnp.abs(p["b_qnorm_g"][l, g]))
                * jnp.max(jnp.abs(p["b_knorm_g"][l, g]))).reshape(1).astype(f32) for g in range(len(B_PATTERNS))]
    bound_d = jnp.maximum(
        D_HD * qs * slack * jnp.max(jnp.abs(p["d_qnorm_g"][l])) * jnp.max(jnp.abs(p["d_knorm_g"][l])),
        LOG2E * jnp.max(jnp.abs(p["d_sink"][l])))
    return {
        "bound_a": bound_a.reshape(1).astype(f32), "bound_c": bound_c.reshape(1).astype(f32),
        "bound_b": bound_b, "bound_d": bound_d.reshape(1).astype(f32),
        "g1": p["norm1_g"][l][None, :], "w_in": w_in, "grow": grow,
        "qag": p["c_qa_norm_g"][l][None, :], "kvag": p["c_kva_norm_g"][l][None, :],
        "wuq": wuq, "wuk": wuk, "wuv": wuv,
        "gqc": pad32(p["c_qnorm_g"][l]) * cscale, "gkc": pad32(p["c_knorm_g"][l]),
        "a_lambda": p["a_lambda"][l], "subln": p["a_subln_g"][l][None, :],
        "d_sink": p["d_sink"][l],
        "w_gate": p["w_gate"][l].astype(bf),
        "w_br_a": p["w_br_a"][l].astype(bf), "w_br_b": p["w_br_b"][l].astype(bf),
        "w_br_c": p["w_br_c"][l].astype(bf), "w_br_d": p["w_br_d"][l].astype(bf),
        "w_o": p["w_o"][l].astype(bf), "g2": p["norm2_g"][l][None, :],
        "w_ffn_gate": p["w_ffn_gate"][l].astype(bf), "w_ffn_up": p["w_ffn_up"][l].astype(bf),
        "w_ffn_down": p["w_ffn_down"][l].astype(bf),
    }


def _layer(xs, l, lw, tables, out_rows):
    (qa, ka, va, qb0, kb0, vb0, qb1, kb1, vb1, qb2, kb2, vb2, qc, kc, vc, qd, kd, vd) = _proj_call(xs, lw, tables)
    oa = _full_attention(functools.partial(_attn_a_call, qa, ka, va, lw["a_lambda"], lw["subln"],
                                           lambda_init(l)), lw["bound_a"])
    obs, lses = [], []
    for g, (qg, kg, vg) in enumerate(((qb0, kb0, vb0), (qb1, kb1, vb1), (qb2, kb2, vb2))):
        window, dil = B_PATTERNS[g]
        res_shape = qg.shape
        qg, kg, vg = (a.reshape(-1, B_W) for a in (qg, kg, vg))
        bound = lw["bound_b"][g]
        o, lse = _full_attention(functools.partial(
            _band_call, qg, kg, vg, heads=B_HEAD_SPECS, bw=B_W, bq=128, hw=window // (2 * dil),
            seg_len=SEQ // dil, with_lse=True, out_dtype=jnp.float32, name=f"band_b{g}"), bound)
        obs.append(o.reshape(res_shape))
        lses.append(lse.reshape(res_shape))
    oc = _full_attention(functools.partial(_attn_c_call, qc, kc, vc), lw["bound_c"])
    od = _full_attention(functools.partial(
        _band_call, qd, kd, vd, heads=D_HEAD_SPECS, bw=LANES, bq=128, hw=D_WIN, seg_len=SEQ,
        sink=lw["d_sink"], with_lse=False, out_dtype=jnp.bfloat16, name="band_d"), lw["bound_d"])[0]
    x = _merge_call(xs, lw, oa, obs, lses, oc, od)
    return _ffn_call(x, lw, out_rows)


def kernel(x_prompt, x_sample, norm1_g, w_in, w_gate, a_qnorm_g, a_knorm_g, a_lambda, a_subln_g, b_qnorm_g, b_knorm_g, c_qa_norm_g, c_kva_norm_g, c_w_uq, c_w_ukv, c_qnorm_g, c_knorm_g, d_qnorm_g, d_knorm_g, d_sink, w_br_a, w_br_b, w_br_c, w_br_d, w_o, norm2_g, w_ffn_gate, w_ffn_up, w_ffn_down):
    p = dict(norm1_g=norm1_g, w_in=w_in, w_gate=w_gate, a_qnorm_g=a_qnorm_g, a_knorm_g=a_knorm_g,
             a_lambda=a_lambda, a_subln_g=a_subln_g, b_qnorm_g=b_qnorm_g, b_knorm_g=b_knorm_g,
             c_qa_norm_g=c_qa_norm_g, c_kva_norm_g=c_kva_norm_g, c_w_uq=c_w_uq, c_w_ukv=c_w_ukv,
             c_qnorm_g=c_qnorm_g, c_knorm_g=c_knorm_g, d_qnorm_g=d_qnorm_g, d_knorm_g=d_knorm_g,
             d_sink=d_sink, w_br_a=w_br_a, w_br_b=w_br_b, w_br_c=w_br_c, w_br_d=w_br_d, w_o=w_o,
             norm2_g=norm2_g, w_ffn_gate=w_ffn_gate, w_ffn_up=w_ffn_up, w_ffn_down=w_ffn_down)
    xs = [x_prompt.reshape(-1, D_MODEL), x_sample.reshape(-1, D_MODEL)]
    rows = [x.shape[0] for x in xs]
    tables = _tables()
    for l in range(DEPTH):
        last = l == DEPTH - 1
        xs = _layer(xs, l, _layer_weights(l, p), tables, rows if last else [sum(rows)])
    return (xs[0].reshape(x_prompt.shape), xs[1].reshape(x_sample.shape))
```

```python
import functools
import math

import jax
import jax.numpy as jnp
import numpy as np
from jax import lax
from jax.experimental import pallas as pl
from jax.experimental.pallas import tpu as pltpu

D_MODEL = 1024
SEQ = 4096
DEPTH = 2
ROPE_THETA = 10000.0
EPS = 1e-6
NEG_INF = -1e30
N_BRANCH = 4
LOG2E = 1.4426950408889634
LN2 = 0.6931471805599453

A_HEADS = 4
A_HD = 64
B_PATTERNS = ((128, 1), (512, 4), (2048, 16))
B_HEADS = 4
B_HD = 64
B_W = B_HEADS * B_HD
C_HEADS = 8
C_Q_RANK = 256
C_KV_RANK = 128
C_NOPE = 64
C_ROPE = 32
C_VD = 64
D_QHEADS = 8
D_KVHEADS = 2
D_HD = 64
D_WIN = 128
D_FF = -(-8 * D_MODEL // (3 * 256)) * 256

IN_SIZES = (512, 512, 512) + (B_W,) * 9 + (C_Q_RANK, C_KV_RANK, C_ROPE, 512, 128, 128)
SPLIT_IDX = tuple(int(i) for i in np.cumsum(IN_SIZES)[:-1])

LANES = 128
MXU_N = 256

OFF_AQ, OFF_AK, OFF_AV = 0, 512, 1024
OFF_B = 1536
OFF_CQ = 3840
OFF_CKV = 4096
OFF_KR = 4224
OFF_DQ = 4352
OFF_DK = 4864
OFF_DV = 5120
N_PROJ = 5376

VMEM_LIMIT = 56 * 1024 * 1024

TM_PROJ = 512
TQ_A = 256
TQ_C = 256
HEADS_PER_STEP_A = 4
HEADS_PER_STEP_C = 4
BAND_UNROLL = 8
MAX_SAFE_BOUND = 50.0
TM_MERGE = 512
TM_FFN = 512
FFN_CHUNKS = ((0, 1024), (1024, 1024), (2048, 768))


def lambda_init(layer):
    return 0.8 - 0.6 * math.exp(-0.3 * layer)


def _const_spec(shape):
    nd = len(shape)
    return pl.BlockSpec(shape, lambda *_: (0,) * nd, pipeline_mode=pl.Buffered(1))


def _layer_spec(shape, layer):
    nd = len(shape)
    return pl.BlockSpec((None,) + tuple(shape), lambda *_: (layer,) + (0,) * nd, pipeline_mode=pl.Buffered(1))


def _params(n_grid):
    return pltpu.CompilerParams(dimension_semantics=("arbitrary",) * n_grid,
                                vmem_limit_bytes=VMEM_LIMIT)


def _dot(a, b):
    return jnp.dot(a, b, preferred_element_type=jnp.float32)


def _dot_nt(a, b):
    return lax.dot_general(a, b, (((1,), (1,)), ((), ())), preferred_element_type=jnp.float32)


def _rms(x, g):
    ms = jnp.mean(x * x, axis=-1, keepdims=True)
    return x * lax.rsqrt(ms + EPS) * g


def _row_specs(row_counts, tm, width):
    if len(row_counts) == 1:
        return [pl.BlockSpec((tm, width), lambda i: (i, 0))]
    nb = row_counts[0] // tm
    return [pl.BlockSpec((tm, width), lambda i: (jnp.minimum(i, nb - 1), 0)),
            pl.BlockSpec((tm, width), lambda i: (jnp.maximum(i - nb, 0), 0))]


def _row_tile(refs, first_blocks):
    if len(refs) == 1:
        return refs[0][...]
    return jnp.where(pl.program_id(0) < first_blocks, refs[0][...], refs[1][...])


def _group_sumsq(p, bd):
    outs = []
    for c in range(p.shape[1] // MXU_N):
        pc = p[:, c * MXU_N:(c + 1) * MXU_N]
        outs.append(_dot((pc * pc).astype(jnp.bfloat16), bd))
    return outs[0] if len(outs) == 1 else jnp.concatenate(outs, axis=-1)


def _rope_chunks(y, cos, sin, first_half, shift):
    outs = []
    for c in range(y.shape[1] // LANES):
        yc = y[:, c * LANES:(c + 1) * LANES]
        sw = jnp.where(first_half, pltpu.roll(yc, LANES - shift, 1), pltpu.roll(yc, shift, 1))
        outs.append(yc * cos + sw * sin)
    return outs[0] if len(outs) == 1 else jnp.concatenate(outs, axis=-1)


def _proj_kernel(*refs, n_x, first_blocks):
    x_refs, refs = refs[:n_x], refs[n_x:]
    (g1_ref, w_ref, grow_ref, bd_ref, cs_ref, qag_ref, kvag_ref,
     wuq_ref, wuk_ref, wuv_ref, gqc_ref, gkc_ref,
     qa, ka, va, qb0, kb0, vb0, qb1, kb1, vb1, qb2, kb2, vb2,
     qc, kc, vc, qd, kd, vd) = refs[:30]
    res_scr = list(refs[30:])
    tm = x_refs[0].shape[0]
    h = _rms(_row_tile(x_refs, first_blocks), g1_ref[...]).astype(jnp.bfloat16)
    bd64 = bd_ref[0]
    bd128 = bd_ref[1]
    cos64, sin64, cosc, sinc = cs_ref[0], cs_ref[1], cs_ref[2], cs_ref[3]
    lane = lax.broadcasted_iota(jnp.int32, (tm, LANES), 1)
    first64 = (lane & 63) < 32
    firstc = lane < (C_NOPE + C_ROPE // 2)

    def proj(off, width):
        return _dot(h, w_ref[:, off:off + width])

    def store(out_ref, val):
        if len(out_ref.shape) == 2:
            out_ref[...] = val.astype(out_ref.dtype)
            return
        dil, rows = out_ref.shape[1], out_ref.shape[2]
        scr = res_scr.pop()
        n_c = scr.shape[0]
        for c in range(n_c):
            scr[c] = val[:, c * LANES:(c + 1) * LANES]
        for r in range(dil):
            out_ref[0, r] = jnp.concatenate(
                [scr[c, pl.ds(r, rows, stride=dil), :] for c in range(n_c)], axis=-1).astype(out_ref.dtype)

    def norm_rope64(p, off, out_ref):
        ss = _group_sumsq(p, bd64)
        y = p * lax.rsqrt(ss * (1.0 / 64) + EPS) * grow_ref[:, off:off + p.shape[1]]
        store(out_ref, _rope_chunks(y, cos64, sin64, first64, 32))

    b_outs = ((qb0, kb0, vb0), (qb1, kb1, vb1), (qb2, kb2, vb2))
    sections = [(OFF_AQ, 512, qa, True), (OFF_AK, 512, ka, True)]
    for g, (qo, ko, _) in enumerate(b_outs):
        sections += [(OFF_B + g * 768, 256, qo, True), (OFF_B + g * 768 + 256, 256, ko, True)]
    sections += [(OFF_DQ, 512, qd, True), (OFF_DK, 256, kd, True), (OFF_AV, 512, va, False)]
    sections += [(OFF_B + g * 768 + 512, 256, vo, False) for g, (_, _, vo) in enumerate(b_outs)]
    sections += [(OFF_DV, 256, vd, False)]

    cq_raw = proj(OFF_CQ, C_Q_RANK)
    ckv_raw = proj(OFF_CKV, C_KV_RANK)
    kr = proj(OFF_KR, LANES)
    cqn = _rms(cq_raw, qag_ref[...]).astype(jnp.bfloat16)
    ckvn = _rms(ckv_raw, kvag_ref[...]).astype(jnp.bfloat16)
    qfull = _dot(cqn, wuq_ref[...])
    kfull = _dot(ckvn, wuk_ref[...]) + jnp.concatenate([kr] * C_HEADS, axis=-1)
    vc[...] = _dot(ckvn, wuv_ref[...]).astype(vc.dtype)
    p_next = proj(sections[0][0], sections[0][1])
    qn = qfull * lax.rsqrt(_group_sumsq(qfull, bd128) * (1.0 / 96) + EPS) * gqc_ref[...]
    kn = kfull * lax.rsqrt(_group_sumsq(kfull, bd128) * (1.0 / 96) + EPS) * gkc_ref[...]
    qc[...] = _rope_chunks(qn, cosc, sinc, firstc, C_ROPE // 2).astype(qc.dtype)
    kc[...] = _rope_chunks(kn, cosc, sinc, firstc, C_ROPE // 2).astype(kc.dtype)

    for n, (off, _, out_ref, roped) in enumerate(sections):
        p = p_next
        if n + 1 < len(sections):
            p_next = proj(sections[n + 1][0], sections[n + 1][1])
        if roped:
            norm_rope64(p, off, out_ref)
        else:
            store(out_ref, p)


def _proj_call(xs, lw, tables):
    rows = [x.shape[0] for x in xs]
    t = sum(rows)
    tm = TM_PROJ
    n_pos = SEQ // tm
    widths = (512, 512, 512) + (256,) * 9 + (1024, 1024, 512, 512, 256, 256)
    out_shape = [jax.ShapeDtypeStruct((t, w), jnp.bfloat16) for w in widths]
    out_specs = [pl.BlockSpec((tm, w), lambda i: (i, 0)) for w in widths]
    for g, (_, dil) in enumerate(B_PATTERNS):
        if dil > 1:
            for j in range(3 + 3 * g, 6 + 3 * g):
                out_shape[j] = jax.ShapeDtypeStruct((t // SEQ, dil, SEQ // dil, B_W), jnp.bfloat16)
                out_specs[j] = pl.BlockSpec((1, dil, tm // dil, B_W), lambda i: (i // n_pos, 0, i % n_pos, 0))
    n_res = 3 * sum(1 for _, dil in B_PATTERNS if dil > 1)
    in_specs = _row_specs(rows, tm, D_MODEL) + [
        _const_spec((1, D_MODEL)),
        _layer_spec((D_MODEL, N_PROJ), lw["layer"]),
        _const_spec((1, N_PROJ)),
        _const_spec((2, MXU_N, MXU_N)),
        pl.BlockSpec((4, tm, LANES), lambda i: (0, i % n_pos, 0)),
        _const_spec((1, C_Q_RANK)),
        _const_spec((1, C_KV_RANK)),
        _const_spec((C_Q_RANK, 1024)),
        _const_spec((C_KV_RANK, 1024)),
        _const_spec((C_KV_RANK, 512)),
        _const_spec((1, 1024)),
        _const_spec((1, 1024)),
    ]
    return pl.pallas_call(
        functools.partial(_proj_kernel, n_x=len(xs), first_blocks=rows[0] // tm),
        grid=(t // tm,), in_specs=in_specs, out_specs=out_specs, out_shape=out_shape,
        scratch_shapes=[pltpu.VMEM((B_W // LANES, tm, LANES), jnp.float32)] * n_res,
        compiler_params=_params(1), name="proj",
    )(*xs, lw["g1"], lw["w_in"], lw["grow"], tables["bd"], tables["cs"], lw["qag"], lw["kvag"],
      lw["wuq"], lw["wuk"], lw["wuv"], lw["gqc"], lw["gkc"])


def _lane_fold(x, op):
    acc = x[:, :LANES]
    for t in range(1, x.shape[1] // LANES):
        acc = op(acc, x[:, t * LANES:(t + 1) * LANES])
    return acc


def _online_step(state, s, v):
    r = jnp.max(_lane_fold(s, jnp.maximum), axis=-1, keepdims=True)
    if state is None:
        e = jnp.exp2(s - r)
        return r, _lane_fold(e, jnp.add), _dot(e.astype(jnp.bfloat16), v)
    m, l, acc = state
    m_new = jnp.maximum(m, r)
    alpha = jnp.exp2(m - m_new)
    e = jnp.exp2(s - m_new)
    return m_new, alpha * l + _lane_fold(e, jnp.add), alpha * acc + _dot(e.astype(jnp.bfloat16), v)


def _online_finish(state):
    _, l, acc = state
    return acc * (1.0 / jnp.sum(l, axis=-1, keepdims=True))


def _attn_a_kernel(lam_ref, q_ref, k_ref, v_ref, g_ref, o_ref, *, lam_init, bounded):
    lp = lam_ref[...]
    lam = (jnp.exp(jnp.sum(lp[0:1] * lp[1:2], axis=-1, keepdims=True))
           - jnp.exp(jnp.sum(lp[2:3] * lp[3:4], axis=-1, keepdims=True)) + lam_init)
    lane = lax.broadcasted_iota(jnp.int32, (q_ref.shape[0], LANES), 1)
    n_heads = q_ref.shape[1] // LANES

    def scores(h):
        hs = slice(h * LANES, (h + 1) * LANES)
        q = q_ref[:, hs]
        k = k_ref[:, hs]
        return [_dot_nt(jnp.where((lane >= c * A_HD) & (lane < (c + 1) * A_HD), q, jnp.zeros_like(q)), k)
                for c in range(2)]

    def finish(h, o):
        o = _rms(o, g_ref[...]) * (1.0 - lam_init)
        o_ref[:, h * LANES:(h + 1) * LANES] = o.astype(o_ref.dtype)

    if bounded:
        def exps(h):
            es, ls = [], []
            for s in scores(h):
                e = jnp.exp2(s)
                ls.append(jnp.sum(_lane_fold(e, jnp.add), axis=-1, keepdims=True))
                es.append(e)
            return es, ls

        nxt = exps(0)
        for h in range(n_heads):
            (e0, e1), (l0, l1) = nxt
            if h + 1 < n_heads:
                nxt = exps(h + 1)
            w = e0 - (lam * l0 / l1) * e1
            finish(h, _dot(w.astype(jnp.bfloat16), v_ref[:, h * LANES:(h + 1) * LANES]) * (1.0 / l0))
        return

    s_next = scores(0)
    for h in range(n_heads):
        s_cur = s_next
        if h + 1 < n_heads:
            s_next = scores(h + 1)
        es, inv = [], []
        for s in s_cur:
            e = jnp.exp2(s - jnp.max(s, axis=-1, keepdims=True))
            es.append(e)
            inv.append(1.0 / jnp.sum(e, axis=-1, keepdims=True))
        w = es[0] * inv[0] - es[1] * (lam * inv[1])
        finish(h, _dot(w.astype(jnp.bfloat16), v_ref[:, h * LANES:(h + 1) * LANES]))


def _attn_a_call(q, k, v, a_lambda, subln_row, lam_init, bounded):
    t = q.shape[0]
    nseq = t // SEQ
    tq = TQ_A
    nq = SEQ // tq
    w = HEADS_PER_STEP_A * LANES
    return pl.pallas_call(
        functools.partial(_attn_a_kernel, lam_init=lam_init, bounded=bounded),
        grid=(nseq, A_HEADS // HEADS_PER_STEP_A, nq),
        in_specs=[
            _const_spec((4, A_HD)),
            pl.BlockSpec((tq, w), lambda s, h, i: (s * nq + i, h)),
            pl.BlockSpec((SEQ, w), lambda s, h, i: (s, h)),
            pl.BlockSpec((SEQ, w), lambda s, h, i: (s, h)),
            _const_spec((1, LANES)),
        ],
        out_specs=pl.BlockSpec((tq, w), lambda s, h, i: (s * nq + i, h)),
        out_shape=jax.ShapeDtypeStruct((t, A_HEADS * LANES), jnp.bfloat16),
        compiler_params=_params(3), name="attn_a",
    )(a_lambda, q, k, v, subln_row)


def _attn_c_kernel(q_ref, k_ref, v_ref, o_ref, *, bounded):
    lane = lax.broadcasted_iota(jnp.int32, (o_ref.shape[0], LANES), 1)
    n_heads = q_ref.shape[1] // LANES

    def scores(h):
        hs = slice(h * LANES, (h + 1) * LANES)
        return _dot_nt(q_ref[:, hs], k_ref[:, hs])

    outs = []
    if bounded:
        for h in range(n_heads):
            e = jnp.exp2(scores(h))
            l = jnp.sum(_lane_fold(e, jnp.add), axis=-1, keepdims=True)
            outs.append(_dot(e.astype(jnp.bfloat16), v_ref[:, (h // 2) * LANES:(h // 2 + 1) * LANES]) * (1.0 / l))
    else:
        s_next = scores(0)
        for h in range(n_heads):
            s = s_next
            if h + 1 < n_heads:
                s_next = scores(h + 1)
            v = v_ref[:, (h // 2) * LANES:(h // 2 + 1) * LANES]
            half = s.shape[0] // 2
            parts = [_online_step(None, s[r:r + half], v) for r in (0, half)]
            outs.append(_online_finish(tuple(jnp.concatenate([a, b], axis=0) for a, b in zip(*parts))))
    for j in range(n_heads // 2):
        o_ref[:, j * LANES:(j + 1) * LANES] = jnp.where(
            lane < C_VD, outs[2 * j], outs[2 * j + 1]).astype(o_ref.dtype)


def _attn_c_call(q, k, v, bounded):
    t = q.shape[0]
    nseq = t // SEQ
    tq = TQ_C
    nq = SEQ // tq
    hps = HEADS_PER_STEP_C
    return pl.pallas_call(
        functools.partial(_attn_c_kernel, bounded=bounded),
        grid=(nseq, C_HEADS // hps, nq),
        in_specs=[
            pl.BlockSpec((tq, hps * LANES), lambda s, j, i: (s * nq + i, j)),
            pl.BlockSpec((SEQ, hps * LANES), lambda s, j, i: (s, j)),
            pl.BlockSpec((SEQ, hps * C_VD), lambda s, j, i: (s, j)),
        ],
        out_specs=pl.BlockSpec((tq, hps * C_VD), lambda s, j, i: (s * nq + i, j)),
        out_shape=jax.ShapeDtypeStruct((t, C_HEADS * C_VD), jnp.bfloat16),
        compiler_params=_params(3), name="attn_c",
    )(q, k, v)


def _full_attention(call, bound):
    return lax.cond(bound[0] <= MAX_SAFE_BOUND, lambda: call(True), lambda: call(False))


def _band_kernel(*refs, heads, bw, bq, hw, seg_len, with_sink, with_lse, bounded):
    if with_sink:
        sink_ref, q_ref, k_ref, v_ref = refs[:4]
        outs = refs[4:]
    else:
        q_ref, k_ref, v_ref = refs[:3]
        outs = refs[3:]
    o_ref = outs[0]
    lse_ref = outs[1] if with_lse else None
    ch = q_ref.shape[0]
    win = bq + 2 * hw
    shift = int(math.log2(seg_len))
    lane = lax.broadcasted_iota(jnp.int32, (bq, bw), 1)
    lane128 = lax.broadcasted_iota(jnp.int32, (bq, LANES), 1)
    assert seg_len % bq == 0 and seg_len >= win and B_HD == D_HD
    row_minus_col = (lax.broadcasted_iota(jnp.int32, (bq, win), 0)
                     - lax.broadcasted_iota(jnp.int32, (bq, win), 1))
    k_offs = sorted({hd[2] for hd in heads})

    def body(i, carry):
        q0 = pl.multiple_of(i * bq, bq)
        seg_lo = (q0 >> shift) << shift
        ws = pl.multiple_of(jnp.clip(q0 - hw, seg_lo, seg_lo + seg_len - win), hw)
        d = row_minus_col + (q0 - ws)
        bias = jnp.where((d <= hw) & (d >= -hw), 0.0, NEG_INF)
        qb = q_ref[pl.ds(q0, bq), :]
        kw = k_ref[pl.ds(ws, win), :]
        vw = v_ref[pl.ds(ws, win), :]
        acc, lacc = {}, {}
        for ko in k_offs:
            group = [hd for hd in heads if hd[2] == ko]
            qs = jnp.concatenate(
                [jnp.where((lane >= mo) & (lane < mo + B_HD), qb[:, qo:qo + bw], jnp.zeros((bq, bw), qb.dtype))
                 for qo, mo, _, _ in group], axis=0)
            s_all = _dot_nt(qs, kw[:, ko:ko + bw])
            es, ms, ls = [], [], []
            for n, (_, _, _, hid) in enumerate(group):
                s = s_all[n * bq:(n + 1) * bq] + bias
                if with_sink:
                    sk = sink_ref[hid] * LOG2E
                if bounded:
                    m = 0.0
                    e = jnp.exp2(s)
                    l = jnp.sum(e, axis=-1, keepdims=True)
                    if with_sink:
                        l = l + jnp.exp2(jnp.full((bq, 1), sk, jnp.float32))
                else:
                    m = jnp.max(s, axis=-1, keepdims=True)
                    if with_sink:
                        m = jnp.maximum(m, sk)
                    e = jnp.exp2(s - m)
                    l = jnp.sum(e, axis=-1, keepdims=True)
                    if with_sink:
                        l = l + jnp.exp2(sk - m)
                es.append(e.astype(jnp.bfloat16))
                ms.append(m)
                ls.append(l)
            for half in range(bw // LANES):
                sub = [n for n, (_, mo, _, _) in enumerate(group) if mo // LANES == half]
                o_sub = _dot(jnp.concatenate([es[n] for n in sub], axis=0),
                             vw[:, ko + half * LANES:ko + (half + 1) * LANES])
                for j, n in enumerate(sub):
                    qo, mo = group[n][0], group[n][1] % LANES
                    hm = (lane128 >= mo) & (lane128 < mo + B_HD)
                    key = (qo, half)
                    o = o_sub[j * bq:(j + 1) * bq] * (1.0 / ls[n])
                    acc[key] = jnp.where(hm, o, acc[key]) if key in acc else o
                    if with_lse:
                        lse = jnp.broadcast_to((ms[n] + jnp.log2(ls[n])) * LN2, (bq, LANES))
                        lacc[key] = jnp.where(hm, lse, lacc[key]) if key in lacc else lse
        for (qo, half), val in acc.items():
            lanes = slice(qo + half * LANES, qo + (half + 1) * LANES)
            o_ref[pl.ds(q0, bq), lanes] = val.astype(o_ref.dtype)
            if with_lse:
                lse_ref[pl.ds(q0, bq), lanes] = lacc[(qo, half)]
        return carry

    lax.fori_loop(0, ch // bq, body, 0, unroll=BAND_UNROLL)


def _band_call(q, k, v, bounded, *, heads, bw, bq, hw, seg_len, sink=None, with_lse, out_dtype, name):
    t, wq = q.shape
    wk = k.shape[1]
    ch = SEQ
    kern = functools.partial(_band_kernel, heads=heads, bw=bw, bq=bq, hw=hw, seg_len=seg_len,
                             with_sink=sink is not None, with_lse=with_lse, bounded=bounded)
    in_specs = [
        pl.BlockSpec((ch, wq), lambda i: (i, 0)),
        pl.BlockSpec((ch, wk), lambda i: (i, 0)),
        pl.BlockSpec((ch, wk), lambda i: (i, 0)),
    ]
    args = [q, k, v]
    if sink is not None:
        in_specs = [pl.BlockSpec(memory_space=pltpu.SMEM)] + in_specs
        args = [sink] + args
    out_shape = [jax.ShapeDtypeStruct((t, wq), out_dtype)]
    out_specs = [pl.BlockSpec((ch, wq), lambda i: (i, 0))]
    if with_lse:
        out_shape.append(jax.ShapeDtypeStruct((t, wq), jnp.float32))
        out_specs.append(pl.BlockSpec((ch, wq), lambda i: (i, 0)))
    return pl.pallas_call(
        kern, grid=(t // ch,), in_specs=in_specs, out_specs=out_specs, out_shape=out_shape,
        compiler_params=_params(1), name=name,
    )(*args)


B_HEAD_SPECS = tuple((0, h * B_HD, 0, h) for h in range(B_HEADS))
D_HEAD_SPECS = tuple(((h // 2) * LANES, (h % 2) * D_HD, (h // 4) * LANES, h) for h in range(D_QHEADS))


def _sigmoid(z):
    return 1.0 / (1.0 + jnp.exp(-z))


def _merge_kernel(*refs, n_x, first_blocks):
    x_refs, refs = refs[:n_x], refs[n_x:]
    (g1_ref, wg_ref, oa_ref, ob0, ob1, ob2, ls0, ls1, ls2, oc_ref, od_ref,
     wa_ref, wb_ref, wc_ref, wd_ref, wo_ref, out_ref) = refs[:17]
    res_scr = list(refs[17:])

    def load(ref):
        if len(ref.shape) == 2:
            return ref[...]
        dil, rows = ref.shape[1], ref.shape[2]
        scr = res_scr.pop()
        n_c = scr.shape[0]
        for r in range(dil):
            for c in range(n_c):
                scr[c, pl.ds(r, rows, stride=dil), :] = ref[0, r, :, c * LANES:(c + 1) * LANES]
        return jnp.concatenate([scr[c] for c in range(n_c)], axis=-1)

    x = _row_tile(x_refs, first_blocks)
    h = _rms(x, g1_ref[...]).astype(jnp.bfloat16)
    l0, l1, l2 = load(ls0), load(ls1), load(ls2)
    lm = jnp.maximum(jnp.maximum(l0, l1), l2)
    e0, e1, e2 = jnp.exp(l0 - lm), jnp.exp(l1 - lm), jnp.exp(l2 - lm)
    den = e0 + e1 + e2
    ob = ((e0 / den) * load(ob0) + (e1 / den) * load(ob1) + (e2 / den) * load(ob2)).astype(jnp.bfloat16)
    branches = ((oa_ref[...], wa_ref), (ob, wb_ref), (oc_ref[...], wc_ref), (od_ref[...], wd_ref))
    merged = None
    for i, (o, w_ref) in enumerate(branches):
        gate = _sigmoid(_dot(h, wg_ref[:, i * D_MODEL:(i + 1) * D_MODEL]))
        term = gate * _dot(o, w_ref[...])
        merged = term if merged is None else merged + term
    out_ref[...] = x + _dot(merged.astype(jnp.bfloat16), wo_ref[...])


def _merge_call(xs, lw, oa, obs, lses, oc, od):
    rows = [x.shape[0] for x in xs]
    t = sum(rows)
    tm = TM_MERGE

    def tile(w):
        return pl.BlockSpec((tm, w), lambda i: (i, 0))

    n_pos = SEQ // tm

    def band_tile(dil):
        if dil == 1:
            return tile(B_W)
        return pl.BlockSpec((1, dil, tm // dil, B_W), lambda i: (i // n_pos, 0, i % n_pos, 0))

    b_specs = [band_tile(dil) for _, dil in B_PATTERNS]
    in_specs = _row_specs(rows, tm, D_MODEL) + [
                _const_spec((1, D_MODEL)), _layer_spec((D_MODEL, N_BRANCH * D_MODEL), lw["layer"]),
                tile(512)] + b_specs + b_specs + [tile(512), tile(512)] + [
                _layer_spec((w, D_MODEL), lw["layer"]) for w in (512, 256, 512, 512, D_MODEL)]
    n_res = 2 * sum(1 for _, dil in B_PATTERNS if dil > 1)
    return pl.pallas_call(
        functools.partial(_merge_kernel, n_x=len(xs), first_blocks=rows[0] // tm),
        grid=(t // tm,), in_specs=in_specs, out_specs=tile(D_MODEL),
        out_shape=jax.ShapeDtypeStruct((t, D_MODEL), jnp.float32),
        scratch_shapes=[pltpu.VMEM((B_W // LANES, tm, LANES), jnp.float32)] * n_res,
        compiler_params=_params(1), name="merge",
    )(*xs, lw["g1"], lw["w_gate"], oa, *obs, *lses, oc, od,
      lw["w_br_a"], lw["w_br_b"], lw["w_br_c"], lw["w_br_d"], lw["w_o"])


def _ffn_kernel(x_ref, g2_ref, wg_ref, wu_ref, wd_ref, *out_refs, first_blocks):
    x = x_ref[...]
    hf = _rms(x, g2_ref[...]).astype(jnp.bfloat16)
    acc = x
    for off, width in FFN_CHUNKS:
        a = _dot(hf, wg_ref[:, off:off + width])
        u = _dot(hf, wu_ref[:, off:off + width])
        act = (a * _sigmoid(a) * u).astype(jnp.bfloat16)
        acc = acc + _dot(act, wd_ref[off:off + width, :])
    if len(out_refs) == 1:
        out_refs[0][...] = acc
    else:
        @pl.when(pl.program_id(0) < first_blocks)
        def _():
            out_refs[0][...] = acc

        @pl.when(pl.program_id(0) >= first_blocks)
        def _():
            out_refs[1][...] = acc


def _ffn_call(x, lw, out_rows):
    t = x.shape[0]
    tm = TM_FFN
    tile = pl.BlockSpec((tm, D_MODEL), lambda i: (i, 0))
    return pl.pallas_call(
        functools.partial(_ffn_kernel, first_blocks=out_rows[0] // tm), grid=(t // tm,),
        in_specs=[tile, _const_spec((1, D_MODEL)), _layer_spec((D_MODEL, D_FF), lw["layer"]),
                  _layer_spec((D_MODEL, D_FF), lw["layer"]), _layer_spec((D_FF, D_MODEL), lw["layer"])],
        out_specs=_row_specs(out_rows, tm, D_MODEL),
        out_shape=[jax.ShapeDtypeStruct((r, D_MODEL), jnp.float32) for r in out_rows],
        compiler_params=_params(1), name="ffn",
    )(x, lw["g2"], lw["w_ffn_gate"], lw["w_ffn_up"], lw["w_ffn_down"])


def _tables():
    pos = jnp.arange(SEQ, dtype=jnp.float32)[:, None]
    lane = jnp.arange(LANES)
    inv64 = jnp.power(ROPE_THETA, -jnp.arange(32, dtype=jnp.float32) / 32)
    ang64 = pos * inv64[lane % 32][None, :]
    sign64 = jnp.where((lane % 64) < 32, -1.0, 1.0)[None, :]
    cos64 = jnp.cos(ang64)
    sin64 = jnp.sin(ang64) * sign64
    invc = jnp.power(ROPE_THETA, -jnp.arange(16, dtype=jnp.float32) / 16)
    angc = pos * invc[lane % 16][None, :]
    is_rope = ((lane >= C_NOPE) & (lane < C_NOPE + C_ROPE))[None, :]
    signc = jnp.where(lane < C_NOPE + C_ROPE // 2, -1.0, 1.0)[None, :]
    cosc = jnp.where(is_rope, jnp.cos(angc), 1.0)
    sinc = jnp.where(is_rope, jnp.sin(angc) * signc, 0.0)
    cs = jnp.stack([cos64, sin64, cosc, sinc]).astype(jnp.float32)
    idx = np.arange(MXU_N)
    bd = np.stack([(idx[:, None] // 64) == (idx[None, :] // 64),
                   (idx[:, None] // 128) == (idx[None, :] // 128)]).astype(np.float32)
    return {"cs": cs, "bd": jnp.asarray(bd, dtype=jnp.bfloat16)}


def _stacked_weights(p):
    bf = jnp.bfloat16
    cols = jnp.split(p["w_in"], SPLIT_IDX, axis=-1)
    zeros = lambda n: jnp.zeros(p["w_in"].shape[:2] + (n,), jnp.float32)
    dk, dv = cols[16], cols[17]
    w_in = jnp.concatenate(
        list(cols[0:12]) + [cols[12], cols[13], zeros(64), cols[14], zeros(32), cols[15],
                            dk[..., :64], dk[..., :64], dk[..., 64:], dk[..., 64:],
                            dv[..., :64], dv[..., :64], dv[..., 64:], dv[..., 64:]], axis=-1).astype(bf)
    names = ("w_gate", "w_br_a", "w_br_b", "w_br_c", "w_br_d", "w_o", "w_ffn_gate", "w_ffn_up", "w_ffn_down")
    return {"w_in": w_in, **{n: p[n].astype(bf) for n in names}}


def _layer_weights(l, p, stacked):
    bf = jnp.bfloat16
    f32 = jnp.float32
    ones = lambda n: jnp.ones((n,), f32)
    qs = A_HD ** -0.5 * LOG2E
    grow = jnp.concatenate(
        [jnp.tile(p["a_qnorm_g"][l], 8) * qs, jnp.tile(p["a_knorm_g"][l], 8), ones(512)]
        + sum([[jnp.tile(p["b_qnorm_g"][l, g], 4) * qs, jnp.tile(p["b_knorm_g"][l, g], 4), ones(256)]
               for g in range(3)], [])
        + [ones(512), jnp.tile(p["d_qnorm_g"][l], 8) * qs, jnp.tile(p["d_knorm_g"][l], 4), ones(256)]
    )[None, :].astype(f32)
    wuq = p["c_w_uq"][l].reshape(C_Q_RANK, C_HEADS, C_NOPE + C_ROPE)
    wuq = jnp.pad(wuq, ((0, 0), (0, 0), (0, 32))).reshape(C_Q_RANK, C_HEADS * LANES).astype(bf)
    wukv = p["c_w_ukv"][l].reshape(C_KV_RANK, C_HEADS, C_NOPE + C_VD)
    wuk = jnp.pad(wukv[:, :, :C_NOPE], ((0, 0), (0, 0), (0, 64))).reshape(C_KV_RANK, C_HEADS * LANES).astype(bf)
    wuv = wukv[:, :, C_NOPE:].reshape(C_KV_RANK, C_HEADS * C_VD).astype(bf)
    cscale = (C_NOPE + C_ROPE) ** -0.5 * LOG2E
    pad32 = lambda g: jnp.tile(jnp.pad(g, (0, 32)), C_HEADS)[None, :].astype(f32)
    slack = (1.0 + 2.0 ** -8) ** 2
    bound_a = (A_HD * qs * slack * jnp.max(jnp.abs(p["a_qnorm_g"][l])) * jnp.max(jnp.abs(p["a_knorm_g"][l])))
    bound_c = ((C_NOPE + C_ROPE) * cscale * slack
               * jnp.max(jnp.abs(p["c_qnorm_g"][l])) * jnp.max(jnp.abs(p["c_knorm_g"][l])))
    bound_b = [(B_HD * qs * slack * jnp.max(jnp.abs(p["b_qnorm_g"][l, g]))
                * jnp.max(jnp.abs(p["b_knorm_g"][l, g]))).reshape(1).astype(f32) for g in range(len(B_PATTERNS))]
    bound_d = jnp.maximum(
        D_HD * qs * slack * jnp.max(jnp.abs(p["d_qnorm_g"][l])) * jnp.max(jnp.abs(p["d_knorm_g"][l])),
        LOG2E * jnp.max(jnp.abs(p["d_sink"][l])))
    return {
        **stacked, "layer": l,
        "bound_a": bound_a.reshape(1).astype(f32), "bound_c": bound_c.reshape(1).astype(f32),
        "bound_b": bound_b, "bound_d": bound_d.reshape(1).astype(f32),
        "g1": p["norm1_g"][l][None, :], "grow": grow,
        "qag": p["c_qa_norm_g"][l][None, :], "kvag": p["c_kva_norm_g"][l][None, :],
        "wuq": wuq, "wuk": wuk, "wuv": wuv,
        "gqc": pad32(p["c_qnorm_g"][l]) * cscale, "gkc": pad32(p["c_knorm_g"][l]),
        "a_lambda": p["a_lambda"][l], "subln": p["a_subln_g"][l][None, :],
        "d_sink": p["d_sink"][l], "g2": p["norm2_g"][l][None, :],
    }


def _layer(xs, l, lw, tables, out_rows):
    (qa, ka, va, qb0, kb0, vb0, qb1, kb1, vb1, qb2, kb2, vb2, qc, kc, vc, qd, kd, vd) = _proj_call(xs, lw, tables)
    oa = _full_attention(functools.partial(_attn_a_call, qa, ka, va, lw["a_lambda"], lw["subln"],
                                           lambda_init(l)), lw["bound_a"])
    obs, lses = [], []
    for g, (qg, kg, vg) in enumerate(((qb0, kb0, vb0), (qb1, kb1, vb1), (qb2, kb2, vb2))):
        window, dil = B_PATTERNS[g]
        res_shape = qg.shape
        qg, kg, vg = (a.reshape(-1, B_W) for a in (qg, kg, vg))
        bound = lw["bound_b"][g]
        o, lse = _full_attention(functools.partial(
            _band_call, qg, kg, vg, heads=B_HEAD_SPECS, bw=B_W, bq=128, hw=window // (2 * dil),
            seg_len=SEQ // dil, with_lse=True, out_dtype=jnp.float32, name=f"band_b{g}"), bound)
        obs.append(o.reshape(res_shape))
        lses.append(lse.reshape(res_shape))
    oc = _full_attention(functools.partial(_attn_c_call, qc, kc, vc), lw["bound_c"])
    od = _full_attention(functools.partial(
        _band_call, qd, kd, vd, heads=D_HEAD_SPECS, bw=LANES, bq=128, hw=D_WIN, seg_len=SEQ,
        sink=lw["d_sink"], with_lse=False, out_dtype=jnp.bfloat16, name="band_d"), lw["bound_d"])[0]
    x = _merge_call(xs, lw, oa, obs, lses, oc, od)
    return _ffn_call(x, lw, out_rows)


def kernel(x_prompt, x_sample, norm1_g, w_in, w_gate, a_qnorm_g, a_knorm_g, a_lambda, a_subln_g, b_qnorm_g, b_knorm_g, c_qa_norm_g, c_kva_norm_g, c_w_uq, c_w_ukv, c_qnorm_g, c_knorm_g, d_qnorm_g, d_knorm_g, d_sink, w_br_a, w_br_b, w_br_c, w_br_d, w_o, norm2_g, w_ffn_gate, w_ffn_up, w_ffn_down):
    p = dict(norm1_g=norm1_g, w_in=w_in, w_gate=w_gate, a_qnorm_g=a_qnorm_g, a_knorm_g=a_knorm_g,
             a_lambda=a_lambda, a_subln_g=a_subln_g, b_qnorm_g=b_qnorm_g, b_knorm_g=b_knorm_g,
             c_qa_norm_g=c_qa_norm_g, c_kva_norm_g=c_kva_norm_g, c_w_uq=c_w_uq, c_w_ukv=c_w_ukv,
             c_qnorm_g=c_qnorm_g, c_knorm_g=c_knorm_g, d_qnorm_g=d_qnorm_g, d_knorm_g=d_knorm_g,
             d_sink=d_sink, w_br_a=w_br_a, w_br_b=w_br_b, w_br_c=w_br_c, w_br_d=w_br_d, w_o=w_o,
             norm2_g=norm2_g, w_ffn_gate=w_ffn_gate, w_ffn_up=w_ffn_up, w_ffn_down=w_ffn_down)
    xs = [x_prompt.reshape(-1, D_MODEL), x_sample.reshape(-1, D_MODEL)]
    rows = [x.shape[0] for x in xs]
    tables = _tables()
    stacked = _stacked_weights(p)
    for l in range(DEPTH):
        last = l == DEPTH - 1
        xs = _layer(xs, l, _layer_weights(l, p, stacked), tables, rows if last else [sum(rows)])
    return (xs[0].reshape(x_prompt.shape), xs[1].reshape(x_sample.shape))
```

```python
import functools
import math

import jax
import jax.numpy as jnp
import numpy as np
from jax import lax
from jax.experimental import pallas as pl
from jax.experimental.pallas import tpu as pltpu

D_MODEL = 1024
SEQ = 4096
DEPTH = 2
ROPE_THETA = 10000.0
EPS = 1e-6
NEG_INF = -1e30
N_BRANCH = 4
LOG2E = 1.4426950408889634
LN2 = 0.6931471805599453

A_HEADS = 4
A_HD = 64
B_PATTERNS = ((128, 1), (512, 4), (2048, 16))
B_HEADS = 4
B_HD = 64
B_W = B_HEADS * B_HD
C_HEADS = 8
C_Q_RANK = 256
C_KV_RANK = 128
C_NOPE = 64
C_ROPE = 32
C_VD = 64
D_QHEADS = 8
D_KVHEADS = 2
D_HD = 64
D_WIN = 128
D_FF = -(-8 * D_MODEL // (3 * 256)) * 256

IN_SIZES = (512, 512, 512) + (B_W,) * 9 + (C_Q_RANK, C_KV_RANK, C_ROPE, 512, 128, 128)
SPLIT_IDX = tuple(int(i) for i in np.cumsum(IN_SIZES)[:-1])

LANES = 128
MXU_N = 256

OFF_AQ, OFF_AK, OFF_AV = 0, 512, 1024
OFF_B = 1536
OFF_CQ = 3840
OFF_CKV = 4096
OFF_KR = 4224
OFF_DQ = 4352
OFF_DK = 4864
OFF_DV = 4992
N_PROJ = 5120

VMEM_LIMIT = 56 * 1024 * 1024

TM_PROJ = 512
TQ_A = 256
TQ_C = 256
HEADS_PER_STEP_A = 4
HEADS_PER_STEP_C = 4
BAND_UNROLL = 8
MAX_SAFE_BOUND = 50.0
TM_MERGE = 512
TM_FFN = 512
FFN_CHUNKS = ((0, 1024), (1024, 1024), (2048, 768))


def lambda_init(layer):
    return 0.8 - 0.6 * math.exp(-0.3 * layer)


def _const_spec(shape):
    nd = len(shape)
    return pl.BlockSpec(shape, lambda *_: (0,) * nd, pipeline_mode=pl.Buffered(1))


def _layer_spec(shape, layer):
    nd = len(shape)
    return pl.BlockSpec((None,) + tuple(shape), lambda *_: (layer,) + (0,) * nd, pipeline_mode=pl.Buffered(1))


def _params(n_grid):
    return pltpu.CompilerParams(dimension_semantics=("arbitrary",) * n_grid,
                                vmem_limit_bytes=VMEM_LIMIT)


def _dot(a, b):
    return jnp.dot(a, b, preferred_element_type=jnp.float32)


def _dot_nt(a, b):
    return lax.dot_general(a, b, (((1,), (1,)), ((), ())), preferred_element_type=jnp.float32)


def _rms(x, g):
    ms = jnp.mean(x * x, axis=-1, keepdims=True)
    return x * lax.rsqrt(ms + EPS) * g


def _row_specs(row_counts, tm, width):
    if len(row_counts) == 1:
        return [pl.BlockSpec((tm, width), lambda i: (i, 0))]
    nb = row_counts[0] // tm
    return [pl.BlockSpec((tm, width), lambda i: (jnp.minimum(i, nb - 1), 0)),
            pl.BlockSpec((tm, width), lambda i: (jnp.maximum(i - nb, 0), 0))]


def _row_tile(refs, first_blocks):
    if len(refs) == 1:
        return refs[0][...]
    return jnp.where(pl.program_id(0) < first_blocks, refs[0][...], refs[1][...])


def _group_sumsq(p, bd):
    if p.shape[1] < MXU_N:
        return _dot((p * p).astype(jnp.bfloat16), bd[:p.shape[1], :p.shape[1]])
    outs = []
    for c in range(p.shape[1] // MXU_N):
        pc = p[:, c * MXU_N:(c + 1) * MXU_N]
        outs.append(_dot((pc * pc).astype(jnp.bfloat16), bd))
    return outs[0] if len(outs) == 1 else jnp.concatenate(outs, axis=-1)


def _rope_chunks(y, cos, sin, first_half, shift):
    outs = []
    for c in range(y.shape[1] // LANES):
        yc = y[:, c * LANES:(c + 1) * LANES]
        sw = jnp.where(first_half, pltpu.roll(yc, LANES - shift, 1), pltpu.roll(yc, shift, 1))
        outs.append(yc * cos + sw * sin)
    return outs[0] if len(outs) == 1 else jnp.concatenate(outs, axis=-1)


def _proj_kernel(*refs, n_x, first_blocks):
    x_refs, refs = refs[:n_x], refs[n_x:]
    (g1_ref, w_ref, grow_ref, bd_ref, cs_ref, qag_ref, kvag_ref,
     wuq_ref, wuk_ref, wuv_ref, gqc_ref, gkc_ref,
     qa, ka, va, qb0, kb0, vb0, qb1, kb1, vb1, qb2, kb2, vb2,
     qc, kc, vc, qd, kd, vd) = refs[:30]
    res_scr = list(refs[30:])
    tm = x_refs[0].shape[0]
    h = _rms(_row_tile(x_refs, first_blocks), g1_ref[...]).astype(jnp.bfloat16)
    bd64 = bd_ref[0]
    bd128 = bd_ref[1]
    cos64, sin64, cosc, sinc = cs_ref[0], cs_ref[1], cs_ref[2], cs_ref[3]
    lane = lax.broadcasted_iota(jnp.int32, (tm, LANES), 1)
    first64 = (lane & 63) < 32
    firstc = lane < (C_NOPE + C_ROPE // 2)

    def proj(off, width):
        return _dot(h, w_ref[:, off:off + width])

    def store(out_ref, val):
        if len(out_ref.shape) == 2:
            out_ref[...] = val.astype(out_ref.dtype)
            return
        dil, rows = out_ref.shape[1], out_ref.shape[2]
        scr = res_scr.pop()
        n_c = scr.shape[0]
        for c in range(n_c):
            scr[c] = val[:, c * LANES:(c + 1) * LANES]
        for r in range(dil):
            out_ref[0, r] = jnp.concatenate(
                [scr[c, pl.ds(r, rows, stride=dil), :] for c in range(n_c)], axis=-1).astype(out_ref.dtype)

    def norm_rope64(p, off, out_ref):
        ss = _group_sumsq(p, bd64)
        y = p * lax.rsqrt(ss * (1.0 / 64) + EPS) * grow_ref[:, off:off + p.shape[1]]
        store(out_ref, _rope_chunks(y, cos64, sin64, first64, 32))

    b_outs = ((qb0, kb0, vb0), (qb1, kb1, vb1), (qb2, kb2, vb2))
    sections = [(OFF_AQ, 512, qa, True), (OFF_AK, 512, ka, True)]
    for g, (qo, ko, _) in enumerate(b_outs):
        sections += [(OFF_B + g * 768, 256, qo, True), (OFF_B + g * 768 + 256, 256, ko, True)]
    sections += [(OFF_DQ, 512, qd, True), (OFF_DK, 128, kd, True), (OFF_AV, 512, va, False)]
    sections += [(OFF_B + g * 768 + 512, 256, vo, False) for g, (_, _, vo) in enumerate(b_outs)]
    sections += [(OFF_DV, 128, vd, False)]

    cq_raw = proj(OFF_CQ, C_Q_RANK)
    ckv_raw = proj(OFF_CKV, C_KV_RANK)
    kr = proj(OFF_KR, LANES)
    cqn = _rms(cq_raw, qag_ref[...]).astype(jnp.bfloat16)
    ckvn = _rms(ckv_raw, kvag_ref[...]).astype(jnp.bfloat16)
    qfull = _dot(cqn, wuq_ref[...])
    kfull = _dot(ckvn, wuk_ref[...]) + jnp.concatenate([kr] * C_HEADS, axis=-1)
    vc[...] = _dot(ckvn, wuv_ref[...]).astype(vc.dtype)
    p_next = proj(sections[0][0], sections[0][1])
    qn = qfull * lax.rsqrt(_group_sumsq(qfull, bd128) * (1.0 / 96) + EPS) * gqc_ref[...]
    kn = kfull * lax.rsqrt(_group_sumsq(kfull, bd128) * (1.0 / 96) + EPS) * gkc_ref[...]
    qc[...] = _rope_chunks(qn, cosc, sinc, firstc, C_ROPE // 2).astype(qc.dtype)
    kc[...] = _rope_chunks(kn, cosc, sinc, firstc, C_ROPE // 2).astype(kc.dtype)

    for n, (off, _, out_ref, roped) in enumerate(sections):
        p = p_next
        if n + 1 < len(sections):
            p_next = proj(sections[n + 1][0], sections[n + 1][1])
        if roped:
            norm_rope64(p, off, out_ref)
        else:
            store(out_ref, p)


def _proj_call(xs, lw, tables):
    rows = [x.shape[0] for x in xs]
    t = sum(rows)
    tm = TM_PROJ
    n_pos = SEQ // tm
    widths = (512, 512, 512) + (256,) * 9 + (1024, 1024, 512, 512, 128, 128)
    out_shape = [jax.ShapeDtypeStruct((t, w), jnp.bfloat16) for w in widths]
    out_specs = [pl.BlockSpec((tm, w), lambda i: (i, 0)) for w in widths]
    for g, (_, dil) in enumerate(B_PATTERNS):
        if dil > 1:
            for j in range(3 + 3 * g, 6 + 3 * g):
                out_shape[j] = jax.ShapeDtypeStruct((t // SEQ, dil, SEQ // dil, B_W), jnp.bfloat16)
                out_specs[j] = pl.BlockSpec((1, dil, tm // dil, B_W), lambda i: (i // n_pos, 0, i % n_pos, 0))
    n_res = 3 * sum(1 for _, dil in B_PATTERNS if dil > 1)
    in_specs = _row_specs(rows, tm, D_MODEL) + [
        _const_spec((1, D_MODEL)),
        _layer_spec((D_MODEL, N_PROJ), lw["layer"]),
        _const_spec((1, N_PROJ)),
        _const_spec((2, MXU_N, MXU_N)),
        pl.BlockSpec((4, tm, LANES), lambda i: (0, i % n_pos, 0)),
        _const_spec((1, C_Q_RANK)),
        _const_spec((1, C_KV_RANK)),
        _const_spec((C_Q_RANK, 1024)),
        _const_spec((C_KV_RANK, 1024)),
        _const_spec((C_KV_RANK, 512)),
        _const_spec((1, 1024)),
        _const_spec((1, 1024)),
    ]
    return pl.pallas_call(
        functools.partial(_proj_kernel, n_x=len(xs), first_blocks=rows[0] // tm),
        grid=(t // tm,), in_specs=in_specs, out_specs=out_specs, out_shape=out_shape,
        scratch_shapes=[pltpu.VMEM((B_W // LANES, tm, LANES), jnp.float32)] * n_res,
        compiler_params=_params(1), name="proj",
    )(*xs, lw["g1"], lw["w_in"], lw["grow"], tables["bd"], tables["cs"], lw["qag"], lw["kvag"],
      lw["wuq"], lw["wuk"], lw["wuv"], lw["gqc"], lw["gkc"])


def _lane_fold(x, op):
    acc = x[:, :LANES]
    for t in range(1, x.shape[1] // LANES):
        acc = op(acc, x[:, t * LANES:(t + 1) * LANES])
    return acc


def _online_step(state, s, v):
    r = jnp.max(_lane_fold(s, jnp.maximum), axis=-1, keepdims=True)
    if state is None:
        e = jnp.exp2(s - r)
        return r, _lane_fold(e, jnp.add), _dot(e.astype(jnp.bfloat16), v)
    m, l, acc = state
    m_new = jnp.maximum(m, r)
    alpha = jnp.exp2(m - m_new)
    e = jnp.exp2(s - m_new)
    return m_new, alpha * l + _lane_fold(e, jnp.add), alpha * acc + _dot(e.astype(jnp.bfloat16), v)


def _online_finish(state):
    _, l, acc = state
    return acc * (1.0 / jnp.sum(l, axis=-1, keepdims=True))


def _attn_a_kernel(lam_ref, q_ref, k_ref, v_ref, g_ref, o_ref, *, lam_init, bounded):
    lp = lam_ref[...]
    lam = (jnp.exp(jnp.sum(lp[0:1] * lp[1:2], axis=-1, keepdims=True))
           - jnp.exp(jnp.sum(lp[2:3] * lp[3:4], axis=-1, keepdims=True)) + lam_init)
    lane = lax.broadcasted_iota(jnp.int32, (q_ref.shape[0], LANES), 1)
    n_heads = q_ref.shape[1] // LANES

    def scores(h):
        hs = slice(h * LANES, (h + 1) * LANES)
        q = q_ref[:, hs]
        k = k_ref[:, hs]
        return [_dot_nt(jnp.where((lane >= c * A_HD) & (lane < (c + 1) * A_HD), q, jnp.zeros_like(q)), k)
                for c in range(2)]

    def finish(h, o):
        o = _rms(o, g_ref[...]) * (1.0 - lam_init)
        o_ref[:, h * LANES:(h + 1) * LANES] = o.astype(o_ref.dtype)

    if bounded:
        def exps(h):
            es, ls = [], []
            for s in scores(h):
                e = jnp.exp2(s)
                ls.append(jnp.sum(_lane_fold(e, jnp.add), axis=-1, keepdims=True))
                es.append(e)
            return es, ls

        nxt = exps(0)
        for h in range(n_heads):
            (e0, e1), (l0, l1) = nxt
            if h + 1 < n_heads:
                nxt = exps(h + 1)
            w = e0 - (lam * l0 / l1) * e1
            finish(h, _dot(w.astype(jnp.bfloat16), v_ref[:, h * LANES:(h + 1) * LANES]) * (1.0 / l0))
        return

    s_next = scores(0)
    for h in range(n_heads):
        s_cur = s_next
        if h + 1 < n_heads:
            s_next = scores(h + 1)
        es, inv = [], []
        for s in s_cur:
            e = jnp.exp2(s - jnp.max(s, axis=-1, keepdims=True))
            es.append(e)
            inv.append(1.0 / jnp.sum(e, axis=-1, keepdims=True))
        w = es[0] * inv[0] - es[1] * (lam * inv[1])
        finish(h, _dot(w.astype(jnp.bfloat16), v_ref[:, h * LANES:(h + 1) * LANES]))


def _attn_a_call(q, k, v, a_lambda, subln_row, lam_init, bounded):
    t = q.shape[0]
    nseq = t // SEQ
    tq = TQ_A
    nq = SEQ // tq
    w = HEADS_PER_STEP_A * LANES
    return pl.pallas_call(
        functools.partial(_attn_a_kernel, lam_init=lam_init, bounded=bounded),
        grid=(nseq, A_HEADS // HEADS_PER_STEP_A, nq),
        in_specs=[
            _const_spec((4, A_HD)),
            pl.BlockSpec((tq, w), lambda s, h, i: (s * nq + i, h)),
            pl.BlockSpec((SEQ, w), lambda s, h, i: (s, h)),
            pl.BlockSpec((SEQ, w), lambda s, h, i: (s, h)),
            _const_spec((1, LANES)),
        ],
        out_specs=pl.BlockSpec((tq, w), lambda s, h, i: (s * nq + i, h)),
        out_shape=jax.ShapeDtypeStruct((t, A_HEADS * LANES), jnp.bfloat16),
        compiler_params=_params(3), name="attn_a",
    )(a_lambda, q, k, v, subln_row)


def _attn_c_kernel(q_ref, k_ref, v_ref, o_ref, *, bounded):
    lane = lax.broadcasted_iota(jnp.int32, (o_ref.shape[0], LANES), 1)
    n_heads = q_ref.shape[1] // LANES

    def scores(h):
        hs = slice(h * LANES, (h + 1) * LANES)
        return _dot_nt(q_ref[:, hs], k_ref[:, hs])

    outs = []
    if bounded:
        for h in range(n_heads):
            e = jnp.exp2(scores(h))
            l = jnp.sum(_lane_fold(e, jnp.add), axis=-1, keepdims=True)
            outs.append(_dot(e.astype(jnp.bfloat16), v_ref[:, (h // 2) * LANES:(h // 2 + 1) * LANES]) * (1.0 / l))
    else:
        s_next = scores(0)
        for h in range(n_heads):
            s = s_next
            if h + 1 < n_heads:
                s_next = scores(h + 1)
            v = v_ref[:, (h // 2) * LANES:(h // 2 + 1) * LANES]
            half = s.shape[0] // 2
            parts = [_online_step(None, s[r:r + half], v) for r in (0, half)]
            outs.append(_online_finish(tuple(jnp.concatenate([a, b], axis=0) for a, b in zip(*parts))))
    for j in range(n_heads // 2):
        o_ref[:, j * LANES:(j + 1) * LANES] = jnp.where(
            lane < C_VD, outs[2 * j], outs[2 * j + 1]).astype(o_ref.dtype)


def _attn_c_call(q, k, v, bounded):
    t = q.shape[0]
    nseq = t // SEQ
    tq = TQ_C
    nq = SEQ // tq
    hps = HEADS_PER_STEP_C
    return pl.pallas_call(
        functools.partial(_attn_c_kernel, bounded=bounded),
        grid=(nseq, C_HEADS // hps, nq),
        in_specs=[
            pl.BlockSpec((tq, hps * LANES), lambda s, j, i: (s * nq + i, j)),
            pl.BlockSpec((SEQ, hps * LANES), lambda s, j, i: (s, j)),
            pl.BlockSpec((SEQ, hps * C_VD), lambda s, j, i: (s, j)),
        ],
        out_specs=pl.BlockSpec((tq, hps * C_VD), lambda s, j, i: (s * nq + i, j)),
        out_shape=jax.ShapeDtypeStruct((t, C_HEADS * C_VD), jnp.bfloat16),
        compiler_params=_params(3), name="attn_c",
    )(q, k, v)


def _full_attention(call, bound):
    return lax.cond(bound[0] <= MAX_SAFE_BOUND, lambda: call(True), lambda: call(False))


def _band_kernel(*refs, heads, bw, bq, hw, seg_len, with_sink, with_lse, bounded):
    if with_sink:
        sink_ref, q_ref, k_ref, v_ref = refs[:4]
        outs = refs[4:]
    else:
        q_ref, k_ref, v_ref = refs[:3]
        outs = refs[3:]
    o_ref = outs[0]
    lse_ref = outs[1] if with_lse else None
    ch = q_ref.shape[0]
    win = bq + 2 * hw
    shift = int(math.log2(seg_len))
    lane = lax.broadcasted_iota(jnp.int32, (bq, bw), 1)
    lane128 = lax.broadcasted_iota(jnp.int32, (bq, LANES), 1)
    assert seg_len % bq == 0 and seg_len >= win and B_HD == D_HD
    row_minus_col = (lax.broadcasted_iota(jnp.int32, (bq, win), 0)
                     - lax.broadcasted_iota(jnp.int32, (bq, win), 1))
    k_offs = sorted({hd[2] for hd in heads})

    def body(i, carry):
        q0 = pl.multiple_of(i * bq, bq)
        seg_lo = (q0 >> shift) << shift
        ws = pl.multiple_of(jnp.clip(q0 - hw, seg_lo, seg_lo + seg_len - win), hw)
        d = row_minus_col + (q0 - ws)
        bias = jnp.where((d <= hw) & (d >= -hw), 0.0, NEG_INF)
        qb = q_ref[pl.ds(q0, bq), :]
        kw = k_ref[pl.ds(ws, win), :]
        vw = v_ref[pl.ds(ws, win), :]
        acc, lacc = {}, {}
        for ko in k_offs:
            group = [hd for hd in heads if hd[2] == ko]
            qs = jnp.concatenate(
                [jnp.where((lane >= mo) & (lane < mo + B_HD), qb[:, qo:qo + bw], jnp.zeros((bq, bw), qb.dtype))
                 for qo, mo, _, _ in group], axis=0)
            s_all = _dot_nt(qs, kw[:, ko:ko + bw])
            es, ms, ls = [], [], []
            for n, (_, _, _, hid) in enumerate(group):
                s = s_all[n * bq:(n + 1) * bq] + bias
                if with_sink:
                    sk = sink_ref[hid] * LOG2E
                if bounded:
                    m = 0.0
                    e = jnp.exp2(s)
                    l = jnp.sum(e, axis=-1, keepdims=True)
                    if with_sink:
                        l = l + jnp.exp2(jnp.full((bq, 1), sk, jnp.float32))
                else:
                    m = jnp.max(s, axis=-1, keepdims=True)
                    if with_sink:
                        m = jnp.maximum(m, sk)
                    e = jnp.exp2(s - m)
                    l = jnp.sum(e, axis=-1, keepdims=True)
                    if with_sink:
                        l = l + jnp.exp2(sk - m)
                es.append(e.astype(jnp.bfloat16))
                ms.append(m)
                ls.append(l)
            for half in range(bw // LANES):
                sub = [n for n, (_, mo, _, _) in enumerate(group) if mo // LANES == half]
                o_sub = _dot(jnp.concatenate([es[n] for n in sub], axis=0),
                             vw[:, ko + half * LANES:ko + (half + 1) * LANES])
                for j, n in enumerate(sub):
                    qo, mo = group[n][0], group[n][1] % LANES
                    hm = (lane128 >= mo) & (lane128 < mo + B_HD)
                    key = (qo, half)
                    o = o_sub[j * bq:(j + 1) * bq] * (1.0 / ls[n])
                    acc[key] = jnp.where(hm, o, acc[key]) if key in acc else o
                    if with_lse:
                        lse = jnp.broadcast_to((ms[n] + jnp.log2(ls[n])) * LN2, (bq, LANES))
                        lacc[key] = jnp.where(hm, lse, lacc[key]) if key in lacc else lse
        for (qo, half), val in acc.items():
            lanes = slice(qo + half * LANES, qo + (half + 1) * LANES)
            o_ref[pl.ds(q0, bq), lanes] = val.astype(o_ref.dtype)
            if with_lse:
                lse_ref[pl.ds(q0, bq), lanes] = lacc[(qo, half)]
        return carry

    lax.fori_loop(0, ch // bq, body, 0, unroll=BAND_UNROLL)


def _band_call(q, k, v, bounded, *, heads, bw, bq, hw, seg_len, sink=None, with_lse, out_dtype, name):
    t, wq = q.shape
    wk = k.shape[1]
    ch = SEQ
    kern = functools.partial(_band_kernel, heads=heads, bw=bw, bq=bq, hw=hw, seg_len=seg_len,
                             with_sink=sink is not None, with_lse=with_lse, bounded=bounded)
    in_specs = [
        pl.BlockSpec((ch, wq), lambda i: (i, 0)),
        pl.BlockSpec((ch, wk), lambda i: (i, 0)),
        pl.BlockSpec((ch, wk), lambda i: (i, 0)),
    ]
    args = [q, k, v]
    if sink is not None:
        in_specs = [pl.BlockSpec(memory_space=pltpu.SMEM)] + in_specs
        args = [sink] + args
    out_shape = [jax.ShapeDtypeStruct((t, wq), out_dtype)]
    out_specs = [pl.BlockSpec((ch, wq), lambda i: (i, 0))]
    if with_lse:
        out_shape.append(jax.ShapeDtypeStruct((t, wq), jnp.float32))
        out_specs.append(pl.BlockSpec((ch, wq), lambda i: (i, 0)))
    return pl.pallas_call(
        kern, grid=(t // ch,), in_specs=in_specs, out_specs=out_specs, out_shape=out_shape,
        compiler_params=_params(1), name=name,
    )(*args)


B_HEAD_SPECS = tuple((0, h * B_HD, 0, h) for h in range(B_HEADS))
D_REP = D_QHEADS // D_KVHEADS
D_HEAD_SPECS = tuple((r * LANES, g * D_HD, 0, g * D_REP + r) for g in range(D_KVHEADS) for r in range(D_REP))


def _sigmoid(z):
    return 1.0 / (1.0 + jnp.exp(-z))


def _merge_kernel(*refs, n_x, first_blocks):
    x_refs, refs = refs[:n_x], refs[n_x:]
    (g1_ref, wg_ref, oa_ref, ob0, ob1, ob2, ls0, ls1, ls2, oc_ref, od_ref,
     wa_ref, wb_ref, wc_ref, wd_ref, wo_ref, out_ref) = refs[:17]
    res_scr = list(refs[17:])

    def load(ref):
        if len(ref.shape) == 2:
            return ref[...]
        dil, rows = ref.shape[1], ref.shape[2]
        scr = res_scr.pop()
        n_c = scr.shape[0]
        for r in range(dil):
            for c in range(n_c):
                scr[c, pl.ds(r, rows, stride=dil), :] = ref[0, r, :, c * LANES:(c + 1) * LANES]
        return jnp.concatenate([scr[c] for c in range(n_c)], axis=-1)

    x = _row_tile(x_refs, first_blocks)
    h = _rms(x, g1_ref[...]).astype(jnp.bfloat16)
    l0, l1, l2 = load(ls0), load(ls1), load(ls2)
    lm = jnp.maximum(jnp.maximum(l0, l1), l2)
    e0, e1, e2 = jnp.exp(l0 - lm), jnp.exp(l1 - lm), jnp.exp(l2 - lm)
    den = e0 + e1 + e2
    ob = ((e0 / den) * load(ob0) + (e1 / den) * load(ob1) + (e2 / den) * load(ob2)).astype(jnp.bfloat16)
    branches = ((oa_ref[...], wa_ref), (ob, wb_ref), (oc_ref[...], wc_ref), (od_ref[...], wd_ref))
    merged = None
    for i, (o, w_ref) in enumerate(branches):
        gate = _sigmoid(_dot(h, wg_ref[:, i * D_MODEL:(i + 1) * D_MODEL]))
        term = gate * _dot(o, w_ref[...])
        merged = term if merged is None else merged + term
    out_ref[...] = x + _dot(merged.astype(jnp.bfloat16), wo_ref[...])


def _merge_call(xs, lw, oa, obs, lses, oc, od):
    rows = [x.shape[0] for x in xs]
    t = sum(rows)
    tm = TM_MERGE

    def tile(w):
        return pl.BlockSpec((tm, w), lambda i: (i, 0))

    n_pos = SEQ // tm

    def band_tile(dil):
        if dil == 1:
            return tile(B_W)
        return pl.BlockSpec((1, dil, tm // dil, B_W), lambda i: (i // n_pos, 0, i % n_pos, 0))

    b_specs = [band_tile(dil) for _, dil in B_PATTERNS]
    in_specs = _row_specs(rows, tm, D_MODEL) + [
                _const_spec((1, D_MODEL)), _layer_spec((D_MODEL, N_BRANCH * D_MODEL), lw["layer"]),
                tile(512)] + b_specs + b_specs + [tile(512), tile(512)] + [
                _layer_spec((w, D_MODEL), lw["layer"]) for w in (512, 256, 512, 512, D_MODEL)]
    n_res = 2 * sum(1 for _, dil in B_PATTERNS if dil > 1)
    return pl.pallas_call(
        functools.partial(_merge_kernel, n_x=len(xs), first_blocks=rows[0] // tm),
        grid=(t // tm,), in_specs=in_specs, out_specs=tile(D_MODEL),
        out_shape=jax.ShapeDtypeStruct((t, D_MODEL), jnp.float32),
        scratch_shapes=[pltpu.VMEM((B_W // LANES, tm, LANES), jnp.float32)] * n_res,
        compiler_params=_params(1), name="merge",
    )(*xs, lw["g1"], lw["w_gate"], oa, *obs, *lses, oc, od,
      lw["w_br_a"], lw["w_br_b"], lw["w_br_c"], lw["w_br_d"], lw["w_o"])


def _ffn_kernel(x_ref, g2_ref, wg_ref, wu_ref, wd_ref, *out_refs, first_blocks):
    x = x_ref[...]
    hf = _rms(x, g2_ref[...]).astype(jnp.bfloat16)
    acc = x
    for off, width in FFN_CHUNKS:
        a = _dot(hf, wg_ref[:, off:off + width])
        u = _dot(hf, wu_ref[:, off:off + width])
        act = (a * _sigmoid(a) * u).astype(jnp.bfloat16)
        acc = acc + _dot(act, wd_ref[off:off + width, :])
    if len(out_refs) == 1:
        out_refs[0][...] = acc
    else:
        @pl.when(pl.program_id(0) < first_blocks)
        def _():
            out_refs[0][...] = acc

        @pl.when(pl.program_id(0) >= first_blocks)
        def _():
            out_refs[1][...] = acc


def _ffn_call(x, lw, out_rows):
    t = x.shape[0]
    tm = TM_FFN
    tile = pl.BlockSpec((tm, D_MODEL), lambda i: (i, 0))
    return pl.pallas_call(
        functools.partial(_ffn_kernel, first_blocks=out_rows[0] // tm), grid=(t // tm,),
        in_specs=[tile, _const_spec((1, D_MODEL)), _layer_spec((D_MODEL, D_FF), lw["layer"]),
                  _layer_spec((D_MODEL, D_FF), lw["layer"]), _layer_spec((D_FF, D_MODEL), lw["layer"])],
        out_specs=_row_specs(out_rows, tm, D_MODEL),
        out_shape=[jax.ShapeDtypeStruct((r, D_MODEL), jnp.float32) for r in out_rows],
        compiler_params=_params(1), name="ffn",
    )(x, lw["g2"], lw["w_ffn_gate"], lw["w_ffn_up"], lw["w_ffn_down"])


def _tables():
    pos = jnp.arange(SEQ, dtype=jnp.float32)[:, None]
    lane = jnp.arange(LANES)
    inv64 = jnp.power(ROPE_THETA, -jnp.arange(32, dtype=jnp.float32) / 32)
    ang64 = pos * inv64[lane % 32][None, :]
    sign64 = jnp.where((lane % 64) < 32, -1.0, 1.0)[None, :]
    cos64 = jnp.cos(ang64)
    sin64 = jnp.sin(ang64) * sign64
    invc = jnp.power(ROPE_THETA, -jnp.arange(16, dtype=jnp.float32) / 16)
    angc = pos * invc[lane % 16][None, :]
    is_rope = ((lane >= C_NOPE) & (lane < C_NOPE + C_ROPE))[None, :]
    signc = jnp.where(lane < C_NOPE + C_ROPE // 2, -1.0, 1.0)[None, :]
    cosc = jnp.where(is_rope, jnp.cos(angc), 1.0)
    sinc = jnp.where(is_rope, jnp.sin(angc) * signc, 0.0)
    cs = jnp.stack([cos64, sin64, cosc, sinc]).astype(jnp.float32)
    idx = np.arange(MXU_N)
    bd = np.stack([(idx[:, None] // 64) == (idx[None, :] // 64),
                   (idx[:, None] // 128) == (idx[None, :] // 128)]).astype(np.float32)
    return {"cs": cs, "bd": jnp.asarray(bd, dtype=jnp.bfloat16)}


def _stacked_weights(p):
    bf = jnp.bfloat16
    cols = jnp.split(p["w_in"], SPLIT_IDX, axis=-1)
    zeros = lambda n: jnp.zeros(p["w_in"].shape[:2] + (n,), jnp.float32)
    d_order = np.array([g * D_REP + r for r in range(D_REP) for g in range(D_KVHEADS)])
    by_head = lambda w, axis: jnp.take(w.reshape(w.shape[:axis] + (D_QHEADS, D_HD) + w.shape[axis + 1:]),
                                       d_order, axis=axis).reshape(w.shape)
    w_in = jnp.concatenate(
        list(cols[0:12]) + [cols[12], cols[13], zeros(64), cols[14], zeros(32), by_head(cols[15], 2),
                            cols[16], cols[17]], axis=-1).astype(bf)
    names = ("w_gate", "w_br_a", "w_br_b", "w_br_c", "w_o", "w_ffn_gate", "w_ffn_up", "w_ffn_down")
    return {"w_in": w_in, "w_br_d": by_head(p["w_br_d"], 1).astype(bf), **{n: p[n].astype(bf) for n in names}}


def _layer_weights(l, p, stacked):
    bf = jnp.bfloat16
    f32 = jnp.float32
    ones = lambda n: jnp.ones((n,), f32)
    qs = A_HD ** -0.5 * LOG2E
    grow = jnp.concatenate(
        [jnp.tile(p["a_qnorm_g"][l], 8) * qs, jnp.tile(p["a_knorm_g"][l], 8), ones(512)]
        + sum([[jnp.tile(p["b_qnorm_g"][l, g], 4) * qs, jnp.tile(p["b_knorm_g"][l, g], 4), ones(256)]
               for g in range(3)], [])
        + [ones(512), jnp.tile(p["d_qnorm_g"][l], 8) * qs, jnp.tile(p["d_knorm_g"][l], 2), ones(128)]
    )[None, :].astype(f32)
    wuq = p["c_w_uq"][l].reshape(C_Q_RANK, C_HEADS, C_NOPE + C_ROPE)
    wuq = jnp.pad(wuq, ((0, 0), (0, 0), (0, 32))).reshape(C_Q_RANK, C_HEADS * LANES).astype(bf)
    wukv = p["c_w_ukv"][l].reshape(C_KV_RANK, C_HEADS, C_NOPE + C_VD)
    wuk = jnp.pad(wukv[:, :, :C_NOPE], ((0, 0), (0, 0), (0, 64))).reshape(C_KV_RANK, C_HEADS * LANES).astype(bf)
    wuv = wukv[:, :, C_NOPE:].reshape(C_KV_RANK, C_HEADS * C_VD).astype(bf)
    cscale = (C_NOPE + C_ROPE) ** -0.5 * LOG2E
    pad32 = lambda g: jnp.tile(jnp.pad(g, (0, 32)), C_HEADS)[None, :].astype(f32)
    slack = (1.0 + 2.0 ** -8) ** 2
    bound_a = (A_HD * qs * slack * jnp.max(jnp.abs(p["a_qnorm_g"][l])) * jnp.max(jnp.abs(p["a_knorm_g"][l])))
    bound_c = ((C_NOPE + C_ROPE) * cscale * slack
               * jnp.max(jnp.abs(p["c_qnorm_g"][l])) * jnp.max(jnp.abs(p["c_knorm_g"][l])))
    bound_b = [(B_HD * qs * slack * jnp.max(jnp.abs(p["b_qnorm_g"][l, g]))
                * jnp.max(jnp.abs(p["b_knorm_g"][l, g]))).reshape(1).astype(f32) for g in range(len(B_PATTERNS))]
    bound_d = jnp.maximum(
        D_HD * qs * slack * jnp.max(jnp.abs(p["d_qnorm_g"][l])) * jnp.max(jnp.abs(p["d_knorm_g"][l])),
        LOG2E * jnp.max(jnp.abs(p["d_sink"][l])))
    return {
        **stacked, "layer": l,
        "bound_a": bound_a.reshape(1).astype(f32), "bound_c": bound_c.reshape(1).astype(f32),
        "bound_b": bound_b, "bound_d": bound_d.reshape(1).astype(f32),
        "g1": p["norm1_g"][l][None, :], "grow": grow,
        "qag": p["c_qa_norm_g"][l][None, :], "kvag": p["c_kva_norm_g"][l][None, :],
        "wuq": wuq, "wuk": wuk, "wuv": wuv,
        "gqc": pad32(p["c_qnorm_g"][l]) * cscale, "gkc": pad32(p["c_knorm_g"][l]),
        "a_lambda": p["a_lambda"][l], "subln": p["a_subln_g"][l][None, :],
        "d_sink": p["d_sink"][l], "g2": p["norm2_g"][l][None, :],
    }


def _layer(xs, l, lw, tables, out_rows):
    (qa, ka, va, qb0, kb0, vb0, qb1, kb1, vb1, qb2, kb2, vb2, qc, kc, vc, qd, kd, vd) = _proj_call(xs, lw, tables)
    oa = _full_attention(functools.partial(_attn_a_call, qa, ka, va, lw["a_lambda"], lw["subln"],
                                           lambda_init(l)), lw["bound_a"])
    obs, lses = [], []
    for g, (qg, kg, vg) in enumerate(((qb0, kb0, vb0), (qb1, kb1, vb1), (qb2, kb2, vb2))):
        window, dil = B_PATTERNS[g]
        res_shape = qg.shape
        qg, kg, vg = (a.reshape(-1, B_W) for a in (qg, kg, vg))
        bound = lw["bound_b"][g]
        o, lse = _full_attention(functools.partial(
            _band_call, qg, kg, vg, heads=B_HEAD_SPECS, bw=B_W, bq=128, hw=window // (2 * dil),
            seg_len=SEQ // dil, with_lse=True, out_dtype=jnp.float32, name=f"band_b{g}"), bound)
        obs.append(o.reshape(res_shape))
        lses.append(lse.reshape(res_shape))
    oc = _full_attention(functools.partial(_attn_c_call, qc, kc, vc), lw["bound_c"])
    od = _full_attention(functools.partial(
        _band_call, qd, kd, vd, heads=D_HEAD_SPECS, bw=LANES, bq=128, hw=D_WIN, seg_len=SEQ,
        sink=lw["d_sink"], with_lse=False, out_dtype=jnp.bfloat16, name="band_d"), lw["bound_d"])[0]
    x = _merge_call(xs, lw, oa, obs, lses, oc, od)
    return _ffn_call(x, lw, out_rows)


def kernel(x_prompt, x_sample, norm1_g, w_in, w_gate, a_qnorm_g, a_knorm_g, a_lambda, a_subln_g, b_qnorm_g, b_knorm_g, c_qa_norm_g, c_kva_norm_g, c_w_uq, c_w_ukv, c_qnorm_g, c_knorm_g, d_qnorm_g, d_knorm_g, d_sink, w_br_a, w_br_b, w_br_c, w_br_d, w_o, norm2_g, w_ffn_gate, w_ffn_up, w_ffn_down):
    p = dict(norm1_g=norm1_g, w_in=w_in, w_gate=w_gate, a_qnorm_g=a_qnorm_g, a_knorm_g=a_knorm_g,
             a_lambda=a_lambda, a_subln_g=a_subln_g, b_qnorm_g=b_qnorm_g, b_knorm_g=b_knorm_g,
             c_qa_norm_g=c_qa_norm_g, c_kva_norm_g=c_kva_norm_g, c_w_uq=c_w_uq, c_w_ukv=c_w_ukv,
             c_qnorm_g=c_qnorm_g, c_knorm_g=c_knorm_g, d_qnorm_g=d_qnorm_g, d_knorm_g=d_knorm_g,
             d_sink=d_sink, w_br_a=w_br_a, w_br_b=w_br_b, w_br_c=w_br_c, w_br_d=w_br_d, w_o=w_o,
             norm2_g=norm2_g, w_ffn_gate=w_ffn_gate, w_ffn_up=w_ffn_up, w_ffn_down=w_ffn_down)
    xs = [x_prompt.reshape(-1, D_MODEL), x_sample.reshape(-1, D_MODEL)]
    rows = [x.shape[0] for x in xs]
    tables = _tables()
    stacked = _stacked_weights(p)
    for l in range(DEPTH):
        last = l == DEPTH - 1
        xs = _layer(xs, l, _layer_weights(l, p, stacked), tables, rows if last else [sum(rows)])
    return (xs[0].reshape(x_prompt.shape), xs[1].reshape(x_sample.shape))
```

```python
import functools
import math

import jax
import jax.numpy as jnp
import numpy as np
from jax import lax
from jax.experimental import pallas as pl
from jax.experimental.pallas import tpu as pltpu

D_MODEL = 1024
SEQ = 4096
DEPTH = 2
ROPE_THETA = 10000.0
EPS = 1e-6
NEG_INF = -1e30
N_BRANCH = 4
LOG2E = 1.4426950408889634
LN2 = 0.6931471805599453

A_HEADS = 4
A_HD = 64
B_PATTERNS = ((128, 1), (512, 4), (2048, 16))
B_HEADS = 4
B_HD = 64
B_W = B_HEADS * B_HD
C_HEADS = 8
C_Q_RANK = 256
C_KV_RANK = 128
C_NOPE = 64
C_ROPE = 32
C_VD = 64
D_QHEADS = 8
D_KVHEADS = 2
D_HD = 64
D_WIN = 128
D_FF = -(-8 * D_MODEL // (3 * 256)) * 256

IN_SIZES = (512, 512, 512) + (B_W,) * 9 + (C_Q_RANK, C_KV_RANK, C_ROPE, 512, 128, 128)
SPLIT_IDX = tuple(int(i) for i in np.cumsum(IN_SIZES)[:-1])

LANES = 128
MXU_N = 256

OFF_AQ, OFF_AK, OFF_AV = 0, 512, 1024
OFF_B = 1536
OFF_CQ = 3840
OFF_CKV = 4096
OFF_KR = 4224
OFF_DQ = 4352
OFF_DK = 4864
OFF_DV = 4992
N_PROJ = 5120

VMEM_LIMIT = 56 * 1024 * 1024

TM_PROJ = 512
TQ_A = 256
TQ_C = 256
HEADS_PER_STEP_A = 4
HEADS_PER_STEP_C = 4
BAND_UNROLL = 16
MAX_SAFE_BOUND = 50.0
TM_MERGE = 512
TM_FFN = 512
FFN_CHUNKS = ((0, 1024), (1024, 1024), (2048, 768))


def lambda_init(layer):
    return 0.8 - 0.6 * math.exp(-0.3 * layer)


def _const_spec(shape):
    nd = len(shape)
    return pl.BlockSpec(shape, lambda *_: (0,) * nd, pipeline_mode=pl.Buffered(1))


def _layer_spec(shape, layer):
    nd = len(shape)
    return pl.BlockSpec((None,) + tuple(shape), lambda *_: (layer,) + (0,) * nd, pipeline_mode=pl.Buffered(1))


def _params(n_grid):
    return pltpu.CompilerParams(dimension_semantics=("arbitrary",) * n_grid,
                                vmem_limit_bytes=VMEM_LIMIT)


def _dot(a, b):
    return jnp.dot(a, b, preferred_element_type=jnp.float32)


def _dot_nt(a, b):
    return lax.dot_general(a, b, (((1,), (1,)), ((), ())), preferred_element_type=jnp.float32)


def _rms(x, g):
    ms = jnp.mean(x * x, axis=-1, keepdims=True)
    return x * lax.rsqrt(ms + EPS) * g


def _row_specs(row_counts, tm, width):
    if len(row_counts) == 1:
        return [pl.BlockSpec((tm, width), lambda i: (i, 0))]
    nb = row_counts[0] // tm
    return [pl.BlockSpec((tm, width), lambda i: (jnp.minimum(i, nb - 1), 0)),
            pl.BlockSpec((tm, width), lambda i: (jnp.maximum(i - nb, 0), 0))]


def _row_tile(refs, first_blocks):
    if len(refs) == 1:
        return refs[0][...]
    return jnp.where(pl.program_id(0) < first_blocks, refs[0][...], refs[1][...])


def _group_sumsq(p, bd):
    if p.shape[1] < MXU_N:
        return _dot((p * p).astype(jnp.bfloat16), bd[:p.shape[1], :p.shape[1]])
    outs = []
    for c in range(p.shape[1] // MXU_N):
        pc = p[:, c * MXU_N:(c + 1) * MXU_N]
        outs.append(_dot((pc * pc).astype(jnp.bfloat16), bd))
    return outs[0] if len(outs) == 1 else jnp.concatenate(outs, axis=-1)


def _rope_chunks(y, cos, sin, first_half, shift):
    outs = []
    for c in range(y.shape[1] // LANES):
        yc = y[:, c * LANES:(c + 1) * LANES]
        sw = jnp.where(first_half, pltpu.roll(yc, LANES - shift, 1), pltpu.roll(yc, shift, 1))
        outs.append(yc * cos + sw * sin)
    return outs[0] if len(outs) == 1 else jnp.concatenate(outs, axis=-1)


def _proj_kernel(*refs, n_x, first_blocks):
    x_refs, refs = refs[:n_x], refs[n_x:]
    (g1_ref, w_ref, grow_ref, bd_ref, cs_ref, qag_ref, kvag_ref,
     wuq_ref, wuk_ref, wuv_ref, gqc_ref, gkc_ref,
     qa, ka, va, qb0, kb0, vb0, qb1, kb1, vb1, qb2, kb2, vb2,
     qc, kc, vc, qd, kd, vd) = refs[:30]
    res_scr = list(refs[30:])
    tm = x_refs[0].shape[0]
    h = _rms(_row_tile(x_refs, first_blocks), g1_ref[...]).astype(jnp.bfloat16)
    bd64 = bd_ref[0]
    bd128 = bd_ref[1]
    cos64, sin64, cosc, sinc = cs_ref[0], cs_ref[1], cs_ref[2], cs_ref[3]
    lane = lax.broadcasted_iota(jnp.int32, (tm, LANES), 1)
    first64 = (lane & 63) < 32
    firstc = lane < (C_NOPE + C_ROPE // 2)

    def proj(off, width):
        return _dot(h, w_ref[:, off:off + width])

    def store(out_ref, val):
        if len(out_ref.shape) == 2:
            out_ref[...] = val.astype(out_ref.dtype)
            return
        dil, rows = out_ref.shape[1], out_ref.shape[2]
        scr = res_scr.pop()
        n_c = scr.shape[0]
        for c in range(n_c):
            scr[c] = val[:, c * LANES:(c + 1) * LANES]
        for r in range(dil):
            out_ref[0, r] = jnp.concatenate(
                [scr[c, pl.ds(r, rows, stride=dil), :] for c in range(n_c)], axis=-1).astype(out_ref.dtype)

    def norm_rope64(p, off, out_ref):
        ss = _group_sumsq(p, bd64)
        y = p * lax.rsqrt(ss * (1.0 / 64) + EPS) * grow_ref[:, off:off + p.shape[1]]
        store(out_ref, _rope_chunks(y, cos64, sin64, first64, 32))

    b_outs = ((qb0, kb0, vb0), (qb1, kb1, vb1), (qb2, kb2, vb2))
    sections = [(OFF_AQ, 512, qa, True), (OFF_AK, 512, ka, True)]
    for g, (qo, ko, _) in enumerate(b_outs):
        sections += [(OFF_B + g * 768, 256, qo, True), (OFF_B + g * 768 + 256, 256, ko, True)]
    sections += [(OFF_DQ, 512, qd, True), (OFF_DK, 128, kd, True), (OFF_AV, 512, va, False)]
    sections += [(OFF_B + g * 768 + 512, 256, vo, False) for g, (_, _, vo) in enumerate(b_outs)]
    sections += [(OFF_DV, 128, vd, False)]

    cq_raw = proj(OFF_CQ, C_Q_RANK)
    ckv_raw = proj(OFF_CKV, C_KV_RANK)
    kr = proj(OFF_KR, LANES)
    cqn = _rms(cq_raw, qag_ref[...]).astype(jnp.bfloat16)
    ckvn = _rms(ckv_raw, kvag_ref[...]).astype(jnp.bfloat16)
    qfull = _dot(cqn, wuq_ref[...])
    kfull = _dot(ckvn, wuk_ref[...]) + jnp.concatenate([kr] * C_HEADS, axis=-1)
    vc[...] = _dot(ckvn, wuv_ref[...]).astype(vc.dtype)
    p_next = proj(sections[0][0], sections[0][1])
    qn = qfull * lax.rsqrt(_group_sumsq(qfull, bd128) * (1.0 / 96) + EPS) * gqc_ref[...]
    kn = kfull * lax.rsqrt(_group_sumsq(kfull, bd128) * (1.0 / 96) + EPS) * gkc_ref[...]
    qc[...] = _rope_chunks(qn, cosc, sinc, firstc, C_ROPE // 2).astype(qc.dtype)
    kc[...] = _rope_chunks(kn, cosc, sinc, firstc, C_ROPE // 2).astype(kc.dtype)

    for n, (off, _, out_ref, roped) in enumerate(sections):
        p = p_next
        if n + 1 < len(sections):
            p_next = proj(sections[n + 1][0], sections[n + 1][1])
        if roped:
            norm_rope64(p, off, out_ref)
        else:
            store(out_ref, p)


def _proj_call(xs, lw, tables):
    rows = [x.shape[0] for x in xs]
    t = sum(rows)
    tm = TM_PROJ
    n_pos = SEQ // tm
    widths = (512, 512, 512) + (256,) * 9 + (1024, 1024, 512, 512, 128, 128)
    out_shape = [jax.ShapeDtypeStruct((t, w), jnp.bfloat16) for w in widths]
    out_specs = [pl.BlockSpec((tm, w), lambda i: (i, 0)) for w in widths]
    for g, (_, dil) in enumerate(B_PATTERNS):
        if dil > 1:
            for j in range(3 + 3 * g, 6 + 3 * g):
                out_shape[j] = jax.ShapeDtypeStruct((t // SEQ, dil, SEQ // dil, B_W), jnp.bfloat16)
                out_specs[j] = pl.BlockSpec((1, dil, tm // dil, B_W), lambda i: (i // n_pos, 0, i % n_pos, 0))
    n_res = 3 * sum(1 for _, dil in B_PATTERNS if dil > 1)
    in_specs = _row_specs(rows, tm, D_MODEL) + [
        _const_spec((1, D_MODEL)),
        _layer_spec((D_MODEL, N_PROJ), lw["layer"]),
        _const_spec((1, N_PROJ)),
        _const_spec((2, MXU_N, MXU_N)),
        pl.BlockSpec((4, tm, LANES), lambda i: (0, i % n_pos, 0)),
        _const_spec((1, C_Q_RANK)),
        _const_spec((1, C_KV_RANK)),
        _const_spec((C_Q_RANK, 1024)),
        _const_spec((C_KV_RANK, 1024)),
        _const_spec((C_KV_RANK, 512)),
        _const_spec((1, 1024)),
        _const_spec((1, 1024)),
    ]
    return pl.pallas_call(
        functools.partial(_proj_kernel, n_x=len(xs), first_blocks=rows[0] // tm),
        grid=(t // tm,), in_specs=in_specs, out_specs=out_specs, out_shape=out_shape,
        scratch_shapes=[pltpu.VMEM((B_W // LANES, tm, LANES), jnp.float32)] * n_res,
        compiler_params=_params(1), name="proj",
    )(*xs, lw["g1"], lw["w_in"], lw["grow"], tables["bd"], tables["cs"], lw["qag"], lw["kvag"],
      lw["wuq"], lw["wuk"], lw["wuv"], lw["gqc"], lw["gkc"])


def _lane_fold(x, op):
    acc = x[:, :LANES]
    for t in range(1, x.shape[1] // LANES):
        acc = op(acc, x[:, t * LANES:(t + 1) * LANES])
    return acc


def _online_step(state, s, v):
    r = jnp.max(_lane_fold(s, jnp.maximum), axis=-1, keepdims=True)
    if state is None:
        e = jnp.exp2(s - r)
        return r, _lane_fold(e, jnp.add), _dot(e.astype(jnp.bfloat16), v)
    m, l, acc = state
    m_new = jnp.maximum(m, r)
    alpha = jnp.exp2(m - m_new)
    e = jnp.exp2(s - m_new)
    return m_new, alpha * l + _lane_fold(e, jnp.add), alpha * acc + _dot(e.astype(jnp.bfloat16), v)


def _online_finish(state):
    _, l, acc = state
    return acc * (1.0 / jnp.sum(l, axis=-1, keepdims=True))


def _attn_a_kernel(lam_ref, q_ref, k_ref, v_ref, g_ref, o_ref, *, lam_init, bounded):
    lp = lam_ref[...]
    lam = (jnp.exp(jnp.sum(lp[0:1] * lp[1:2], axis=-1, keepdims=True))
           - jnp.exp(jnp.sum(lp[2:3] * lp[3:4], axis=-1, keepdims=True)) + lam_init)
    lane = lax.broadcasted_iota(jnp.int32, (q_ref.shape[0], LANES), 1)
    n_heads = q_ref.shape[1] // LANES

    def scores(h):
        hs = slice(h * LANES, (h + 1) * LANES)
        q = q_ref[:, hs]
        k = k_ref[:, hs]
        return [_dot_nt(jnp.where((lane >= c * A_HD) & (lane < (c + 1) * A_HD), q, jnp.zeros_like(q)), k)
                for c in range(2)]

    def finish(h, o):
        o = _rms(o, g_ref[...]) * (1.0 - lam_init)
        o_ref[:, h * LANES:(h + 1) * LANES] = o.astype(o_ref.dtype)

    if bounded:
        def exps(h):
            es, ls = [], []
            for s in scores(h):
                e = jnp.exp2(s)
                ls.append(jnp.sum(_lane_fold(e, jnp.add), axis=-1, keepdims=True))
                es.append(e)
            return es, ls

        nxt = exps(0)
        for h in range(n_heads):
            (e0, e1), (l0, l1) = nxt
            if h + 1 < n_heads:
                nxt = exps(h + 1)
            w = e0 - (lam * l0 / l1) * e1
            finish(h, _dot(w.astype(jnp.bfloat16), v_ref[:, h * LANES:(h + 1) * LANES]) * (1.0 / l0))
        return

    s_next = scores(0)
    for h in range(n_heads):
        s_cur = s_next
        if h + 1 < n_heads:
            s_next = scores(h + 1)
        es, inv = [], []
        for s in s_cur:
            e = jnp.exp2(s - jnp.max(s, axis=-1, keepdims=True))
            es.append(e)
            inv.append(1.0 / jnp.sum(e, axis=-1, keepdims=True))
        w = es[0] * inv[0] - es[1] * (lam * inv[1])
        finish(h, _dot(w.astype(jnp.bfloat16), v_ref[:, h * LANES:(h + 1) * LANES]))


def _attn_a_call(q, k, v, a_lambda, subln_row, lam_init, bounded):
    t = q.shape[0]
    nseq = t // SEQ
    tq = TQ_A
    nq = SEQ // tq
    w = HEADS_PER_STEP_A * LANES
    return pl.pallas_call(
        functools.partial(_attn_a_kernel, lam_init=lam_init, bounded=bounded),
        grid=(nseq, A_HEADS // HEADS_PER_STEP_A, nq),
        in_specs=[
            _const_spec((4, A_HD)),
            pl.BlockSpec((tq, w), lambda s, h, i: (s * nq + i, h)),
            pl.BlockSpec((SEQ, w), lambda s, h, i: (s, h)),
            pl.BlockSpec((SEQ, w), lambda s, h, i: (s, h)),
            _const_spec((1, LANES)),
        ],
        out_specs=pl.BlockSpec((tq, w), lambda s, h, i: (s * nq + i, h)),
        out_shape=jax.ShapeDtypeStruct((t, A_HEADS * LANES), jnp.bfloat16),
        compiler_params=_params(3), name="attn_a",
    )(a_lambda, q, k, v, subln_row)


def _attn_c_kernel(q_ref, k_ref, v_ref, o_ref, *, bounded):
    lane = lax.broadcasted_iota(jnp.int32, (o_ref.shape[0], LANES), 1)
    n_heads = q_ref.shape[1] // LANES

    def scores(h):
        hs = slice(h * LANES, (h + 1) * LANES)
        return _dot_nt(q_ref[:, hs], k_ref[:, hs])

    outs = []
    if bounded:
        for h in range(n_heads):
            e = jnp.exp2(scores(h))
            l = jnp.sum(_lane_fold(e, jnp.add), axis=-1, keepdims=True)
            outs.append(_dot(e.astype(jnp.bfloat16), v_ref[:, (h // 2) * LANES:(h // 2 + 1) * LANES]) * (1.0 / l))
    else:
        s_next = scores(0)
        for h in range(n_heads):
            s = s_next
            if h + 1 < n_heads:
                s_next = scores(h + 1)
            v = v_ref[:, (h // 2) * LANES:(h // 2 + 1) * LANES]
            half = s.shape[0] // 2
            parts = [_online_step(None, s[r:r + half], v) for r in (0, half)]
            outs.append(_online_finish(tuple(jnp.concatenate([a, b], axis=0) for a, b in zip(*parts))))
    for j in range(n_heads // 2):
        o_ref[:, j * LANES:(j + 1) * LANES] = jnp.where(
            lane < C_VD, outs[2 * j], outs[2 * j + 1]).astype(o_ref.dtype)


def _attn_c_call(q, k, v, bounded):
    t = q.shape[0]
    nseq = t // SEQ
    tq = TQ_C
    nq = SEQ // tq
    hps = HEADS_PER_STEP_C
    return pl.pallas_call(
        functools.partial(_attn_c_kernel, bounded=bounded),
        grid=(nseq, C_HEADS // hps, nq),
        in_specs=[
            pl.BlockSpec((tq, hps * LANES), lambda s, j, i: (s * nq + i, j)),
            pl.BlockSpec((SEQ, hps * LANES), lambda s, j, i: (s, j)),
            pl.BlockSpec((SEQ, hps * C_VD), lambda s, j, i: (s, j)),
        ],
        out_specs=pl.BlockSpec((tq, hps * C_VD), lambda s, j, i: (s * nq + i, j)),
        out_shape=jax.ShapeDtypeStruct((t, C_HEADS * C_VD), jnp.bfloat16),
        compiler_params=_params(3), name="attn_c",
    )(q, k, v)


def _full_attention(call, bound):
    return lax.cond(bound[0] <= MAX_SAFE_BOUND, lambda: call(True), lambda: call(False))


def _band_kernel(*refs, heads, bw, bq, hw, seg_len, with_sink, with_lse, bounded):
    if with_sink:
        sink_ref, q_ref, k_ref, v_ref = refs[:4]
        outs = refs[4:]
    else:
        q_ref, k_ref, v_ref = refs[:3]
        outs = refs[3:]
    o_ref = outs[0]
    lse_ref = outs[1] if with_lse else None
    ch = q_ref.shape[0]
    win = bq + 2 * hw
    shift = int(math.log2(seg_len))
    lane = lax.broadcasted_iota(jnp.int32, (bq, bw), 1)
    lane128 = lax.broadcasted_iota(jnp.int32, (bq, LANES), 1)
    assert seg_len % bq == 0 and seg_len >= win and B_HD == D_HD
    row_minus_col = (lax.broadcasted_iota(jnp.int32, (bq, win), 0)
                     - lax.broadcasted_iota(jnp.int32, (bq, win), 1))
    k_offs = sorted({hd[2] for hd in heads})

    def body(i, carry):
        q0 = pl.multiple_of(i * bq, bq)
        seg_lo = (q0 >> shift) << shift
        ws = pl.multiple_of(jnp.clip(q0 - hw, seg_lo, seg_lo + seg_len - win), hw)
        d = row_minus_col + (q0 - ws)
        bias = jnp.where((d <= hw) & (d >= -hw), 0.0, NEG_INF)
        qb = q_ref[pl.ds(q0, bq), :]
        kw = k_ref[pl.ds(ws, win), :]
        vw = v_ref[pl.ds(ws, win), :]
        acc, lacc = {}, {}
        for ko in k_offs:
            group = [hd for hd in heads if hd[2] == ko]
            qs = jnp.concatenate(
                [jnp.where((lane >= mo) & (lane < mo + B_HD), qb[:, qo:qo + bw], jnp.zeros((bq, bw), qb.dtype))
                 for qo, mo, _, _ in group], axis=0)
            s_all = _dot_nt(qs, kw[:, ko:ko + bw])
            es, ms, ls = [], [], []
            for n, (_, _, _, hid) in enumerate(group):
                s = s_all[n * bq:(n + 1) * bq] + bias
                if with_sink:
                    sk = sink_ref[hid] * LOG2E
                if bounded:
                    m = 0.0
                    e = jnp.exp2(s)
                    l = jnp.sum(e, axis=-1, keepdims=True)
                    if with_sink:
                        l = l + jnp.exp2(jnp.full((bq, 1), sk, jnp.float32))
                else:
                    m = jnp.max(s, axis=-1, keepdims=True)
                    if with_sink:
                        m = jnp.maximum(m, sk)
                    e = jnp.exp2(s - m)
                    l = jnp.sum(e, axis=-1, keepdims=True)
                    if with_sink:
                        l = l + jnp.exp2(sk - m)
                es.append(e.astype(jnp.bfloat16))
                ms.append(m)
                ls.append(l)
            for half in range(bw // LANES):
                sub = [n for n, (_, mo, _, _) in enumerate(group) if mo // LANES == half]
                o_sub = _dot(jnp.concatenate([es[n] for n in sub], axis=0),
                             vw[:, ko + half * LANES:ko + (half + 1) * LANES])
                for j, n in enumerate(sub):
                    qo, mo = group[n][0], group[n][1] % LANES
                    hm = (lane128 >= mo) & (lane128 < mo + B_HD)
                    key = (qo, half)
                    o = o_sub[j * bq:(j + 1) * bq] * (1.0 / ls[n])
                    acc[key] = jnp.where(hm, o, acc[key]) if key in acc else o
                    if with_lse:
                        lse = jnp.broadcast_to((ms[n] + jnp.log2(ls[n])) * LN2, (bq, LANES))
                        lacc[key] = jnp.where(hm, lse, lacc[key]) if key in lacc else lse
        for (qo, half), val in acc.items():
            lanes = slice(qo + half * LANES, qo + (half + 1) * LANES)
            o_ref[pl.ds(q0, bq), lanes] = val.astype(o_ref.dtype)
            if with_lse:
                lse_ref[pl.ds(q0, bq), lanes] = lacc[(qo, half)]
        return carry

    lax.fori_loop(0, ch // bq, body, 0, unroll=BAND_UNROLL)


def _band_call(q, k, v, bounded, *, heads, bw, bq, hw, seg_len, sink=None, with_lse, out_dtype, name):
    t, wq = q.shape
    wk = k.shape[1]
    ch = SEQ
    kern = functools.partial(_band_kernel, heads=heads, bw=bw, bq=bq, hw=hw, seg_len=seg_len,
                             with_sink=sink is not None, with_lse=with_lse, bounded=bounded)
    in_specs = [
        pl.BlockSpec((ch, wq), lambda i: (i, 0)),
        pl.BlockSpec((ch, wk), lambda i: (i, 0)),
        pl.BlockSpec((ch, wk), lambda i: (i, 0)),
    ]
    args = [q, k, v]
    if sink is not None:
        in_specs = [pl.BlockSpec(memory_space=pltpu.SMEM)] + in_specs
        args = [sink] + args
    out_shape = [jax.ShapeDtypeStruct((t, wq), out_dtype)]
    out_specs = [pl.BlockSpec((ch, wq), lambda i: (i, 0))]
    if with_lse:
        out_shape.append(jax.ShapeDtypeStruct((t, wq), jnp.float32))
        out_specs.append(pl.BlockSpec((ch, wq), lambda i: (i, 0)))
    return pl.pallas_call(
        kern, grid=(t // ch,), in_specs=in_specs, out_specs=out_specs, out_shape=out_shape,
        compiler_params=_params(1), name=name,
    )(*args)


B_HEAD_SPECS = tuple((0, h * B_HD, 0, h) for h in range(B_HEADS))
D_REP = D_QHEADS // D_KVHEADS
D_HEAD_SPECS = tuple((r * LANES, g * D_HD, 0, g * D_REP + r) for g in range(D_KVHEADS) for r in range(D_REP))


def _sigmoid(z):
    return 1.0 / (1.0 + jnp.exp(-z))


def _merge_kernel(*refs, n_x, first_blocks):
    x_refs, refs = refs[:n_x], refs[n_x:]
    (g1_ref, wg_ref, oa_ref, ob0, ob1, ob2, ls0, ls1, ls2, oc_ref, od_ref,
     wa_ref, wb_ref, wc_ref, wd_ref, wo_ref, out_ref) = refs[:17]
    res_scr = list(refs[17:])

    def load(ref):
        if len(ref.shape) == 2:
            return ref[...]
        dil, rows = ref.shape[1], ref.shape[2]
        scr = res_scr.pop()
        n_c = scr.shape[0]
        for r in range(dil):
            for c in range(n_c):
                scr[c, pl.ds(r, rows, stride=dil), :] = ref[0, r, :, c * LANES:(c + 1) * LANES]
        return jnp.concatenate([scr[c] for c in range(n_c)], axis=-1)

    x = _row_tile(x_refs, first_blocks)
    h = _rms(x, g1_ref[...]).astype(jnp.bfloat16)
    l0, l1, l2 = load(ls0), load(ls1), load(ls2)
    lm = jnp.maximum(jnp.maximum(l0, l1), l2)
    e0, e1, e2 = jnp.exp(l0 - lm), jnp.exp(l1 - lm), jnp.exp(l2 - lm)
    den = e0 + e1 + e2
    ob = ((e0 / den) * load(ob0) + (e1 / den) * load(ob1) + (e2 / den) * load(ob2)).astype(jnp.bfloat16)
    branches = ((oa_ref[...], wa_ref), (ob, wb_ref), (oc_ref[...], wc_ref), (od_ref[...], wd_ref))
    merged = None
    for i, (o, w_ref) in enumerate(branches):
        gate = _sigmoid(_dot(h, wg_ref[:, i * D_MODEL:(i + 1) * D_MODEL]))
        term = gate * _dot(o, w_ref[...])
        merged = term if merged is None else merged + term
    out_ref[...] = x + _dot(merged.astype(jnp.bfloat16), wo_ref[...])


def _merge_call(xs, lw, oa, obs, lses, oc, od):
    rows = [x.shape[0] for x in xs]
    t = sum(rows)
    tm = TM_MERGE

    def tile(w):
        return pl.BlockSpec((tm, w), lambda i: (i, 0))

    n_pos = SEQ // tm

    def band_tile(dil):
        if dil == 1:
            return tile(B_W)
        return pl.BlockSpec((1, dil, tm // dil, B_W), lambda i: (i // n_pos, 0, i % n_pos, 0))

    b_specs = [band_tile(dil) for _, dil in B_PATTERNS]
    in_specs = _row_specs(rows, tm, D_MODEL) + [
                _const_spec((1, D_MODEL)), _layer_spec((D_MODEL, N_BRANCH * D_MODEL), lw["layer"]),
                tile(512)] + b_specs + b_specs + [tile(512), tile(512)] + [
                _layer_spec((w, D_MODEL), lw["layer"]) for w in (512, 256, 512, 512, D_MODEL)]
    n_res = 2 * sum(1 for _, dil in B_PATTERNS if dil > 1)
    return pl.pallas_call(
        functools.partial(_merge_kernel, n_x=len(xs), first_blocks=rows[0] // tm),
        grid=(t // tm,), in_specs=in_specs, out_specs=tile(D_MODEL),
        out_shape=jax.ShapeDtypeStruct((t, D_MODEL), jnp.float32),
        scratch_shapes=[pltpu.VMEM((B_W // LANES, tm, LANES), jnp.float32)] * n_res,
        compiler_params=_params(1), name="merge",
    )(*xs, lw["g1"], lw["w_gate"], oa, *obs, *lses, oc, od,
      lw["w_br_a"], lw["w_br_b"], lw["w_br_c"], lw["w_br_d"], lw["w_o"])


def _ffn_kernel(x_ref, g2_ref, wg_ref, wu_ref, wd_ref, *out_refs, first_blocks):
    x = x_ref[...]
    hf = _rms(x, g2_ref[...]).astype(jnp.bfloat16)
    acc = x
    for off, width in FFN_CHUNKS:
        a = _dot(hf, wg_ref[:, off:off + width])
        u = _dot(hf, wu_ref[:, off:off + width])
        act = (a * _sigmoid(a) * u).astype(jnp.bfloat16)
        acc = acc + _dot(act, wd_ref[off:off + width, :])
    if len(out_refs) == 1:
        out_refs[0][...] = acc
    else:
        @pl.when(pl.program_id(0) < first_blocks)
        def _():
            out_refs[0][...] = acc

        @pl.when(pl.program_id(0) >= first_blocks)
        def _():
            out_refs[1][...] = acc


def _ffn_call(x, lw, out_rows):
    t = x.shape[0]
    tm = TM_FFN
    tile = pl.BlockSpec((tm, D_MODEL), lambda i: (i, 0))
    return pl.pallas_call(
        functools.partial(_ffn_kernel, first_blocks=out_rows[0] // tm), grid=(t // tm,),
        in_specs=[tile, _const_spec((1, D_MODEL)), _layer_spec((D_MODEL, D_FF), lw["layer"]),
                  _layer_spec((D_MODEL, D_FF), lw["layer"]), _layer_spec((D_FF, D_MODEL), lw["layer"])],
        out_specs=_row_specs(out_rows, tm, D_MODEL),
        out_shape=[jax.ShapeDtypeStruct((r, D_MODEL), jnp.float32) for r in out_rows],
        compiler_params=_params(1), name="ffn",
    )(x, lw["g2"], lw["w_ffn_gate"], lw["w_ffn_up"], lw["w_ffn_down"])


def _tables():
    pos = jnp.arange(SEQ, dtype=jnp.float32)[:, None]
    lane = jnp.arange(LANES)
    inv64 = jnp.power(ROPE_THETA, -jnp.arange(32, dtype=jnp.float32) / 32)
    ang64 = pos * inv64[lane % 32][None, :]
    sign64 = jnp.where((lane % 64) < 32, -1.0, 1.0)[None, :]
    cos64 = jnp.cos(ang64)
    sin64 = jnp.sin(ang64) * sign64
    invc = jnp.power(ROPE_THETA, -jnp.arange(16, dtype=jnp.float32) / 16)
    angc = pos * invc[lane % 16][None, :]
    is_rope = ((lane >= C_NOPE) & (lane < C_NOPE + C_ROPE))[None, :]
    signc = jnp.where(lane < C_NOPE + C_ROPE // 2, -1.0, 1.0)[None, :]
    cosc = jnp.where(is_rope, jnp.cos(angc), 1.0)
    sinc = jnp.where(is_rope, jnp.sin(angc) * signc, 0.0)
    cs = jnp.stack([cos64, sin64, cosc, sinc]).astype(jnp.float32)
    idx = np.arange(MXU_N)
    bd = np.stack([(idx[:, None] // 64) == (idx[None, :] // 64),
                   (idx[:, None] // 128) == (idx[None, :] // 128)]).astype(np.float32)
    return {"cs": cs, "bd": jnp.asarray(bd, dtype=jnp.bfloat16)}


def _stacked_weights(p):
    bf = jnp.bfloat16
    cols = jnp.split(p["w_in"], SPLIT_IDX, axis=-1)
    zeros = lambda n: jnp.zeros(p["w_in"].shape[:2] + (n,), jnp.float32)
    d_order = np.array([g * D_REP + r for r in range(D_REP) for g in range(D_KVHEADS)])
    by_head = lambda w, axis: jnp.take(w.reshape(w.shape[:axis] + (D_QHEADS, D_HD) + w.shape[axis + 1:]),
                                       d_order, axis=axis).reshape(w.shape)
    w_in = jnp.concatenate(
        list(cols[0:12]) + [cols[12], cols[13], zeros(64), cols[14], zeros(32), by_head(cols[15], 2),
                            cols[16], cols[17]], axis=-1).astype(bf)
    names = ("w_gate", "w_br_a", "w_br_b", "w_br_c", "w_o", "w_ffn_gate", "w_ffn_up", "w_ffn_down")
    return {"w_in": w_in, "w_br_d": by_head(p["w_br_d"], 1).astype(bf), **{n: p[n].astype(bf) for n in names}}


def _layer_weights(l, p, stacked):
    bf = jnp.bfloat16
    f32 = jnp.float32
    ones = lambda n: jnp.ones((n,), f32)
    qs = A_HD ** -0.5 * LOG2E
    grow = jnp.concatenate(
        [jnp.tile(p["a_qnorm_g"][l], 8) * qs, jnp.tile(p["a_knorm_g"][l], 8), ones(512)]
        + sum([[jnp.tile(p["b_qnorm_g"][l, g], 4) * qs, jnp.tile(p["b_knorm_g"][l, g], 4), ones(256)]
               for g in range(3)], [])
        + [ones(512), jnp.tile(p["d_qnorm_g"][l], 8) * qs, jnp.tile(p["d_knorm_g"][l], 2), ones(128)]
    )[None, :].astype(f32)
    wuq = p["c_w_uq"][l].reshape(C_Q_RANK, C_HEADS, C_NOPE + C_ROPE)
    wuq = jnp.pad(wuq, ((0, 0), (0, 0), (0, 32))).reshape(C_Q_RANK, C_HEADS * LANES).astype(bf)
    wukv = p["c_w_ukv"][l].reshape(C_KV_RANK, C_HEADS, C_NOPE + C_VD)
    wuk = jnp.pad(wukv[:, :, :C_NOPE], ((0, 0), (0, 0), (0, 64))).reshape(C_KV_RANK, C_HEADS * LANES).astype(bf)
    wuv = wukv[:, :, C_NOPE:].reshape(C_KV_RANK, C_HEADS * C_VD).astype(bf)
    cscale = (C_NOPE + C_ROPE) ** -0.5 * LOG2E
    pad32 = lambda g: jnp.tile(jnp.pad(g, (0, 32)), C_HEADS)[None, :].astype(f32)
    slack = (1.0 + 2.0 ** -8) ** 2
    bound_a = (A_HD * qs * slack * jnp.max(jnp.abs(p["a_qnorm_g"][l])) * jnp.max(jnp.abs(p["a_knorm_g"][l])))
    bound_c = ((C_NOPE + C_ROPE) * cscale * slack
               * jnp.max(jnp.abs(p["c_qnorm_g"][l])) * jnp.max(jnp.abs(p["c_knorm_g"][l])))
    bound_b = [(B_HD * qs * slack * jnp.max(jnp.abs(p["b_qnorm_g"][l, g]))
                * jnp.max(jnp.abs(p["b_knorm_g"][l, g]))).reshape(1).astype(f32) for g in range(len(B_PATTERNS))]
    bound_d = jnp.maximum(
        D_HD * qs * slack * jnp.max(jnp.abs(p["d_qnorm_g"][l])) * jnp.max(jnp.abs(p["d_knorm_g"][l])),
        LOG2E * jnp.max(jnp.abs(p["d_sink"][l])))
    return {
        **stacked, "layer": l,
        "bound_a": bound_a.reshape(1).astype(f32), "bound_c": bound_c.reshape(1).astype(f32),
        "bound_b": bound_b, "bound_d": bound_d.reshape(1).astype(f32),
        "g1": p["norm1_g"][l][None, :], "grow": grow,
        "qag": p["c_qa_norm_g"][l][None, :], "kvag": p["c_kva_norm_g"][l][None, :],
        "wuq": wuq, "wuk": wuk, "wuv": wuv,
        "gqc": pad32(p["c_qnorm_g"][l]) * cscale, "gkc": pad32(p["c_knorm_g"][l]),
        "a_lambda": p["a_lambda"][l], "subln": p["a_subln_g"][l][None, :],
        "d_sink": p["d_sink"][l], "g2": p["norm2_g"][l][None, :],
    }


def _layer(xs, l, lw, tables, out_rows):
    (qa, ka, va, qb0, kb0, vb0, qb1, kb1, vb1, qb2, kb2, vb2, qc, kc, vc, qd, kd, vd) = _proj_call(xs, lw, tables)
    oa = _full_attention(functools.partial(_attn_a_call, qa, ka, va, lw["a_lambda"], lw["subln"],
                                           lambda_init(l)), lw["bound_a"])
    obs, lses = [], []
    for g, (qg, kg, vg) in enumerate(((qb0, kb0, vb0), (qb1, kb1, vb1), (qb2, kb2, vb2))):
        window, dil = B_PATTERNS[g]
        res_shape = qg.shape
        qg, kg, vg = (a.reshape(-1, B_W) for a in (qg, kg, vg))
        bound = lw["bound_b"][g]
        o, lse = _full_attention(functools.partial(
            _band_call, qg, kg, vg, heads=B_HEAD_SPECS, bw=B_W, bq=128, hw=window // (2 * dil),
            seg_len=SEQ // dil, with_lse=True, out_dtype=jnp.float32, name=f"band_b{g}"), bound)
        obs.append(o.reshape(res_shape))
        lses.append(lse.reshape(res_shape))
    oc = _full_attention(functools.partial(_attn_c_call, qc, kc, vc), lw["bound_c"])
    od = _full_attention(functools.partial(
        _band_call, qd, kd, vd, heads=D_HEAD_SPECS, bw=LANES, bq=128, hw=D_WIN, seg_len=SEQ,
        sink=lw["d_sink"], with_lse=False, out_dtype=jnp.bfloat16, name="band_d"), lw["bound_d"])[0]
    x = _merge_call(xs, lw, oa, obs, lses, oc, od)
    return _ffn_call(x, lw, out_rows)


def kernel(x_prompt, x_sample, norm1_g, w_in, w_gate, a_qnorm_g, a_knorm_g, a_lambda, a_subln_g, b_qnorm_g, b_knorm_g, c_qa_norm_g, c_kva_norm_g, c_w_uq, c_w_ukv, c_qnorm_g, c_knorm_g, d_qnorm_g, d_knorm_g, d_sink, w_br_a, w_br_b, w_br_c, w_br_d, w_o, norm2_g, w_ffn_gate, w_ffn_up, w_ffn_down):
    p = dict(norm1_g=norm1_g, w_in=w_in, w_gate=w_gate, a_qnorm_g=a_qnorm_g, a_knorm_g=a_knorm_g,
             a_lambda=a_lambda, a_subln_g=a_subln_g, b_qnorm_g=b_qnorm_g, b_knorm_g=b_knorm_g,
             c_qa_norm_g=c_qa_norm_g, c_kva_norm_g=c_kva_norm_g, c_w_uq=c_w_uq, c_w_ukv=c_w_ukv,
             c_qnorm_g=c_qnorm_g, c_knorm_g=c_knorm_g, d_qnorm_g=d_qnorm_g, d_knorm_g=d_knorm_g,
             d_sink=d_sink, w_br_a=w_br_a, w_br_b=w_br_b, w_br_c=w_br_c, w_br_d=w_br_d, w_o=w_o,
             norm2_g=norm2_g, w_ffn_gate=w_ffn_gate, w_ffn_up=w_ffn_up, w_ffn_down=w_ffn_down)
    xs = [x_prompt.reshape(-1, D_MODEL), x_sample.reshape(-1, D_MODEL)]
    rows = [x.shape[0] for x in xs]
    tables = _tables()
    stacked = _stacked_weights(p)
    for l in range(DEPTH):
        last = l == DEPTH - 1
        xs = _layer(xs, l, _layer_weights(l, p, stacked), tables, rows if last else [sum(rows)])
    return (xs[0].reshape(x_prompt.shape), xs[1].reshape(x_sample.shape))
```
